```python
import jax, jax.numpy as jnp
from jax import lax
import numpy as np

D_MODEL = 1024
BATCH = 8
SEQ = 4096
DEPTH = 2

HEAD_DIM = 64
N_RET_HEADS = D_MODEL // (2 * HEAD_DIM)
N_SB_HEADS = D_MODEL // (2 * HEAD_DIM)
N_FOX_HEADS = D_MODEL // HEAD_DIM
RET_WIDTH = N_RET_HEADS * HEAD_DIM
SB_WIDTH = N_SB_HEADS * HEAD_DIM
FOX_WIDTH = N_FOX_HEADS * HEAD_DIM
EVEN_MIX_WIDTH = RET_WIDTH + SB_WIDTH
EVEN_IN_SIZES = (RET_WIDTH, RET_WIDTH, RET_WIDTH, RET_WIDTH, SB_WIDTH, SB_WIDTH, SB_WIDTH)
EVEN_IN_WIDTH = sum(EVEN_IN_SIZES)
ODD_IN_SIZES = (FOX_WIDTH, FOX_WIDTH, FOX_WIDTH, N_FOX_HEADS)
ODD_IN_WIDTH = sum(ODD_IN_SIZES)
D_FF = ((8 * D_MODEL // 3 + 127) // 128) * 128
N_EXPERTS = 8
TOP_K = 2
BLOCK = 128
CHUNK = 128
ROPE_BASE = 10000.0
NORM_EPS = 1e-6
GROUP_NORM_EPS = 1e-5
FORGET_BIAS_CENTER = 2.0
N_EVEN = (DEPTH + 1) // 2
N_ODD = DEPTH // 2

kernel_name = "hybrid_retention_stickbreak_fox_moe"


def _split_points(sizes):
    return [int(v) for v in np.cumsum(sizes)[:-1]]


def _rms_norm(x, gain):
    xf = x.astype(jnp.float32)
    y = xf * lax.rsqrt(jnp.mean(xf * xf, axis=-1, keepdims=True) + NORM_EPS)
    return (y * gain.astype(jnp.float32)).astype(x.dtype)


def _split_heads(t, n_heads):
    b, s, _ = t.shape
    return t.reshape(b, s, n_heads, HEAD_DIM).transpose(0, 2, 1, 3)


def _merge_heads(t):
    b, h, s, d = t.shape
    return t.transpose(0, 2, 1, 3).reshape(b, s, h * d)


def _rotary(t):
    s, d = t.shape[2], t.shape[3]
    half = d // 2
    inv_freq = ROPE_BASE ** (-jnp.arange(half, dtype=jnp.float32) / half)
    ang = jnp.arange(s, dtype=jnp.float32)[:, None] * inv_freq[None, :]
    cos, sin = jnp.cos(ang), jnp.sin(ang)
    t1, t2 = t[..., :half], t[..., half:]
    return jnp.concatenate([t1 * cos - t2 * sin, t1 * sin + t2 * cos], axis=-1)


def _retention_chunkwise(q, k, v):
    b, h, s, d = q.shape
    nc = s // CHUNK
    log_gamma = jnp.log(1.0 - 2.0 ** (-5.0 - jnp.arange(h, dtype=jnp.float32)))
    pos = jnp.arange(CHUNK, dtype=jnp.float32)
    diff = pos[:, None] - pos[None, :]
    intra_decay = jnp.where(diff >= 0.0,
                            jnp.exp(log_gamma[:, None, None] * jnp.maximum(diff, 0.0)),
                            0.0)
    q_decay = jnp.exp(log_gamma[:, None] * (pos + 1.0))
    k_decay = jnp.exp(log_gamma[:, None] * (CHUNK - 1.0 - pos))
    chunk_decay = jnp.exp(log_gamma * CHUNK)
    qc = q.reshape(b, h, nc, CHUNK, d)
    kc = k.reshape(b, h, nc, CHUNK, d)
    vc = v.reshape(b, h, nc, CHUNK, d)
    scores = jnp.einsum("bhnqd,bhnkd->bhnqk", qc, kc) * intra_decay[None, :, None]
    inner = jnp.einsum("bhnqk,bhnke->bhnqe", scores, vc)
    kv = jnp.einsum("bhnkd,bhnke->nbhde", kc * k_decay[None, :, None, :, None], vc)

    def step(state, kv_n):
        return state * chunk_decay[None, :, None, None] + kv_n, state

    _, states = lax.scan(step, jnp.zeros((b, h, d, d), jnp.float32), kv)
    cross = jnp.einsum("bhnqd,nbhde->bhnqe", qc * q_decay[None, :, None, :, None], states)
    return (inner + cross).reshape(b, h, s, d)


def _stick_breaking_attention(q, k, v):
    b, h, s, d = q.shape
    scale = d ** -0.5
    outs = []
    for blk in range(s // BLOCK):
        lo, hi = blk * BLOCK, (blk + 1) * BLOCK
        z = jnp.einsum("bhqd,bhkd->bhqk", q[:, :, lo:hi], k[:, :, :hi]) * scale
        q_pos = lo + jnp.arange(BLOCK)
        k_pos = jnp.arange(hi)
        strict = k_pos[None, :] < q_pos[:, None]
        log_fail = jnp.where(strict, jax.nn.log_sigmoid(-z), 0.0)
        later_fail = lax.cumsum(log_fail, axis=3, reverse=True) - log_fail
        weights = jnp.where(strict, jnp.exp(jax.nn.log_sigmoid(z) + later_fail), 0.0)
        outs.append(jnp.einsum("bhqk,bhkd->bhqd", weights, v[:, :, :hi]))
    return jnp.concatenate(outs, axis=2)


def _forgetting_attention(q, k, v, log_f):
    b, h, s, d = q.shape
    scale = d ** -0.5
    cum = jnp.cumsum(log_f, axis=-1)
    outs = []
    for blk in range(s // BLOCK):
        lo, hi = blk * BLOCK, (blk + 1) * BLOCK
        z = jnp.einsum("bhqd,bhkd->bhqk", q[:, :, lo:hi], k[:, :, :hi]) * scale
        logits = z + cum[:, :, lo:hi, None] - cum[:, :, None, :hi]
        q_pos = lo + jnp.arange(BLOCK)
        k_pos = jnp.arange(hi)
        causal = k_pos[None, :] <= q_pos[:, None]
        probs = jax.nn.softmax(jnp.where(causal, logits, -jnp.inf), axis=-1)
        outs.append(jnp.einsum("bhqk,bhkd->bhqd", probs, v[:, :, :hi]))
    return jnp.concatenate(outs, axis=2)


def _even_mixer(h, w_in, ret_norm, w_out):
    f32 = jnp.float32
    proj = jnp.einsum("bsm,mn->bsn", h, w_in)
    q_r, k_r, v_r, g_r, q_s, k_s, v_s = jnp.split(proj, _split_points(EVEN_IN_SIZES), axis=-1)
    q_r = _rotary(_split_heads(q_r, N_RET_HEADS).astype(f32))
    k_r = _rotary(_split_heads(k_r, N_RET_HEADS).astype(f32)) * (HEAD_DIM ** -0.5)
    y_r = _retention_chunkwise(q_r, k_r, _split_heads(v_r, N_RET_HEADS).astype(f32))
    mu = jnp.mean(y_r, axis=-1, keepdims=True)
    var = jnp.mean(jnp.square(y_r - mu), axis=-1, keepdims=True)
    y_r = _merge_heads((y_r - mu) * lax.rsqrt(var + GROUP_NORM_EPS))
    y_r = y_r * ret_norm.astype(f32) * jax.nn.silu(g_r.astype(f32))
    y_s = _merge_heads(_stick_breaking_attention(
        _split_heads(q_s, N_SB_HEADS).astype(f32),
        _split_heads(k_s, N_SB_HEADS).astype(f32),
        _split_heads(v_s, N_SB_HEADS).astype(f32)))
    y = jnp.concatenate([y_r, y_s], axis=-1).astype(h.dtype)
    return jnp.einsum("bsm,md->bsd", y, w_out)


def _odd_mixer(h, w_in, b_forget, w_out):
    f32 = jnp.float32
    proj = jnp.einsum("bsm,mn->bsn", h, w_in)
    q, k, v, f_logit = jnp.split(proj, _split_points(ODD_IN_SIZES), axis=-1)
    log_f = jax.nn.log_sigmoid(f_logit.astype(f32) + b_forget.astype(f32)).transpose(0, 2, 1)
    y = _forgetting_attention(_split_heads(q, N_FOX_HEADS).astype(f32),
                              _split_heads(k, N_FOX_HEADS).astype(f32),
                              _split_heads(v, N_FOX_HEADS).astype(f32), log_f)
    return jnp.einsum("bsm,md->bsd", _merge_heads(y).astype(h.dtype), w_out)


def _swiglu(h, w_gate, w_up, w_down):
    a = jax.nn.silu(jnp.einsum("bsd,df->bsf", h, w_gate)) * jnp.einsum("bsd,df->bsf", h, w_up)
    return jnp.einsum("bsf,fd->bsd", a, w_down)


def _moe_swiglu(h, w_router, w_gate, w_up, w_down):
    b, s, dm = h.shape
    t = h.reshape(b * s, dm)
    logits = jnp.einsum("td,de->te", t, w_router).astype(jnp.float32)
    top_vals, top_idx = lax.top_k(logits, TOP_K)
    gates = jax.nn.softmax(top_vals, axis=-1)
    combine = jnp.sum(jax.nn.one_hot(top_idx, N_EXPERTS, dtype=jnp.float32) * gates[..., None], axis=1)
    out = jnp.zeros((b * s, dm), jnp.float32)
    for e in range(N_EXPERTS):
        a = jax.nn.silu(t @ w_gate[e]) * (t @ w_up[e])
        out = out + combine[:, e:e + 1] * (a @ w_down[e]).astype(jnp.float32)
    return out.reshape(b, s, dm).astype(h.dtype)


def _normal(key, shape, scale):
    return jax.random.normal(key, shape, jnp.float32) * scale


def setup_inputs(seed: int = 0) -> dict:
    key = jax.random.key(seed)
    ks = jax.random.split(key, 20)
    d = D_MODEL
    return {
        "x": _normal(ks[0], (BATCH, SEQ, d), 1.0),
        "attn_norm_even": 1.0 + _normal(ks[1], (N_EVEN, d), 0.02),
        "w_in_even": _normal(ks[2], (N_EVEN, d, EVEN_IN_WIDTH), d ** -0.5),
        "ret_norm_even": 1.0 + _normal(ks[3], (N_EVEN, RET_WIDTH), 0.02),
        "w_out_even": _normal(ks[4], (N_EVEN, EVEN_MIX_WIDTH, d), EVEN_MIX_WIDTH ** -0.5),
        "ffn_norm_even": 1.0 + _normal(ks[5], (N_EVEN, d), 0.02),
        "w_gate_even": _normal(ks[6], (N_EVEN, d, D_FF), d ** -0.5),
        "w_up_even": _normal(ks[7], (N_EVEN, d, D_FF), d ** -0.5),
        "w_down_even": _normal(ks[8], (N_EVEN, D_FF, d), D_FF ** -0.5),
        "attn_norm_odd": 1.0 + _normal(ks[9], (N_ODD, d), 0.02),
        "w_in_odd": _normal(ks[10], (N_ODD, d, ODD_IN_WIDTH), d ** -0.5),
        "b_forget_odd": FORGET_BIAS_CENTER + _normal(ks[11], (N_ODD, N_FOX_HEADS), 0.1),
        "w_out_odd": _normal(ks[12], (N_ODD, FOX_WIDTH, d), FOX_WIDTH ** -0.5),
        "ffn_norm_odd": 1.0 + _normal(ks[13], (N_ODD, d), 0.02),
        "w_router_odd": _normal(ks[14], (N_ODD, d, N_EXPERTS), d ** -0.5),
        "w_gate_moe_odd": _normal(ks[15], (N_ODD, N_EXPERTS, d, D_FF), d ** -0.5),
        "w_up_moe_odd": _normal(ks[16], (N_ODD, N_EXPERTS, d, D_FF), d ** -0.5),
        "w_down_moe_odd": _normal(ks[17], (N_ODD, N_EXPERTS, D_FF, d), D_FF ** -0.5),
        "final_norm": 1.0 + _normal(ks[18], (d,), 0.02),
    }


def reference(x, attn_norm_even, w_in_even, ret_norm_even, w_out_even, ffn_norm_even,
              w_gate_even, w_up_even, w_down_even, attn_norm_odd, w_in_odd, b_forget_odd,
              w_out_odd, ffn_norm_odd, w_router_odd, w_gate_moe_odd, w_up_moe_odd,
              w_down_moe_odd, final_norm):
    h = x
    for layer in range(DEPTH):
        i = layer // 2
        if layer % 2 == 0:
            h = h + _even_mixer(_rms_norm(h, attn_norm_even[i]), w_in_even[i],
                                ret_norm_even[i], w_out_even[i]).astype(h.dtype)
            h = h + _swiglu(_rms_norm(h, ffn_norm_even[i]), w_gate_even[i],
                            w_up_even[i], w_down_even[i]).astype(h.dtype)
        else:
            h = h + _odd_mixer(_rms_norm(h, attn_norm_odd[i]), w_in_odd[i],
                               b_forget_odd[i], w_out_odd[i]).astype(h.dtype)
            h = h + _moe_swiglu(_rms_norm(h, ffn_norm_odd[i]), w_router_odd[i],
                                w_gate_moe_odd[i], w_up_moe_odd[i],
                                w_down_moe_odd[i]).astype(h.dtype)
    return _rms_norm(h, final_norm)
```

```python
import functools

import jax
import jax.numpy as jnp
import numpy as np
from jax import lax
from jax.experimental import pallas as pl
from jax.experimental.pallas import tpu as pltpu

F32 = jnp.float32
BF16 = jnp.bfloat16

LANES = 128
HEAD_DIM = 64
HEADS_PER_BLOCK = LANES // HEAD_DIM
N_EXPERTS = 8
TOP_K = 2
ROPE_BASE = 10000.0
NORM_EPS = 1e-6
GROUP_NORM_EPS = 1e-5
RET_CHUNK = 128
VMEM_LIMIT = 56 * 1024 * 1024

MM_TM = 1024
MM_TN = 512
ROW_TM = 512
ATT_T = 256
RET_TS = 512
FFN_TM = 512
FFN_NF = 2
GATHER_TM = 256


def _cparams(sem, vmem=VMEM_LIMIT):
    return pltpu.CompilerParams(dimension_semantics=sem, vmem_limit_bytes=vmem)


def _rms(xf, gain_row):
    ms = jnp.mean(xf * xf, axis=-1, keepdims=True)
    return xf * lax.rsqrt(ms + NORM_EPS) * gain_row


def _dot(a, b):
    return jnp.dot(a, b, preferred_element_type=F32)


def _dot_nt(a, b):
    return lax.dot_general(a, b, (((1,), (1,)), ((), ())), preferred_element_type=F32)


def _dot_tn(a, b):
    return lax.dot_general(a, b, (((0,), (0,)), ((), ())), preferred_element_type=F32)


def _norm_mm_kernel(x_ref, g_ref, w_ref, o_ref, xn_ref):
    @pl.when(pl.program_id(1) == 0)
    def _():
        xn_ref[...] = _rms(x_ref[...], g_ref[...]).astype(BF16)

    o_ref[...] = _dot(xn_ref[...], w_ref[...]).astype(o_ref.dtype)


def _norm_mm(x, gain, w, out_dtype):
    t, k = x.shape
    n = w.shape[1]
    tm, tn = min(MM_TM, t), min(MM_TN, n)
    return pl.pallas_call(
        _norm_mm_kernel,
        grid=(t // tm, n // tn),
        in_specs=[pl.BlockSpec((tm, k), lambda i, j: (i, 0)),
                  pl.BlockSpec((1, k), lambda i, j: (0, 0)),
                  pl.BlockSpec((k, tn), lambda i, j: (0, j))],
        out_specs=pl.BlockSpec((tm, tn), lambda i, j: (i, j)),
        out_shape=jax.ShapeDtypeStruct((t, n), out_dtype),
        scratch_shapes=[pltpu.VMEM((tm, k), BF16)],
        compiler_params=_cparams(("parallel", "arbitrary")),
        name="norm_mm",
    )(x, gain.reshape(1, k), w)


def _mm_kernel(x_ref, w_ref, o_ref):
    o_ref[...] = _dot(x_ref[...], w_ref[...]).astype(o_ref.dtype)


def _mm(x, w, out_dtype):
    t, k = x.shape
    n = w.shape[1]
    tm, tn = min(MM_TM, t), min(MM_TN, n)
    return pl.pallas_call(
        _mm_kernel,
        grid=(t // tm, n // tn),
        in_specs=[pl.BlockSpec((tm, k), lambda i, j: (i, 0)),
                  pl.BlockSpec((k, tn), lambda i, j: (0, j))],
        out_specs=pl.BlockSpec((tm, tn), lambda i, j: (i, j)),
        out_shape=jax.ShapeDtypeStruct((t, n), out_dtype),
        compiler_params=_cparams(("parallel", "parallel")),
        name="mm",
    )(x, w)


def _outproj_kernel(*refs, n_in, with_norm):
    ys = refs[:n_in]
    ws = refs[n_in:2 * n_in]
    h_ref = refs[2 * n_in]
    pos = 2 * n_in + 1
    acc = h_ref[...]
    for y_ref, w_ref in zip(ys, ws):
        acc = acc + _dot(y_ref[...], w_ref[...])
    if with_norm:
        g_ref, ho_ref, xn_ref = refs[pos], refs[pos + 1], refs[pos + 2]
        ho_ref[...] = acc
        xn_ref[...] = _rms(acc, g_ref[...]).astype(xn_ref.dtype)
    else:
        refs[pos][...] = acc


def _outproj(ys, ws, h, gain=None):
    t, d = h.shape
    tm = min(ROW_TM, t)
    n_in = len(ys)
    with_norm = gain is not None
    in_specs = [pl.BlockSpec((tm, y.shape[1]), lambda i: (i, 0)) for y in ys]
    in_specs += [pl.BlockSpec(w.shape, lambda i: (0, 0)) for w in ws]
    in_specs += [pl.BlockSpec((tm, d), lambda i: (i, 0))]
    args = list(ys) + list(ws) + [h]
    row_spec = pl.BlockSpec((tm, d), lambda i: (i, 0))
    if with_norm:
        in_specs += [pl.BlockSpec((1, d), lambda i: (0, 0))]
        args += [gain.reshape(1, d)]
        out_specs = [row_spec, row_spec]
        out_shape = [jax.ShapeDtypeStruct((t, d), F32), jax.ShapeDtypeStruct((t, d), BF16)]
    else:
        out_specs = row_spec
        out_shape = jax.ShapeDtypeStruct((t, d), F32)
    return pl.pallas_call(
        functools.partial(_outproj_kernel, n_in=n_in, with_norm=with_norm),
        grid=(t // tm,),
        in_specs=in_specs,
        out_specs=out_specs,
        out_shape=out_shape,
        compiler_params=_cparams(("parallel",)),
        name="outproj",
    )(*args)


def _retention_tables(seq, n_heads):
    half = HEAD_DIM // 2
    lane = np.arange(LANES)
    inv_freq = ROPE_BASE ** (-jnp.arange(half, dtype=F32) / half)
    ang = jnp.arange(seq, dtype=F32)[:, None] * inv_freq[None, :]
    cos, sin = jnp.cos(ang), jnp.sin(ang)
    cos_t = jnp.tile(cos, (1, LANES // half))
    sign = np.where((lane % HEAD_DIM) < half, -1.0, 1.0).astype(np.float32)
    sin_t = jnp.tile(sin, (1, LANES // half)) * sign[None, :]
    c = RET_CHUNK
    log_gamma = jnp.log(1.0 - 2.0 ** (-5.0 - jnp.arange(n_heads, dtype=F32)))
    pos = jnp.arange(c, dtype=F32)
    diff = pos[:, None] - pos[None, :]
    intra = jnp.where(diff >= 0.0,
                      jnp.exp(log_gamma[:, None, None] * jnp.maximum(diff, 0.0)), 0.0)
    intra = intra.reshape(n_heads // 2, 2, c, c)
    q_decay = jnp.exp(log_gamma[:, None] * (pos + 1.0))
    k_decay = jnp.exp(log_gamma[:, None] * (c - 1.0 - pos))
    chunk_decay = jnp.exp(log_gamma * c)

    def per_lane(tab):
        tab = tab.reshape(n_heads // 2, 2, c)
        return jnp.repeat(tab.transpose(0, 2, 1), HEAD_DIM, axis=2)

    head_of = lane // HEAD_DIM
    same = (head_of[:, None] == head_of[None, :]).astype(np.float32)
    cd = chunk_decay.reshape(n_heads // 2, 2)
    cd_rows = jnp.repeat(cd, HEAD_DIM, axis=1)
    state_decay = cd_rows[:, :, None] * same[None]
    return cos_t, sin_t, intra, per_lane(q_decay), per_lane(k_decay), state_decay, jnp.asarray(same)


def _retention_kernel(q_ref, k_ref, v_ref, g_ref, cos_ref, sin_ref, intra_ref, qd_ref, kd_ref,
                      sd_ref, same_ref, rn_ref, o_ref, state_ref, *, ts):
    @pl.when(pl.program_id(2) == 0)
    def _():
        state_ref[...] = jnp.zeros_like(state_ref)

    c = RET_CHUNK
    lane = lax.broadcasted_iota(jnp.int32, (1, LANES), 1)
    first_half = (lane % HEAD_DIM) < (HEAD_DIM // 2)
    head0 = lane < HEAD_DIM

    def rot(t, cos, sin):
        swapped = jnp.where(first_half, pltpu.roll(t, LANES - HEAD_DIM // 2, 1),
                            pltpu.roll(t, HEAD_DIM // 2, 1))
        return t * cos + swapped * sin

    for ci in range(ts // c):
        rows = slice(ci * c, (ci + 1) * c)
        cos, sin = cos_ref[rows, :], sin_ref[rows, :]
        q = rot(q_ref[rows, :].astype(F32), cos, sin)
        k = rot(k_ref[rows, :].astype(F32), cos, sin) * (HEAD_DIM ** -0.5)
        v = v_ref[rows, :]
        kb = k.astype(BF16)
        inner = []
        for hd in range(HEADS_PER_BLOCK):
            hmask = head0 if hd == 0 else jnp.logical_not(head0)
            qh = jnp.where(hmask, q, 0.0).astype(BF16)
            scores = _dot_nt(qh, kb) * intra_ref[hd]
            inner.append(_dot(scores.astype(BF16), v))
        state = state_ref[...]
        cross = _dot((q * qd_ref[...]).astype(BF16), state.astype(BF16))
        y = jnp.where(head0, inner[0], inner[1]) + cross
        kv = _dot_tn((k * kd_ref[...]).astype(BF16), v)
        state_ref[...] = state * sd_ref[...] + kv * same_ref[...]

        s0 = jnp.sum(jnp.where(head0, y, 0.0), axis=1, keepdims=True)
        s1 = jnp.sum(jnp.where(head0, 0.0, y), axis=1, keepdims=True)
        d = y - jnp.where(head0, s0, s1) * (1.0 / HEAD_DIM)
        dd = d * d
        v0 = jnp.sum(jnp.where(head0, dd, 0.0), axis=1, keepdims=True)
        v1 = jnp.sum(jnp.where(head0, 0.0, dd), axis=1, keepdims=True)
        var = jnp.where(head0, v0, v1) * (1.0 / HEAD_DIM)
        g = g_ref[rows, :].astype(F32)
        silu = g * (1.0 / (1.0 + jnp.exp(-g)))
        o_ref[rows, :] = (d * lax.rsqrt(var + GROUP_NORM_EPS) * rn_ref[...] * silu).astype(o_ref.dtype)


def _retention(proj, ret_norm, batch, seq, n_heads):
    t = proj.shape[0]
    width = n_heads * HEAD_DIM
    nb = width // LANES
    ts = min(RET_TS, seq)
    ns = seq // ts
    tabs = _retention_tables(seq, n_heads)
    cos_t, sin_t, intra, qd, kd, sd, same = tabs
    c = RET_CHUNK

    def col(group):
        return pl.BlockSpec((ts, LANES), lambda b, hp, si: (b * ns + si, group * nb + hp))

    in_specs = [col(0), col(1), col(2), col(3),
                pl.BlockSpec((ts, LANES), lambda b, hp, si: (si, 0)),
                pl.BlockSpec((ts, LANES), lambda b, hp, si: (si, 0)),
                pl.BlockSpec((None, 2, c, c), lambda b, hp, si: (hp, 0, 0, 0)),
                pl.BlockSpec((None, c, LANES), lambda b, hp, si: (hp, 0, 0)),
                pl.BlockSpec((None, c, LANES), lambda b, hp, si: (hp, 0, 0)),
                pl.BlockSpec((None, LANES, LANES), lambda b, hp, si: (hp, 0, 0)),
                pl.BlockSpec((LANES, LANES), lambda b, hp, si: (0, 0)),
                pl.BlockSpec((1, LANES), lambda b, hp, si: (0, hp))]
    return pl.pallas_call(
        functools.partial(_retention_kernel, ts=ts),
        grid=(batch, nb, ns),
        in_specs=in_specs,
        out_specs=pl.BlockSpec((ts, LANES), lambda b, hp, si: (b * ns + si, hp)),
        out_shape=jax.ShapeDtypeStruct((t, width), BF16),
        scratch_shapes=[pltpu.VMEM((LANES, LANES), F32)],
        compiler_params=_cparams(("parallel", "parallel", "arbitrary")),
        name="retention",
    )(proj, proj, proj, proj, cos_t, sin_t, intra, qd, kd, sd, same, ret_norm.reshape(1, width))


def _sb_kernel(q_ref, k_ref, v_ref, o_ref, *, tq):
    qi = pl.program_id(2)
    lane = lax.broadcasted_iota(jnp.int32, (1, LANES), 1)
    head0 = lane < HEAD_DIM
    row = lax.broadcasted_iota(jnp.int32, (tq, tq), 0)
    coli = lax.broadcasted_iota(jnp.int32, (tq, tq), 1)
    strict = coli < row
    suffix = jnp.where(row >= coli, 1.0, 0.0).astype(BF16)
    qf = q_ref[...].astype(F32) * (HEAD_DIM ** -0.5)

    outs = []
    for hd in range(HEADS_PER_BLOCK):
        hmask = head0 if hd == 0 else jnp.logical_not(head0)
        qh = jnp.where(hmask, qf, 0.0).astype(BF16)

        def block(kj, carry, diag):
            later, acc = carry
            ks = pl.multiple_of(kj * tq, tq)
            k = k_ref[pl.ds(ks, tq), :]
            v = v_ref[pl.ds(ks, tq), :]
            z = _dot_nt(qh, k)
            log_fail = -(jnp.maximum(z, 0.0) + jnp.log(1.0 + jnp.exp(-jnp.abs(z))))
            if diag:
                log_fail = jnp.where(strict, log_fail, 0.0)
            tail = _dot(log_fail.astype(BF16), suffix)
            w = jnp.exp(z + tail + later)
            if diag:
                w = jnp.where(strict, w, 0.0)
            acc = acc + _dot(w.astype(BF16), v)
            return later + tail[:, 0:1], acc

        carry = (jnp.zeros((tq, 1), F32), jnp.zeros((tq, LANES), F32))
        carry = block(qi, carry, True)
        carry = lax.fori_loop(0, qi, lambda j, cr: block(qi - 1 - j, cr, False), carry)
        outs.append(carry[1])
    o_ref[...] = jnp.where(head0, outs[0], outs[1]).astype(o_ref.dtype)


def _stick_breaking(proj, batch, seq, n_heads, first_group):
    t = proj.shape[0]
    width = n_heads * HEAD_DIM
    nb = width // LANES
    tq = min(ATT_T, seq)
    nq = seq // tq
    q_spec = pl.BlockSpec((tq, LANES), lambda b, hp, qi: (b * nq + qi, first_group * nb + hp))
    k_spec = pl.BlockSpec((seq, LANES), lambda b, hp, qi: (b, (first_group + 1) * nb + hp))
    v_spec = pl.BlockSpec((seq, LANES), lambda b, hp, qi: (b, (first_group + 2) * nb + hp))
    return pl.pallas_call(
        functools.partial(_sb_kernel, tq=tq),
        grid=(batch, nb, nq),
        in_specs=[q_spec, k_spec, v_spec],
        out_specs=pl.BlockSpec((tq, LANES), lambda b, hp, qi: (b * nq + qi, hp)),
        out_shape=jax.ShapeDtypeStruct((t, width), BF16),
        compiler_params=_cparams(("parallel", "parallel", "arbitrary")),
        name="stick_breaking",
    )(proj, proj, proj)


def _forget_cum_kernel(f_ref, b_ref, row_ref, col_ref, *, seq):
    x = f_ref[...] + b_ref[...]
    log_f = jnp.minimum(x, 0.0) - jnp.log(1.0 + jnp.exp(-jnp.abs(x)))
    xt = log_f.T
    pos = lax.broadcasted_iota(jnp.int32, xt.shape, 1)
    shift = 1
    while shift < seq:
        xt = xt + jnp.where(pos >= shift, pltpu.roll(xt, shift, 1), 0.0)
        shift *= 2
    row_ref[...] = xt
    col_ref[...] = xt.T


def _forget_cum(f_logit, b_forget, batch, seq):
    t = f_logit.shape[0]
    return pl.pallas_call(
        functools.partial(_forget_cum_kernel, seq=seq),
        grid=(batch,),
        in_specs=[pl.BlockSpec((seq, LANES), lambda b: (b, 0)),
                  pl.BlockSpec((1, LANES), lambda b: (0, 0))],
        out_specs=[pl.BlockSpec((None, LANES, seq), lambda b: (b, 0, 0)),
                   pl.BlockSpec((seq, LANES), lambda b: (b, 0))],
        out_shape=[jax.ShapeDtypeStruct((batch, LANES, seq), F32),
                   jax.ShapeDtypeStruct((t, LANES), F32)],
        compiler_params=_cparams(("parallel",)),
        name="forget_cum",
    )(f_logit, b_forget)


def _fox_kernel(q_ref, k_ref, v_ref, cq_ref, ck_ref, o_ref, *, tq):
    hp = pl.program_id(1)
    qi = pl.program_id(2)
    lane = lax.broadcasted_iota(jnp.int32, (1, LANES), 1)
    head0 = lane < HEAD_DIM
    row = lax.broadcasted_iota(jnp.int32, (tq, tq), 0)
    coli = lax.broadcasted_iota(jnp.int32, (tq, tq), 1)
    causal = coli <= row
    qf = q_ref[...].astype(F32) * (HEAD_DIM ** -0.5)
    cq = cq_ref[...]

    outs = []
    for hd in range(HEADS_PER_BLOCK):
        hmask = head0 if hd == 0 else jnp.logical_not(head0)
        qh = jnp.where(hmask, qf, 0.0).astype(BF16)
        bq = jnp.sum(jnp.where(lane == HEADS_PER_BLOCK * hp + hd, cq, 0.0), axis=1, keepdims=True)

        def block(kj, carry, diag):
            m, l, acc = carry
            ks = pl.multiple_of(kj * tq, tq)
            k = k_ref[pl.ds(ks, tq), :]
            v = v_ref[pl.ds(ks, tq), :]
            bk = ck_ref[hd, pl.ds(kj, 1), :]
            s = _dot_nt(qh, k) + (bq - bk)
            if diag:
                s = jnp.where(causal, s, -jnp.inf)
            m_new = jnp.maximum(m, jnp.max(s, axis=1, keepdims=True))
            p = jnp.exp(s - m_new)
            alpha = jnp.exp(m - m_new)
            l = alpha * l + jnp.sum(p, axis=1, keepdims=True)
            acc = alpha * acc + _dot(p.astype(BF16), v)
            return m_new, l, acc

        carry = (jnp.full((tq, 1), -jnp.inf, F32), jnp.zeros((tq, 1), F32),
                 jnp.zeros((tq, LANES), F32))
        carry = lax.fori_loop(0, qi, lambda j, cr: block(j, cr, False), carry)
        _, l, acc = block(qi, carry, True)
        outs.append(acc / l)
    o_ref[...] = jnp.where(head0, outs[0], outs[1]).astype(o_ref.dtype)


def _forgetting_attention(proj, cum_rows, cum_cols, batch, seq, n_heads):
    t = proj.shape[0]
    width = n_heads * HEAD_DIM
    nb = width // LANES
    tq = min(ATT_T, seq)
    nq = seq // tq
    q_spec = pl.BlockSpec((tq, LANES), lambda b, hp, qi: (b * nq + qi, hp))
    k_spec = pl.BlockSpec((seq, LANES), lambda b, hp, qi: (b, nb + hp))
    v_spec = pl.BlockSpec((seq, LANES), lambda b, hp, qi: (b, 2 * nb + hp))
    cq_spec = pl.BlockSpec((tq, LANES), lambda b, hp, qi: (b * nq + qi, 0))
    ck_spec = pl.BlockSpec((None, None, HEADS_PER_BLOCK, nq, tq), lambda b, hp, qi: (b, hp, 0, 0, 0))
    return pl.pallas_call(
        functools.partial(_fox_kernel, tq=tq),
        grid=(batch, nb, nq),
        in_specs=[q_spec, k_spec, v_spec, cq_spec, ck_spec],
        out_specs=pl.BlockSpec((tq, LANES), lambda b, hp, qi: (b * nq + qi, hp)),
        out_shape=jax.ShapeDtypeStruct((t, width), BF16),
        compiler_params=_cparams(("parallel", "parallel", "arbitrary")),
        name="forgetting_attention",
    )(proj, proj, proj, cum_cols, cum_rows)


def _swiglu_partial(x, wg_ref, wu_ref, wd_ref):
    g = _dot(x, wg_ref[...])
    u = _dot(x, wu_ref[...])
    a = g * (1.0 / (1.0 + jnp.exp(-g))) * u
    return _dot(a.astype(BF16), wd_ref[...])


def _ffn_dense_kernel(x_ref, wg_ref, wu_ref, wd_ref, h_ref, gain_ref, ho_ref, xn_ref, acc_ref):
    f = pl.program_id(1)

    @pl.when(f == 0)
    def _():
        acc_ref[...] = h_ref[...]

    acc_ref[...] += _swiglu_partial(x_ref[...], wg_ref, wu_ref, wd_ref)

    @pl.when(f == pl.num_programs(1) - 1)
    def _():
        h_new = acc_ref[...]
        ho_ref[...] = h_new
        xn_ref[...] = _rms(h_new, gain_ref[...]).astype(xn_ref.dtype)


def _ffn_dense(xn, wg, wu, wd, h, next_gain):
    t, d = h.shape
    dff = wg.shape[1]
    tm = min(FFN_TM, t)
    tf = dff // FFN_NF
    row = pl.BlockSpec((tm, d), lambda i, f: (i, 0))
    return pl.pallas_call(
        _ffn_dense_kernel,
        grid=(t // tm, FFN_NF),
        in_specs=[row,
                  pl.BlockSpec((d, tf), lambda i, f: (0, f)),
                  pl.BlockSpec((d, tf), lambda i, f: (0, f)),
                  pl.BlockSpec((tf, d), lambda i, f: (f, 0)),
                  row,
                  pl.BlockSpec((1, d), lambda i, f: (0, 0))],
        out_specs=[row, row],
        out_shape=[jax.ShapeDtypeStruct((t, d), F32), jax.ShapeDtypeStruct((t, d), BF16)],
        scratch_shapes=[pltpu.VMEM((tm, d), F32)],
        compiler_params=_cparams(("parallel", "arbitrary")),
        name="ffn_dense",
    )(xn, wg, wu, wd, h, next_gain.reshape(1, d))


def _ffn_grouped_kernel(te_ref, nu_ref, x_ref, wg_ref, wu_ref, wd_ref, o_ref, acc_ref):
    i = pl.program_id(0)
    f = pl.program_id(1)
    used = i < nu_ref[0]

    @pl.when(jnp.logical_and(used, f == 0))
    def _():
        acc_ref[...] = jnp.zeros_like(acc_ref)

    @pl.when(used)
    def _():
        acc_ref[...] += _swiglu_partial(x_ref[...].astype(BF16), wg_ref, wu_ref, wd_ref)

    @pl.when(f == pl.num_programs(1) - 1)
    def _():
        @pl.when(used)
        def _():
            o_ref[...] = acc_ref[...]

        @pl.when(jnp.logical_not(used))
        def _():
            o_ref[...] = jnp.zeros_like(o_ref)


def _ffn_grouped(x_sorted, wg, wu, wd, tile_expert, n_used):
    r, d = x_sorted.shape
    dff = wg.shape[2]
    tm = FFN_TM
    tf = dff // FFN_NF
    grid_spec = pltpu.PrefetchScalarGridSpec(
        num_scalar_prefetch=2,
        grid=(r // tm, FFN_NF),
        in_specs=[pl.BlockSpec((tm, d), lambda i, f, te, nu: (i, 0)),
                  pl.BlockSpec((None, d, tf), lambda i, f, te, nu: (te[i], 0, f)),
                  pl.BlockSpec((None, d, tf), lambda i, f, te, nu: (te[i], 0, f)),
                  pl.BlockSpec((None, tf, d), lambda i, f, te, nu: (te[i], f, 0))],
        out_specs=pl.BlockSpec((tm, d), lambda i, f, te, nu: (i, 0)),
        scratch_shapes=[pltpu.VMEM((tm, d), F32)],
    )
    return pl.pallas_call(
        _ffn_grouped_kernel,
        grid_spec=grid_spec,
        out_shape=jax.ShapeDtypeStruct((r, d), F32),
        compiler_params=_cparams(("arbitrary", "arbitrary")),
        name="ffn_grouped",
    )(tile_expert, n_used, x_sorted, wg, wu, wd)


def _router_kernel(h_ref, gain_ref, wr_ref, xn_ref, route_ref):
    xn = _rms(h_ref[...], gain_ref[...])
    xn_ref[...] = xn
    logits = jnp.dot(xn, wr_ref[...], precision=lax.Precision.HIGHEST, preferred_element_type=F32)
    lane = lax.broadcasted_iota(jnp.int32, logits.shape, 1)
    lane_f = lane.astype(F32)
    lg = jnp.where(lane < N_EXPERTS, logits, -jnp.inf)
    v1 = jnp.max(lg, axis=1, keepdims=True)
    i1 = jnp.min(jnp.where(lg == v1, lane_f, float(LANES)), axis=1, keepdims=True)
    lg2 = jnp.where(lane_f == i1, -jnp.inf, lg)
    v2 = jnp.max(lg2, axis=1, keepdims=True)
    i2 = jnp.min(jnp.where(lg2 == v2, lane_f, float(LANES)), axis=1, keepdims=True)
    e2 = jnp.exp(v2 - v1)
    g1 = 1.0 / (1.0 + e2)
    g2 = e2 / (1.0 + e2)
    route_ref[...] = jnp.where(lane == 0, i1, jnp.where(lane == 1, i2,
                               jnp.where(lane == 2, g1, jnp.where(lane == 3, g2, 0.0))))


def _router(h, gain, w_router):
    t, d = h.shape
    tm = min(ROW_TM, t)
    wr = jnp.zeros((d, LANES), F32).at[:, :N_EXPERTS].set(w_router)
    row = pl.BlockSpec((tm, d), lambda i: (i, 0))
    return pl.pallas_call(
        _router_kernel,
        grid=(t // tm,),
        in_specs=[row, pl.BlockSpec((1, d), lambda i: (0, 0)),
                  pl.BlockSpec((d, LANES), lambda i: (0, 0))],
        out_specs=[row, pl.BlockSpec((tm, LANES), lambda i: (i, 0))],
        out_shape=[jax.ShapeDtypeStruct((t, d), F32), jax.ShapeDtypeStruct((t, LANES), F32)],
        compiler_params=_cparams(("parallel",)),
        name="router",
    )(h, gain.reshape(1, d), wr)


def _row_copy(src_hbm, dst_hbm, sem, src_row, dst_row):
    return pltpu.make_async_copy(src_hbm.at[pl.ds(src_row, 1), :], dst_hbm.at[pl.ds(dst_row, 1), :], sem)


def _scatter_kernel(dest_ref, x_hbm, init_hbm, o_hbm, sems, *, tm):
    del init_hbm
    i = pl.program_id(0)
    n = pl.num_programs(0)
    slot = i % 2

    def issue(r, carry):
        for kk in range(TOP_K):
            _row_copy(x_hbm, o_hbm, sems.at[slot], i * tm + r, dest_ref[0, 0, TOP_K * r + kk]).start()
        return carry

    lax.fori_loop(0, tm, issue, 0)

    def drain(s):
        def wait(r, carry):
            _row_copy(x_hbm, o_hbm, sems.at[s], 0, 0).wait()
            return carry
        lax.fori_loop(0, TOP_K * tm, wait, 0)

    @pl.when(i > 0)
    def _():
        drain(1 - slot)

    @pl.when(i == n - 1)
    def _():
        drain(slot)


def _scatter_rows(xn, dest, n_rows):
    t, d = xn.shape
    tm = min(GATHER_TM, t)
    nt = t // tm
    dest3 = dest.reshape(nt, 1, TOP_K * tm)
    init = jnp.zeros((n_rows, d), F32)
    return pl.pallas_call(
        functools.partial(_scatter_kernel, tm=tm),
        grid=(nt,),
        in_specs=[pl.BlockSpec((1, 1, TOP_K * tm), lambda i: (i, 0, 0), memory_space=pltpu.SMEM),
                  pl.BlockSpec(memory_space=pl.ANY),
                  pl.BlockSpec(memory_space=pl.ANY)],
        out_specs=pl.BlockSpec(memory_space=pl.ANY),
        out_shape=jax.ShapeDtypeStruct((n_rows, d), F32),
        scratch_shapes=[pltpu.SemaphoreType.DMA((2,))],
        input_output_aliases={2: 0},
        compiler_params=_cparams(("arbitrary",)),
        name="moe_scatter",
    )(dest3, xn, init)


def _combine_kernel(dcur_ref, dnext_ref, y_hbm, h_ref, route_ref, gain_ref, o_ref, buf, sems, *, tm):
    i = pl.program_id(0)
    n = pl.num_programs(0)
    slot = i % 2

    def row_copy(d_ref, s, r, kk):
        return pltpu.make_async_copy(y_hbm.at[pl.ds(d_ref[0, 0, TOP_K * r + kk], 1), :],
                                     buf.at[s, kk, pl.ds(r, 1), :], sems.at[s])

    def issue(d_ref, s):
        def body(r, carry):
            for kk in range(TOP_K):
                row_copy(d_ref, s, r, kk).start()
            return carry
        lax.fori_loop(0, tm, body, 0)

    @pl.when(i == 0)
    def _():
        issue(dcur_ref, slot)

    @pl.when(i + 1 < n)
    def _():
        issue(dnext_ref, 1 - slot)

    def wait(r, carry):
        for kk in range(TOP_K):
            row_copy(dcur_ref, slot, r, kk).wait()
        return carry
    lax.fori_loop(0, tm, wait, 0)

    route = route_ref[...]
    lane = lax.broadcasted_iota(jnp.int32, route.shape, 1)
    g1 = jnp.sum(jnp.where(lane == 2, route, 0.0), axis=1, keepdims=True)
    g2 = jnp.sum(jnp.where(lane == 3, route, 0.0), axis=1, keepdims=True)
    out = h_ref[...] + (g1 * buf[slot, 0] + g2 * buf[slot, 1])
    o_ref[...] = _rms(out, gain_ref[...])


def _combine(y_sorted, dest, h, route, final_gain):
    t, d = h.shape
    tm = min(GATHER_TM, t)
    nt = t // tm
    dest3 = dest.reshape(nt, 1, TOP_K * tm)
    row = pl.BlockSpec((tm, d), lambda i: (i, 0))
    smem = functools.partial(pl.BlockSpec, (1, 1, TOP_K * tm), memory_space=pltpu.SMEM)
    return pl.pallas_call(
        functools.partial(_combine_kernel, tm=tm),
        grid=(nt,),
        in_specs=[smem(index_map=lambda i: (i, 0, 0)),
                  smem(index_map=lambda i: (jnp.minimum(i + 1, nt - 1), 0, 0)),
                  pl.BlockSpec(memory_space=pl.ANY),
                  row,
                  pl.BlockSpec((tm, LANES), lambda i: (i, 0)),
                  pl.BlockSpec((1, d), lambda i: (0, 0))],
        out_specs=row,
        out_shape=jax.ShapeDtypeStruct((t, d), F32),
        scratch_shapes=[pltpu.VMEM((2, TOP_K, tm, d), F32), pltpu.SemaphoreType.DMA((2,))],
        compiler_params=_cparams(("arbitrary",)),
        name="moe_combine",
    )(dest3, dest3, y_sorted, h, route, final_gain.reshape(1, d))


def _routing_plan(expert_idx, tm):
    t = expert_idx.shape[0]
    flat = expert_idx.reshape(-1)
    onehot = (flat[:, None] == jnp.arange(N_EXPERTS, dtype=jnp.int32)[None, :]).astype(jnp.int32)
    running = jnp.cumsum(onehot, axis=0)
    rank = jnp.sum((running - onehot) * onehot, axis=1)
    counts = running[-1]
    tiles = (counts + tm - 1) // tm
    tile_end = jnp.cumsum(tiles)
    group_start = (tile_end - tiles) * tm
    dest = jnp.sum(onehot * group_start[None, :], axis=1) + rank
    n_tiles = (t * TOP_K) // tm + N_EXPERTS
    n_used = tile_end[-1]
    tile_id = jnp.minimum(jnp.arange(n_tiles, dtype=jnp.int32), n_used - 1)
    tile_expert = jnp.sum((tile_id[:, None] >= tile_end[None, :]).astype(jnp.int32), axis=1)
    return dest.reshape(t, TOP_K).astype(jnp.int32), tile_expert.astype(jnp.int32), \
        n_used.reshape(1).astype(jnp.int32), n_tiles * tm


def kernel(x, attn_norm_even, w_in_even, ret_norm_even, w_out_even, ffn_norm_even, w_gate_even, w_up_even, w_down_even, attn_norm_odd, w_in_odd, b_forget_odd, w_out_odd, ffn_norm_odd, w_router_odd, w_gate_moe_odd, w_up_moe_odd, w_down_moe_odd, final_norm):
    batch, seq, d = x.shape
    t = batch * seq
    n_ret = d // (2 * HEAD_DIM)
    n_sb = d // (2 * HEAD_DIM)
    n_fox = d // HEAD_DIM
    ret_width = n_ret * HEAD_DIM
    fox_width = n_fox * HEAD_DIM
    h0 = x.reshape(t, d)

    proj = _norm_mm(h0, attn_norm_even[0], w_in_even[0].astype(BF16), BF16)
    y_ret = _retention(proj, ret_norm_even[0], batch, seq, n_ret)
    y_sb = _stick_breaking(proj, batch, seq, n_sb, first_group=4)
    w_out = w_out_even[0].astype(BF16)
    h1, xn1 = _outproj([y_ret, y_sb], [w_out[:ret_width], w_out[ret_width:]], h0, ffn_norm_even[0])
    h2, xn2 = _ffn_dense(xn1, w_gate_even[0].astype(BF16), w_up_even[0].astype(BF16),
                         w_down_even[0].astype(BF16), h1, attn_norm_odd[0])

    w_in = w_in_odd[0]
    proj = _mm(xn2, w_in[:, :3 * fox_width].astype(BF16), BF16)
    w_f = jnp.zeros((d, LANES), BF16).at[:, :n_fox].set(w_in[:, 3 * fox_width:].astype(BF16))
    f_logit = _mm(xn2, w_f, F32)
    b_f = jnp.zeros((1, LANES), F32).at[0, :n_fox].set(b_forget_odd[0])
    cum_rows, cum_cols = _forget_cum(f_logit, b_f, batch, seq)
    tq = min(ATT_T, seq)
    cum_rows = cum_rows[:, :n_fox, :].reshape(batch, n_fox // HEADS_PER_BLOCK, HEADS_PER_BLOCK, seq // tq, tq)
    y_fox = _forgetting_attention(proj, cum_rows, cum_cols, batch, seq, n_fox)
    h3 = _outproj([y_fox], [w_out_odd[0].astype(BF16)], h2)

    xn3, route = _router(h3, ffn_norm_odd[0], w_router_odd[0])
    expert_idx = route[:, :TOP_K].astype(jnp.int32)
    dest, tile_expert, n_used, n_rows = _routing_plan(expert_idx, FFN_TM)
    x_sorted = _scatter_rows(xn3, dest, n_rows)
    y_sorted = _ffn_grouped(x_sorted, w_gate_moe_odd[0].astype(BF16), w_up_moe_odd[0].astype(BF16),
                            w_down_moe_odd[0].astype(BF16), tile_expert, n_used)
    out = _combine(y_sorted, dest, h3, route, final_norm)
    return out.reshape(batch, seq, d)
```

```python
import functools

import jax
import jax.numpy as jnp
import numpy as np
from jax import lax
from jax.experimental import pallas as pl
from jax.experimental.pallas import tpu as pltpu

F32 = jnp.float32
BF16 = jnp.bfloat16

LANES = 128
HEAD_DIM = 64
HEADS_PER_BLOCK = LANES // HEAD_DIM
N_EXPERTS = 8
TOP_K = 2
ROPE_BASE = 10000.0
NORM_EPS = 1e-6
GROUP_NORM_EPS = 1e-5
LOG2E = 1.4426950408889634
RET_CHUNK = 128
VMEM_LIMIT = 56 * 1024 * 1024

MM_TM = 1024
MM_TN = 512
ROW_TM = 512
ATT_T = 512
ATT_SUB = 256
RET_TS = 512
FFN_TM = 512
FFN_NF = 2
GATHER_TM = 256
N_BIAS_PIECES = 3


def _cparams(sem, vmem=VMEM_LIMIT):
    return pltpu.CompilerParams(dimension_semantics=sem, vmem_limit_bytes=vmem)


def _rms(xf, gain_row):
    ms = jnp.mean(xf * xf, axis=-1, keepdims=True)
    return xf * lax.rsqrt(ms + NORM_EPS) * gain_row


def _dot(a, b):
    return jnp.dot(a, b, preferred_element_type=F32)


def _dot_nt(a, b):
    return lax.dot_general(a, b, (((1,), (1,)), ((), ())), preferred_element_type=F32)


def _dot_tn(a, b):
    return lax.dot_general(a, b, (((0,), (0,)), ((), ())), preferred_element_type=F32)


def _norm_mm_kernel(x_ref, g_ref, w_ref, o_ref, xn_ref):
    @pl.when(pl.program_id(1) == 0)
    def _():
        xn_ref[...] = _rms(x_ref[...], g_ref[...]).astype(BF16)

    o_ref[...] = _dot(xn_ref[...], w_ref[...]).astype(o_ref.dtype)


def _norm_mm(x, gain, w, out_dtype):
    t, k = x.shape
    n = w.shape[1]
    tm, tn = min(MM_TM, t), min(MM_TN, n)
    return pl.pallas_call(
        _norm_mm_kernel,
        grid=(t // tm, n // tn),
        in_specs=[pl.BlockSpec((tm, k), lambda i, j: (i, 0)),
                  pl.BlockSpec((1, k), lambda i, j: (0, 0)),
                  pl.BlockSpec((k, tn), lambda i, j: (0, j))],
        out_specs=[pl.BlockSpec((tm, tn), lambda i, j: (i, j)),
                   pl.BlockSpec((tm, k), lambda i, j: (i, 0))],
        out_shape=[jax.ShapeDtypeStruct((t, n), out_dtype), jax.ShapeDtypeStruct((t, k), BF16)],
        compiler_params=_cparams(("parallel", "arbitrary")),
        name="norm_mm",
    )(x, gain.reshape(1, k), w)


def _mm_kernel(x_ref, w_ref, o_ref):
    o_ref[...] = _dot(x_ref[...], w_ref[...]).astype(o_ref.dtype)


def _mm(x, w, out_dtype):
    t, k = x.shape
    n = w.shape[1]
    tm, tn = min(MM_TM, t), min(MM_TN, n)
    return pl.pallas_call(
        _mm_kernel,
        grid=(t // tm, n // tn),
        in_specs=[pl.BlockSpec((tm, k), lambda i, j: (i, 0)),
                  pl.BlockSpec((k, tn), lambda i, j: (0, j))],
        out_specs=pl.BlockSpec((tm, tn), lambda i, j: (i, j)),
        out_shape=jax.ShapeDtypeStruct((t, n), out_dtype),
        compiler_params=_cparams(("parallel", "parallel")),
        name="mm",
    )(x, w)


def _mm_nt_kernel(x_ref, wt_ref, o_ref, *, sub):
    res = _dot_nt(wt_ref[...], x_ref[...]).astype(o_ref.dtype)
    for s in range(o_ref.shape[0]):
        o_ref[s] = res[:, s * sub:(s + 1) * sub]


def _mm_nt(x, wt, sub):
    t, k = x.shape
    n = wt.shape[0]
    tm, tn = min(MM_TM, t), min(MM_TN, n)
    return pl.pallas_call(
        functools.partial(_mm_nt_kernel, sub=sub),
        grid=(t // tm, n // tn),
        in_specs=[pl.BlockSpec((tm, k), lambda i, j: (i, 0)),
                  pl.BlockSpec((tn, k), lambda i, j: (j, 0))],
        out_specs=pl.BlockSpec((tm // sub, tn, sub), lambda i, j: (i, j, 0)),
        out_shape=jax.ShapeDtypeStruct((t // sub, n, sub), BF16),
        compiler_params=_cparams(("parallel", "parallel")),
        name="mm_nt",
    )(x, wt)


def _outproj_kernel(*refs, n_in, with_norm):
    ys = refs[:n_in]
    ws = refs[n_in:2 * n_in]
    h_ref = refs[2 * n_in]
    pos = 2 * n_in + 1
    acc = h_ref[...]
    for y_ref, w_ref in zip(ys, ws):
        acc = acc + _dot(y_ref[...], w_ref[...])
    if with_norm:
        g_ref, ho_ref, xn_ref = refs[pos], refs[pos + 1], refs[pos + 2]
        ho_ref[...] = acc
        xn_ref[...] = _rms(acc, g_ref[...]).astype(xn_ref.dtype)
    else:
        refs[pos][...] = acc


def _outproj(ys, ws, h, gain=None):
    t, d = h.shape
    tm = min(ROW_TM, t)
    n_in = len(ys)
    with_norm = gain is not None
    in_specs = [pl.BlockSpec((tm, y.shape[1]), lambda i: (i, 0)) for y in ys]
    in_specs += [pl.BlockSpec(w.shape, lambda i: (0, 0)) for w in ws]
    in_specs += [pl.BlockSpec((tm, d), lambda i: (i, 0))]
    args = list(ys) + list(ws) + [h]
    row_spec = pl.BlockSpec((tm, d), lambda i: (i, 0))
    if with_norm:
        in_specs += [pl.BlockSpec((1, d), lambda i: (0, 0))]
        args += [gain.reshape(1, d)]
        out_specs = [row_spec, row_spec]
        out_shape = [jax.ShapeDtypeStruct((t, d), F32), jax.ShapeDtypeStruct((t, d), BF16)]
    else:
        out_specs = row_spec
        out_shape = jax.ShapeDtypeStruct((t, d), F32)
    return pl.pallas_call(
        functools.partial(_outproj_kernel, n_in=n_in, with_norm=with_norm),
        grid=(t // tm,),
        in_specs=in_specs,
        out_specs=out_specs,
        out_shape=out_shape,
        compiler_params=_cparams(("parallel",)),
        name="outproj",
    )(*args)


def _retention_tables(seq, n_heads):
    half = HEAD_DIM // 2
    lane = np.arange(LANES)
    inv_freq = ROPE_BASE ** (-jnp.arange(half, dtype=F32) / half)
    ang = jnp.arange(seq, dtype=F32)[:, None] * inv_freq[None, :]
    cos, sin = jnp.cos(ang), jnp.sin(ang)
    cos_t = jnp.tile(cos, (1, LANES // half))
    sign = np.where((lane % HEAD_DIM) < half, -1.0, 1.0).astype(np.float32)
    sin_t = jnp.tile(sin, (1, LANES // half)) * sign[None, :]
    c = RET_CHUNK
    log_gamma = jnp.log(1.0 - 2.0 ** (-5.0 - jnp.arange(n_heads, dtype=F32)))
    pos = jnp.arange(c, dtype=F32)
    diff = pos[:, None] - pos[None, :]
    intra = jnp.where(diff >= 0.0,
                      jnp.exp(log_gamma[:, None, None] * jnp.maximum(diff, 0.0)), 0.0)
    intra = intra.reshape(n_heads // 2, 2, c, c)
    q_decay = jnp.exp(log_gamma[:, None] * (pos + 1.0))
    k_decay = jnp.exp(log_gamma[:, None] * (c - 1.0 - pos))
    chunk_decay = jnp.exp(log_gamma * c)

    def per_lane(tab):
        tab = tab.reshape(n_heads // 2, 2, c)
        return jnp.repeat(tab.transpose(0, 2, 1), HEAD_DIM, axis=2)

    head_of = lane // HEAD_DIM
    same = (head_of[:, None] == head_of[None, :]).astype(np.float32)
    cd = chunk_decay.reshape(n_heads // 2, 2)
    cd_rows = jnp.repeat(cd, HEAD_DIM, axis=1)
    state_decay = cd_rows[:, :, None] * same[None]
    return cos_t, sin_t, intra, per_lane(q_decay), per_lane(k_decay), state_decay, jnp.asarray(same)


def _retention_kernel(q_ref, k_ref, v_ref, g_ref, cos_ref, sin_ref, intra_ref, qd_ref, kd_ref,
                      sd_ref, same_ref, rn_ref, o_ref, state_ref, *, ts):
    @pl.when(pl.program_id(2) == 0)
    def _():
        state_ref[...] = jnp.zeros_like(state_ref)

    c = RET_CHUNK
    lane = lax.broadcasted_iota(jnp.int32, (1, LANES), 1)
    first_half = (lane % HEAD_DIM) < (HEAD_DIM // 2)
    head0 = lane < HEAD_DIM

    def rot(t, cos, sin):
        swapped = jnp.where(first_half, pltpu.roll(t, LANES - HEAD_DIM // 2, 1),
                            pltpu.roll(t, HEAD_DIM // 2, 1))
        return t * cos + swapped * sin

    for ci in range(ts // c):
        rows = slice(ci * c, (ci + 1) * c)
        cos, sin = cos_ref[rows, :], sin_ref[rows, :]
        q = rot(q_ref[rows, :].astype(F32), cos, sin)
        k = rot(k_ref[rows, :].astype(F32), cos, sin) * (HEAD_DIM ** -0.5)
        v = v_ref[rows, :]
        kb = k.astype(BF16)
        inner = []
        for hd in range(HEADS_PER_BLOCK):
            hmask = head0 if hd == 0 else jnp.logical_not(head0)
            qh = jnp.where(hmask, q, 0.0).astype(BF16)
            scores = _dot_nt(qh, kb) * intra_ref[hd]
            inner.append(_dot(scores.astype(BF16), v))
        state = state_ref[...]
        cross = _dot((q * qd_ref[...]).astype(BF16), state.astype(BF16))
        y = jnp.where(head0, inner[0], inner[1]) + cross
        kv = _dot_tn((k * kd_ref[...]).astype(BF16), v)
        state_ref[...] = state * sd_ref[...] + kv * same_ref[...]

        s0 = jnp.sum(jnp.where(head0, y, 0.0), axis=1, keepdims=True)
        s1 = jnp.sum(jnp.where(head0, 0.0, y), axis=1, keepdims=True)
        d = y - jnp.where(head0, s0, s1) * (1.0 / HEAD_DIM)
        dd = d * d
        v0 = jnp.sum(jnp.where(head0, dd, 0.0), axis=1, keepdims=True)
        v1 = jnp.sum(jnp.where(head0, 0.0, dd), axis=1, keepdims=True)
        var = jnp.where(head0, v0, v1) * (1.0 / HEAD_DIM)
        g = g_ref[rows, :].astype(F32)
        silu = g * (1.0 / (1.0 + jnp.exp(-g)))
        o_ref[rows, :] = (d * lax.rsqrt(var + GROUP_NORM_EPS) * rn_ref[...] * silu).astype(o_ref.dtype)


def _retention(proj, ret_norm, batch, seq, n_heads):
    t = proj.shape[0]
    width = n_heads * HEAD_DIM
    nb = width // LANES
    ts = min(RET_TS, seq)
    ns = seq // ts
    tabs = _retention_tables(seq, n_heads)
    cos_t, sin_t, intra, qd, kd, sd, same = tabs
    c = RET_CHUNK

    def col(group):
        return pl.BlockSpec((ts, LANES), lambda b, hp, si: (b * ns + si, group * nb + hp))

    in_specs = [col(0), col(1), col(2), col(3),
                pl.BlockSpec((ts, LANES), lambda b, hp, si: (si, 0)),
                pl.BlockSpec((ts, LANES), lambda b, hp, si: (si, 0)),
                pl.BlockSpec((None, 2, c, c), lambda b, hp, si: (hp, 0, 0, 0)),
                pl.BlockSpec((None, c, LANES), lambda b, hp, si: (hp, 0, 0)),
                pl.BlockSpec((None, c, LANES), lambda b, hp, si: (hp, 0, 0)),
                pl.BlockSpec((None, LANES, LANES), lambda b, hp, si: (hp, 0, 0)),
                pl.BlockSpec((LANES, LANES), lambda b, hp, si: (0, 0)),
                pl.BlockSpec((1, LANES), lambda b, hp, si: (0, hp))]
    return pl.pallas_call(
        functools.partial(_retention_kernel, ts=ts),
        grid=(batch, nb, ns),
        in_specs=in_specs,
        out_specs=pl.BlockSpec((ts, LANES), lambda b, hp, si: (b * ns + si, hp)),
        out_shape=jax.ShapeDtypeStruct((t, width), BF16),
        scratch_shapes=[pltpu.VMEM((LANES, LANES), F32)],
        compiler_params=_cparams(("parallel", "parallel", "arbitrary")),
        name="retention",
    )(proj, proj, proj, proj, cos_t, sin_t, intra, qd, kd, sd, same, ret_norm.reshape(1, width))


def _head_masks():
    lane = lax.broadcasted_iota(jnp.int32, (1, LANES), 1)
    head0 = lane < HEAD_DIM
    return lane, head0, [head0, jnp.logical_not(head0)]


def _untranspose_heads(acc_t, head0):
    return jnp.where(head0, acc_t[0].T, acc_t[1].T)


def _sb_kernel(q_ref, k_ref, vt_ref, o_ref, *, tq, sub):
    qi = pl.program_id(2)
    n_sub = tq // sub
    _, head0, hmasks = _head_masks()
    key = lax.broadcasted_iota(jnp.int32, (sub, tq), 0)
    qry = lax.broadcasted_iota(jnp.int32, (sub, tq), 1)
    r = lax.broadcasted_iota(jnp.int32, (sub, sub), 0)
    c = lax.broadcasted_iota(jnp.int32, (sub, sub), 1)
    suffix = jnp.where(c >= r, 1.0, 0.0).astype(BF16)
    qf = q_ref[...].astype(F32) * (HEAD_DIM ** -0.5 * LOG2E)
    qh = [jnp.where(m, qf, 0.0).astype(BF16) for m in hmasks]

    def sub_block(jb, carry, diag_off):
        k = k_ref[jb]
        vt = vt_ref[jb]
        out = []
        for hd in range(HEADS_PER_BLOCK):
            later, acc = carry[hd]
            z = _dot_nt(k, qh[hd])
            fail = jnp.maximum(z, 0.0) + jnp.log2(1.0 + jnp.exp2(-jnp.abs(z)))
            if diag_off is not None:
                strict = (key + diag_off) < qry
                fail = jnp.where(strict, fail, 0.0)
            tail = _dot(suffix, fail.astype(BF16))
            w = jnp.exp2(z - tail - later)
            if diag_off is not None:
                w = jnp.where(strict, w, 0.0)
            acc = acc + _dot(vt, w.astype(BF16))
            out.append((later + tail[0:1, :], acc))
        return out

    carry = [(jnp.zeros((1, tq), F32), jnp.zeros((LANES, tq), F32)) for _ in range(HEADS_PER_BLOCK)]
    for s in reversed(range(n_sub)):
        carry = sub_block(qi * n_sub + s, carry, s * sub)

    def step(j, cr):
        base = (qi - 1 - j) * n_sub
        for s in reversed(range(n_sub)):
            cr = sub_block(base + s, cr, None)
        return cr

    carry = lax.fori_loop(0, qi, step, carry)
    o_ref[...] = _untranspose_heads([carry[0][1], carry[1][1]], head0).astype(o_ref.dtype)


def _stick_breaking(proj, v_t, batch, seq, n_heads, first_group):
    t = proj.shape[0]
    width = n_heads * HEAD_DIM
    nb = width // LANES
    tq = min(ATT_T, seq)
    sub = min(ATT_SUB, tq)
    nq = seq // tq
    nk = seq // sub
    proj_k = proj.reshape(t // sub, sub, proj.shape[1])
    q_spec = pl.BlockSpec((tq, LANES), lambda b, hp, qi: (b * nq + qi, first_group * nb + hp))
    k_spec = pl.BlockSpec((nk, sub, LANES), lambda b, hp, qi: (b, 0, (first_group + 1) * nb + hp))
    v_spec = pl.BlockSpec((nk, LANES, sub), lambda b, hp, qi: (b, hp, 0))
    return pl.pallas_call(
        functools.partial(_sb_kernel, tq=tq, sub=sub),
        grid=(batch, nb, nq),
        in_specs=[q_spec, k_spec, v_spec],
        out_specs=pl.BlockSpec((tq, LANES), lambda b, hp, qi: (b * nq + qi, hp)),
        out_shape=jax.ShapeDtypeStruct((t, width), BF16),
        compiler_params=_cparams(("parallel", "parallel", "arbitrary")),
        name="stick_breaking",
    )(proj, proj_k, v_t)


def _forget_cum_kernel(f_ref, b_ref, col_ref, *, seq):
    x = f_ref[...] + b_ref[...]
    log_f = jnp.minimum(x, 0.0) - jnp.log(1.0 + jnp.exp(-jnp.abs(x)))
    xt = log_f.T
    pos = lax.broadcasted_iota(jnp.int32, xt.shape, 1)
    shift = 1
    while shift < seq:
        xt = xt + jnp.where(pos >= shift, pltpu.roll(xt, shift, 1), 0.0)
        shift *= 2
    col_ref[...] = xt.T


def _forget_cum(f_logit, b_forget, batch, seq):
    t = f_logit.shape[0]
    return pl.pallas_call(
        functools.partial(_forget_cum_kernel, seq=seq),
        grid=(batch,),
        in_specs=[pl.BlockSpec((seq, LANES), lambda b: (b, 0)),
                  pl.BlockSpec((1, LANES), lambda b: (0, 0))],
        out_specs=pl.BlockSpec((seq, LANES), lambda b: (b, 0)),
        out_shape=jax.ShapeDtypeStruct((t, LANES), F32),
        compiler_params=_cparams(("parallel",)),
        name="forget_cum",
    )(f_logit, b_forget)


def _bias_lanes(hd):
    return HEAD_DIM * (1 - hd)


def _with_bias_lanes(x, bias, lane, hmask, hd, is_query):
    pieces = []
    rest = bias
    for _ in range(N_BIAS_PIECES):
        p = rest.astype(BF16).astype(F32)
        pieces.append(p)
        rest = rest - p
    base = _bias_lanes(hd)
    out = jnp.where(hmask, x, 0.0)
    for i, p in enumerate(pieces):
        piece_lane = base + i if is_query else base + N_BIAS_PIECES + i
        one_lane = base + N_BIAS_PIECES + i if is_query else base + i
        out = jnp.where(lane == piece_lane, p if is_query else -p, out)
        out = jnp.where(lane == one_lane, 1.0, out)
    return out.astype(BF16)


def _fox_kernel(q_ref, k_ref, vt_ref, cq_ref, ck_ref, o_ref, kp_ref, vp_ref, *, tq, sub, seq):
    hp = pl.program_id(1)
    qi = pl.program_id(2)
    n_sub = tq // sub
    nk = seq // sub
    lane, head0, hmasks = _head_masks()
    ones_row = [HEAD_DIM, 0]

    def head_bias(cum, hd):
        sel = lane == HEADS_PER_BLOCK * hp + hd
        return jnp.sum(jnp.where(sel, cum, 0.0), axis=1, keepdims=True) * LOG2E

    @pl.when(qi == 0)
    def _():
        kf = k_ref[...].astype(F32)
        ck = ck_ref[...]
        row = lax.broadcasted_iota(jnp.int32, (LANES, sub), 0)
        for hd in range(HEADS_PER_BLOCK):
            kp_ref[hd] = _with_bias_lanes(kf, head_bias(ck, hd), lane, hmasks[hd], hd, False)
            in_head = (row < HEAD_DIM) if hd == 0 else (row >= HEAD_DIM)
            for jb in range(nk):
                vt = vt_ref[jb]
                vp_ref[hd, jb] = jnp.where(in_head, vt, jnp.where(row == ones_row[hd], 1.0, 0.0).astype(BF16))

    qf = q_ref[...].astype(F32) * (HEAD_DIM ** -0.5 * LOG2E)
    cq = cq_ref[...]
    qh = [_with_bias_lanes(qf, head_bias(cq, hd), lane, hmasks[hd], hd, True)
          for hd in range(HEADS_PER_BLOCK)]
    key = lax.broadcasted_iota(jnp.int32, (tq, tq), 0)
    qry = lax.broadcasted_iota(jnp.int32, (tq, tq), 1)
    causal = key <= qry

    def block(kj, carry, diag):
        ks = pl.multiple_of(kj * tq, tq)
        out = []
        for hd in range(HEADS_PER_BLOCK):
            m, acc = carry[hd]
            s = _dot_nt(kp_ref[hd, pl.ds(ks, tq), :], qh[hd])
            if diag:
                s = jnp.where(causal, s, -jnp.inf)
            m_new = jnp.maximum(m, jnp.max(s, axis=0, keepdims=True))
            p = jnp.exp2(s - m_new).astype(BF16)
            pv = _dot(vp_ref[hd, kj * n_sub], p[0:sub])
            for sb in range(1, n_sub):
                pv = pv + _dot(vp_ref[hd, kj * n_sub + sb], p[sb * sub:(sb + 1) * sub])
            out.append((m_new, jnp.exp2(m - m_new) * acc + pv))
        return out

    carry = [(jnp.full((1, tq), -jnp.inf, F32), jnp.zeros((LANES, tq), F32))
             for _ in range(HEADS_PER_BLOCK)]
    carry = lax.fori_loop(0, qi, lambda j, cr: block(j, cr, False), carry)
    carry = block(qi, carry, True)
    normed = []
    for hd in range(HEADS_PER_BLOCK):
        acc = carry[hd][1]
        normed.append(acc * (1.0 / acc[ones_row[hd]:ones_row[hd] + 1, :]))
    o_ref[...] = _untranspose_heads(normed, head0).astype(o_ref.dtype)


def _forgetting_attention(proj, v_t, cum_cols, batch, seq, n_heads):
    t = proj.shape[0]
    width = n_heads * HEAD_DIM
    nb = width // LANES
    tq = min(ATT_T, seq)
    sub = min(ATT_SUB, tq)
    nq = seq // tq
    nk = seq // sub
    q_spec = pl.BlockSpec((tq, LANES), lambda b, hp, qi: (b * nq + qi, hp))
    k_spec = pl.BlockSpec((seq, LANES), lambda b, hp, qi: (b, nb + hp))
    v_spec = pl.BlockSpec((nk, LANES, sub), lambda b, hp, qi: (b, hp, 0))
    cq_spec = pl.BlockSpec((tq, LANES), lambda b, hp, qi: (b * nq + qi, 0))
    ck_spec = pl.BlockSpec((seq, LANES), lambda b, hp, qi: (b, 0))
    return pl.pallas_call(
        functools.partial(_fox_kernel, tq=tq, sub=sub, seq=seq),
        grid=(batch, nb, nq),
        in_specs=[q_spec, k_spec, v_spec, cq_spec, ck_spec],
        out_specs=pl.BlockSpec((tq, LANES), lambda b, hp, qi: (b * nq + qi, hp)),
        out_shape=jax.ShapeDtypeStruct((t, width), BF16),
        scratch_shapes=[pltpu.VMEM((HEADS_PER_BLOCK, seq, LANES), BF16),
                        pltpu.VMEM((HEADS_PER_BLOCK, nk, LANES, sub), BF16)],
        compiler_params=_cparams(("parallel", "parallel", "arbitrary")),
        name="forgetting_attention",
    )(proj, proj, v_t, cum_cols, cum_cols)


def _swiglu_partial(x, wg_ref, wu_ref, wd_ref):
    g = _dot(x, wg_ref[...])
    u = _dot(x, wu_ref[...])
    a = g * (1.0 / (1.0 + jnp.exp(-g))) * u
    return _dot(a.astype(BF16), wd_ref[...])


def _ffn_dense_kernel(x_ref, wg_ref, wu_ref, wd_ref, h_ref, gain_ref, ho_ref, xn_ref, acc_ref):
    f = pl.program_id(1)

    @pl.when(f == 0)
    def _():
        acc_ref[...] = h_ref[...]

    acc_ref[...] += _swiglu_partial(x_ref[...], wg_ref, wu_ref, wd_ref)

    @pl.when(f == pl.num_programs(1) - 1)
    def _():
        h_new = acc_ref[...]
        ho_ref[...] = h_new
        xn_ref[...] = _rms(h_new, gain_ref[...]).astype(xn_ref.dtype)


def _ffn_dense(xn, wg, wu, wd, h, next_gain):
    t, d = h.shape
    dff = wg.shape[1]
    tm = min(FFN_TM, t)
    tf = dff // FFN_NF
    row = pl.BlockSpec((tm, d), lambda i, f: (i, 0))
    return pl.pallas_call(
        _ffn_dense_kernel,
        grid=(t // tm, FFN_NF),
        in_specs=[row,
                  pl.BlockSpec((d, tf), lambda i, f: (0, f)),
                  pl.BlockSpec((d, tf), lambda i, f: (0, f)),
                  pl.BlockSpec((tf, d), lambda i, f: (f, 0)),
                  row,
                  pl.BlockSpec((1, d), lambda i, f: (0, 0))],
        out_specs=[row, row],
        out_shape=[jax.ShapeDtypeStruct((t, d), F32), jax.ShapeDtypeStruct((t, d), BF16)],
        scratch_shapes=[pltpu.VMEM((tm, d), F32)],
        compiler_params=_cparams(("parallel", "arbitrary")),
        name="ffn_dense",
    )(xn, wg, wu, wd, h, next_gain.reshape(1, d))


def _ffn_grouped_kernel(te_ref, nu_ref, x_ref, wg_ref, wu_ref, wd_ref, o_ref, acc_ref):
    i = pl.program_id(0)
    f = pl.program_id(1)
    used = i < nu_ref[0]

    @pl.when(jnp.logical_and(used, f == 0))
    def _():
        acc_ref[...] = jnp.zeros_like(acc_ref)

    @pl.when(used)
    def _():
        acc_ref[...] += _swiglu_partial(x_ref[...].astype(BF16), wg_ref, wu_ref, wd_ref)

    @pl.when(f == pl.num_programs(1) - 1)
    def _():
        @pl.when(used)
        def _():
            o_ref[...] = acc_ref[...]

        @pl.when(jnp.logical_not(used))
        def _():
            o_ref[...] = jnp.zeros_like(o_ref)


def _ffn_grouped(x_sorted, wg, wu, wd, tile_expert, n_used):
    r, d = x_sorted.shape
    dff = wg.shape[2]
    tm = FFN_TM
    tf = dff // FFN_NF
    grid_spec = pltpu.PrefetchScalarGridSpec(
        num_scalar_prefetch=2,
        grid=(r // tm, FFN_NF),
        in_specs=[pl.BlockSpec((tm, d), lambda i, f, te, nu: (i, 0)),
                  pl.BlockSpec((None, d, tf), lambda i, f, te, nu: (te[i], 0, f)),
                  pl.BlockSpec((None, d, tf), lambda i, f, te, nu: (te[i], 0, f)),
                  pl.BlockSpec((None, tf, d), lambda i, f, te, nu: (te[i], f, 0))],
        out_specs=pl.BlockSpec((tm, d), lambda i, f, te, nu: (i, 0)),
        scratch_shapes=[pltpu.VMEM((tm, d), F32)],
    )
    return pl.pallas_call(
        _ffn_grouped_kernel,
        grid_spec=grid_spec,
        out_shape=jax.ShapeDtypeStruct((r, d), F32),
        compiler_params=_cparams(("arbitrary", "arbitrary")),
        name="ffn_grouped",
    )(tile_expert, n_used, x_sorted, wg, wu, wd)


def _router_kernel(h_ref, gain_ref, wr_ref, xn_ref, route_ref):
    xn = _rms(h_ref[...], gain_ref[...])
    xn_ref[...] = xn
    logits = jnp.dot(xn, wr_ref[...], precision=lax.Precision.HIGHEST, preferred_element_type=F32)
    lane = lax.broadcasted_iota(jnp.int32, logits.shape, 1)
    lane_f = lane.astype(F32)
    lg = jnp.where(lane < N_EXPERTS, logits, -jnp.inf)
    v1 = jnp.max(lg, axis=1, keepdims=True)
    i1 = jnp.min(jnp.where(lg == v1, lane_f, float(LANES)), axis=1, keepdims=True)
    lg2 = jnp.where(lane_f == i1, -jnp.inf, lg)
    v2 = jnp.max(lg2, axis=1, keepdims=True)
    i2 = jnp.min(jnp.where(lg2 == v2, lane_f, float(LANES)), axis=1, keepdims=True)
    e2 = jnp.exp(v2 - v1)
    g1 = 1.0 / (1.0 + e2)
    g2 = e2 / (1.0 + e2)
    route_ref[...] = jnp.where(lane == 0, i1, jnp.where(lane == 1, i2,
                               jnp.where(lane == 2, g1, jnp.where(lane == 3, g2, 0.0))))


def _router(h, gain, w_router):
    t, d = h.shape
    tm = min(ROW_TM, t)
    wr = jnp.zeros((d, LANES), F32).at[:, :N_EXPERTS].set(w_router)
    row = pl.BlockSpec((tm, d), lambda i: (i, 0))
    return pl.pallas_call(
        _router_kernel,
        grid=(t // tm,),
        in_specs=[row, pl.BlockSpec((1, d), lambda i: (0, 0)),
                  pl.BlockSpec((d, LANES), lambda i: (0, 0))],
        out_specs=[row, pl.BlockSpec((tm, LANES), lambda i: (i, 0))],
        out_shape=[jax.ShapeDtypeStruct((t, d), F32), jax.ShapeDtypeStruct((t, LANES), F32)],
        compiler_params=_cparams(("parallel",)),
        name="router",
    )(h, gain.reshape(1, d), wr)


def _scatter_kernel(dest_ref, x_ref, init_hbm, o_hbm, sem, *, tm):
    del init_hbm

    def row_copy(r, kk):
        return pltpu.make_async_copy(x_ref.at[pl.ds(r, 1), :],
                                     o_hbm.at[pl.ds(dest_ref[0, 0, TOP_K * r + kk], 1), :], sem)

    def issue(r, carry):
        for kk in range(TOP_K):
            row_copy(r, kk).start()
        return carry

    lax.fori_loop(0, tm, issue, 0)

    def wait(r, carry):
        for kk in range(TOP_K):
            row_copy(r, kk).wait()
        return carry

    lax.fori_loop(0, tm, wait, 0)


def _scatter_rows(xn, dest, n_rows):
    t, d = xn.shape
    tm = min(GATHER_TM, t)
    nt = t // tm
    dest3 = dest.reshape(nt, 1, TOP_K * tm)
    init = jnp.zeros((n_rows, d), F32)
    return pl.pallas_call(
        functools.partial(_scatter_kernel, tm=tm),
        grid=(nt,),
        in_specs=[pl.BlockSpec((1, 1, TOP_K * tm), lambda i: (i, 0, 0), memory_space=pltpu.SMEM),
                  pl.BlockSpec((tm, d), lambda i: (i, 0)),
                  pl.BlockSpec(memory_space=pl.ANY)],
        out_specs=pl.BlockSpec(memory_space=pl.ANY),
        out_shape=jax.ShapeDtypeStruct((n_rows, d), F32),
        scratch_shapes=[pltpu.SemaphoreType.DMA(())],
        input_output_aliases={2: 0},
        compiler_params=_cparams(("arbitrary",)),
        name="moe_scatter",
    )(dest3, xn, init)


def _combine_kernel(dcur_ref, dnext_ref, y_hbm, h_ref, route_ref, gain_ref, o_ref, buf, sems, *, tm):
    i = pl.program_id(0)
    n = pl.num_programs(0)
    slot = i % 2

    def row_copy(d_ref, s, r, kk):
        return pltpu.make_async_copy(y_hbm.at[pl.ds(d_ref[0, 0, TOP_K * r + kk], 1), :],
                                     buf.at[s, kk, pl.ds(r, 1), :], sems.at[s])

    def issue(d_ref, s):
        def body(r, carry):
            for kk in range(TOP_K):
                row_copy(d_ref, s, r, kk).start()
            return carry
        lax.fori_loop(0, tm, body, 0)

    @pl.when(i == 0)
    def _():
        issue(dcur_ref, slot)

    @pl.when(i + 1 < n)
    def _():
        issue(dnext_ref, 1 - slot)

    def wait(r, carry):
        for kk in range(TOP_K):
            row_copy(dcur_ref, slot, r, kk).wait()
        return carry
    lax.fori_loop(0, tm, wait, 0)

    route = route_ref[...]
    lane = lax.broadcasted_iota(jnp.int32, route.shape, 1)
    g1 = jnp.sum(jnp.where(lane == 2, route, 0.0), axis=1, keepdims=True)
    g2 = jnp.sum(jnp.where(lane == 3, route, 0.0), axis=1, keepdims=True)
    out = h_ref[...] + (g1 * buf[slot, 0] + g2 * buf[slot, 1])
    o_ref[...] = _rms(out, gain_ref[...])


def _combine(y_sorted, dest, h, route, final_gain):
    t, d = h.shape
    tm = min(GATHER_TM, t)
    nt = t // tm
    dest3 = dest.reshape(nt, 1, TOP_K * tm)
    row = pl.BlockSpec((tm, d), lambda i: (i, 0))
    smem = functools.partial(pl.BlockSpec, (1, 1, TOP_K * tm), memory_space=pltpu.SMEM)
    return pl.pallas_call(
        functools.partial(_combine_kernel, tm=tm),
        grid=(nt,),
        in_specs=[smem(index_map=lambda i: (i, 0, 0)),
                  smem(index_map=lambda i: (jnp.minimum(i + 1, nt - 1), 0, 0)),
                  pl.BlockSpec(memory_space=pl.ANY),
                  row,
                  pl.BlockSpec((tm, LANES), lambda i: (i, 0)),
                  pl.BlockSpec((1, d), lambda i: (0, 0))],
        out_specs=row,
        out_shape=jax.ShapeDtypeStruct((t, d), F32),
        scratch_shapes=[pltpu.VMEM((2, TOP_K, tm, d), F32), pltpu.SemaphoreType.DMA((2,))],
        compiler_params=_cparams(("arbitrary",)),
        name="moe_combine",
    )(dest3, dest3, y_sorted, h, route, final_gain.reshape(1, d))


def _routing_plan(expert_idx, tm):
    t = expert_idx.shape[0]
    flat = expert_idx.reshape(-1)
    onehot = (flat[:, None] == jnp.arange(N_EXPERTS, dtype=jnp.int32)[None, :]).astype(jnp.int32)
    running = jnp.cumsum(onehot, axis=0)
    rank = jnp.sum((running - onehot) * onehot, axis=1)
    counts = running[-1]
    tiles = (counts + tm - 1) // tm
    tile_end = jnp.cumsum(tiles)
    group_start = (tile_end - tiles) * tm
    dest = jnp.sum(onehot * group_start[None, :], axis=1) + rank
    n_tiles = (t * TOP_K) // tm + N_EXPERTS
    n_used = tile_end[-1]
    tile_id = jnp.minimum(jnp.arange(n_tiles, dtype=jnp.int32), n_used - 1)
    tile_expert = jnp.sum((tile_id[:, None] >= tile_end[None, :]).astype(jnp.int32), axis=1)
    return dest.reshape(t, TOP_K).astype(jnp.int32), tile_expert.astype(jnp.int32), \
        n_used.reshape(1).astype(jnp.int32), n_tiles * tm


def kernel(x, attn_norm_even, w_in_even, ret_norm_even, w_out_even, ffn_norm_even, w_gate_even, w_up_even, w_down_even, attn_norm_odd, w_in_odd, b_forget_odd, w_out_odd, ffn_norm_odd, w_router_odd, w_gate_moe_odd, w_up_moe_odd, w_down_moe_odd, final_norm):
    batch, seq, d = x.shape
    t = batch * seq
    n_ret = d // (2 * HEAD_DIM)
    n_sb = d // (2 * HEAD_DIM)
    n_fox = d // HEAD_DIM
    ret_width = n_ret * HEAD_DIM
    sb_width = n_sb * HEAD_DIM
    fox_width = n_fox * HEAD_DIM
    sub = min(ATT_SUB, seq)
    h0 = x.reshape(t, d)

    w_in = w_in_even[0]
    n_direct = 4 * ret_width + 2 * sb_width
    proj, xn0 = _norm_mm(h0, attn_norm_even[0], w_in[:, :n_direct].astype(BF16), BF16)
    v_sb_t = _mm_nt(xn0, w_in[:, n_direct:].T.astype(BF16), sub)
    y_ret = _retention(proj, ret_norm_even[0], batch, seq, n_ret)
    y_sb = _stick_breaking(proj, v_sb_t, batch, seq, n_sb, first_group=4)
    w_out = w_out_even[0].astype(BF16)
    h1, xn1 = _outproj([y_ret, y_sb], [w_out[:ret_width], w_out[ret_width:]], h0, ffn_norm_even[0])
    h2, xn2 = _ffn_dense(xn1, w_gate_even[0].astype(BF16), w_up_even[0].astype(BF16),
                         w_down_even[0].astype(BF16), h1, attn_norm_odd[0])

    w_in = w_in_odd[0]
    proj = _mm(xn2, w_in[:, :2 * fox_width].astype(BF16), BF16)
    v_fox_t = _mm_nt(xn2, w_in[:, 2 * fox_width:3 * fox_width].T.astype(BF16), sub)
    w_f = jnp.zeros((d, LANES), BF16).at[:, :n_fox].set(w_in[:, 3 * fox_width:].astype(BF16))
    f_logit = _mm(xn2, w_f, F32)
    b_f = jnp.zeros((1, LANES), F32).at[0, :n_fox].set(b_forget_odd[0])
    cum_cols = _forget_cum(f_logit, b_f, batch, seq)
    y_fox = _forgetting_attention(proj, v_fox_t, cum_cols, batch, seq, n_fox)
    h3 = _outproj([y_fox], [w_out_odd[0].astype(BF16)], h2)

    xn3, route = _router(h3, ffn_norm_odd[0], w_router_odd[0])
    expert_idx = route[:, :TOP_K].astype(jnp.int32)
    dest, tile_expert, n_used, n_rows = _routing_plan(expert_idx, FFN_TM)
    x_sorted = _scatter_rows(xn3, dest, n_rows)
    y_sorted = _ffn_grouped(x_sorted, w_gate_moe_odd[0].astype(BF16), w_up_moe_odd[0].astype(BF16),
                            w_down_moe_odd[0].astype(BF16), tile_expert, n_used)
    out = _combine(y_sorted, dest, h3, route, final_norm)
    return out.reshape(batch, seq, d)
```

```python
import functools

import jax
import jax.numpy as jnp
import numpy as np
from jax import lax
from jax.experimental import pallas as pl
from jax.experimental.pallas import tpu as pltpu

F32 = jnp.float32
BF16 = jnp.bfloat16

LANES = 128
HEAD_DIM = 64
HEADS_PER_BLOCK = LANES // HEAD_DIM
N_EXPERTS = 8
TOP_K = 2
ROPE_BASE = 10000.0
NORM_EPS = 1e-6
GROUP_NORM_EPS = 1e-5
LOG2E = 1.4426950408889634
RET_CHUNK = 128
VMEM_LIMIT = 56 * 1024 * 1024

MM_TM = 1024
MM_TN = 512
ROW_TM = 512
ATT_T = 512
RET_TS = 512
FFN_TM = 256
FFN_NF = 1
GATHER_TM = 256
N_BIAS_PIECES = 3


def _cparams(sem, vmem=VMEM_LIMIT):
    return pltpu.CompilerParams(dimension_semantics=sem, vmem_limit_bytes=vmem)


def _rms(xf, gain_row):
    ms = jnp.mean(xf * xf, axis=-1, keepdims=True)
    return xf * lax.rsqrt(ms + NORM_EPS) * gain_row


def _dot(a, b):
    return jnp.dot(a, b, preferred_element_type=F32)


def _dot_nt(a, b):
    return lax.dot_general(a, b, (((1,), (1,)), ((), ())), preferred_element_type=F32)


def _dot_tn(a, b):
    return lax.dot_general(a, b, (((0,), (0,)), ((), ())), preferred_element_type=F32)


def _norm_mm_kernel(x_ref, g_ref, w_ref, o_ref, xn_ref):
    @pl.when(pl.program_id(1) == 0)
    def _():
        xn_ref[...] = _rms(x_ref[...], g_ref[...]).astype(BF16)

    o_ref[...] = _dot(xn_ref[...], w_ref[...]).astype(o_ref.dtype)


def _norm_mm(x, gain, w, out_dtype):
    t, k = x.shape
    n = w.shape[1]
    tm, tn = min(MM_TM, t), min(MM_TN, n)
    return pl.pallas_call(
        _norm_mm_kernel,
        grid=(t // tm, n // tn),
        in_specs=[pl.BlockSpec((tm, k), lambda i, j: (i, 0)),
                  pl.BlockSpec((1, k), lambda i, j: (0, 0)),
                  pl.BlockSpec((k, tn), lambda i, j: (0, j))],
        out_specs=[pl.BlockSpec((tm, tn), lambda i, j: (i, j)),
                   pl.BlockSpec((tm, k), lambda i, j: (i, 0))],
        out_shape=[jax.ShapeDtypeStruct((t, n), out_dtype), jax.ShapeDtypeStruct((t, k), BF16)],
        compiler_params=_cparams(("parallel", "arbitrary")),
        name="norm_mm",
    )(x, gain.reshape(1, k), w)


def _mm_kernel(x_ref, w_ref, o_ref):
    o_ref[...] = _dot(x_ref[...], w_ref[...]).astype(o_ref.dtype)


def _mm(x, w, out_dtype):
    t, k = x.shape
    n = w.shape[1]
    tm, tn = min(MM_TM, t), min(MM_TN, n)
    return pl.pallas_call(
        _mm_kernel,
        grid=(t // tm, n // tn),
        in_specs=[pl.BlockSpec((tm, k), lambda i, j: (i, 0)),
                  pl.BlockSpec((k, tn), lambda i, j: (0, j))],
        out_specs=pl.BlockSpec((tm, tn), lambda i, j: (i, j)),
        out_shape=jax.ShapeDtypeStruct((t, n), out_dtype),
        compiler_params=_cparams(("parallel", "parallel")),
        name="mm",
    )(x, w)


def _mm_nt_kernel(x_ref, wt_ref, o_ref, *, sub):
    res = _dot_nt(wt_ref[...], x_ref[...]).astype(o_ref.dtype)
    for s in range(o_ref.shape[0]):
        o_ref[s] = res[:, s * sub:(s + 1) * sub]


def _mm_nt(x, wt, sub):
    t, k = x.shape
    n = wt.shape[0]
    tm, tn = min(MM_TM, t), min(MM_TN, n)
    return pl.pallas_call(
        functools.partial(_mm_nt_kernel, sub=sub),
        grid=(t // tm, n // tn),
        in_specs=[pl.BlockSpec((tm, k), lambda i, j: (i, 0)),
                  pl.BlockSpec((tn, k), lambda i, j: (j, 0))],
        out_specs=pl.BlockSpec((tm // sub, tn, sub), lambda i, j: (i, j, 0)),
        out_shape=jax.ShapeDtypeStruct((t // sub, n, sub), BF16),
        compiler_params=_cparams(("parallel", "parallel")),
        name="mm_nt",
    )(x, wt)


def _outproj_kernel(*refs, n_in, with_norm):
    ys = refs[:n_in]
    ws = refs[n_in:2 * n_in]
    h_ref = refs[2 * n_in]
    pos = 2 * n_in + 1
    acc = h_ref[...]
    for y_ref, w_ref in zip(ys, ws):
        acc = acc + _dot(y_ref[...], w_ref[...])
    if with_norm:
        g_ref, ho_ref, xn_ref = refs[pos], refs[pos + 1], refs[pos + 2]
        ho_ref[...] = acc
        xn_ref[...] = _rms(acc, g_ref[...]).astype(xn_ref.dtype)
    else:
        refs[pos][...] = acc


def _outproj(ys, ws, h, gain=None):
    t, d = h.shape
    tm = min(ROW_TM, t)
    n_in = len(ys)
    with_norm = gain is not None
    in_specs = [pl.BlockSpec((tm, y.shape[1]), lambda i: (i, 0)) for y in ys]
    in_specs += [pl.BlockSpec(w.shape, lambda i: (0, 0)) for w in ws]
    in_specs += [pl.BlockSpec((tm, d), lambda i: (i, 0))]
    args = list(ys) + list(ws) + [h]
    row_spec = pl.BlockSpec((tm, d), lambda i: (i, 0))
    if with_norm:
        in_specs += [pl.BlockSpec((1, d), lambda i: (0, 0))]
        args += [gain.reshape(1, d)]
        out_specs = [row_spec, row_spec]
        out_shape = [jax.ShapeDtypeStruct((t, d), F32), jax.ShapeDtypeStruct((t, d), BF16)]
    else:
        out_specs = row_spec
        out_shape = jax.ShapeDtypeStruct((t, d), F32)
    return pl.pallas_call(
        functools.partial(_outproj_kernel, n_in=n_in, with_norm=with_norm),
        grid=(t // tm,),
        in_specs=in_specs,
        out_specs=out_specs,
        out_shape=out_shape,
        compiler_params=_cparams(("parallel",)),
        name="outproj",
    )(*args)


def _retention_tables(seq, n_heads):
    half = HEAD_DIM // 2
    lane = np.arange(LANES)
    inv_freq = ROPE_BASE ** (-jnp.arange(half, dtype=F32) / half)
    ang = jnp.arange(seq, dtype=F32)[:, None] * inv_freq[None, :]
    cos, sin = jnp.cos(ang), jnp.sin(ang)
    cos_t = jnp.tile(cos, (1, LANES // half))
    sign = np.where((lane % HEAD_DIM) < half, -1.0, 1.0).astype(np.float32)
    sin_t = jnp.tile(sin, (1, LANES // half)) * sign[None, :]
    c = RET_CHUNK
    log_gamma = jnp.log(1.0 - 2.0 ** (-5.0 - jnp.arange(n_heads, dtype=F32)))
    pos = jnp.arange(c, dtype=F32)
    diff = pos[:, None] - pos[None, :]
    intra = jnp.where(diff >= 0.0,
                      jnp.exp(log_gamma[:, None, None] * jnp.maximum(diff, 0.0)), 0.0)
    intra = intra.reshape(n_heads // 2, 2, c, c)
    q_decay = jnp.exp(log_gamma[:, None] * (pos + 1.0))
    k_decay = jnp.exp(log_gamma[:, None] * (c - 1.0 - pos))
    chunk_decay = jnp.exp(log_gamma * c)

    def per_lane(tab):
        tab = tab.reshape(n_heads // 2, 2, c)
        return jnp.repeat(tab.transpose(0, 2, 1), HEAD_DIM, axis=2)

    head_of = lane // HEAD_DIM
    same = (head_of[:, None] == head_of[None, :]).astype(np.float32)
    cd = chunk_decay.reshape(n_heads // 2, 2)
    cd_rows = jnp.repeat(cd, HEAD_DIM, axis=1)
    state_decay = cd_rows[:, :, None] * same[None]
    return cos_t, sin_t, intra, per_lane(q_decay), per_lane(k_decay), state_decay, jnp.asarray(same)


def _retention_kernel(q_ref, k_ref, v_ref, g_ref, cos_ref, sin_ref, intra_ref, qd_ref, kd_ref,
                      sd_ref, same_ref, rn_ref, o_ref, state_ref, *, ts):
    @pl.when(pl.program_id(2) == 0)
    def _():
        state_ref[...] = jnp.zeros_like(state_ref)

    c = RET_CHUNK
    lane = lax.broadcasted_iota(jnp.int32, (1, LANES), 1)
    first_half = (lane % HEAD_DIM) < (HEAD_DIM // 2)
    head0 = lane < HEAD_DIM

    def rot(t, cos, sin):
        swapped = jnp.where(first_half, pltpu.roll(t, LANES - HEAD_DIM // 2, 1),
                            pltpu.roll(t, HEAD_DIM // 2, 1))
        return t * cos + swapped * sin

    for ci in range(ts // c):
        rows = slice(ci * c, (ci + 1) * c)
        cos, sin = cos_ref[rows, :], sin_ref[rows, :]
        q = rot(q_ref[rows, :].astype(F32), cos, sin)
        k = rot(k_ref[rows, :].astype(F32), cos, sin) * (HEAD_DIM ** -0.5)
        v = v_ref[rows, :]
        kb = k.astype(BF16)
        inner = []
        for hd in range(HEADS_PER_BLOCK):
            hmask = head0 if hd == 0 else jnp.logical_not(head0)
            qh = jnp.where(hmask, q, 0.0).astype(BF16)
            scores = _dot_nt(qh, kb) * intra_ref[hd]
            inner.append(_dot(scores.astype(BF16), v))
        state = state_ref[...]
        cross = _dot((q * qd_ref[...]).astype(BF16), state.astype(BF16))
        y = jnp.where(head0, inner[0], inner[1]) + cross
        kv = _dot_tn((k * kd_ref[...]).astype(BF16), v)
        state_ref[...] = state * sd_ref[...] + kv * same_ref[...]

        s0 = jnp.sum(jnp.where(head0, y, 0.0), axis=1, keepdims=True)
        s1 = jnp.sum(jnp.where(head0, 0.0, y), axis=1, keepdims=True)
        d = y - jnp.where(head0, s0, s1) * (1.0 / HEAD_DIM)
        dd = d * d
        v0 = jnp.sum(jnp.where(head0, dd, 0.0), axis=1, keepdims=True)
        v1 = jnp.sum(jnp.where(head0, 0.0, dd), axis=1, keepdims=True)
        var = jnp.where(head0, v0, v1) * (1.0 / HEAD_DIM)
        g = g_ref[rows, :].astype(F32)
        silu = g * (1.0 / (1.0 + jnp.exp(-g)))
        o_ref[rows, :] = (d * lax.rsqrt(var + GROUP_NORM_EPS) * rn_ref[...] * silu).astype(o_ref.dtype)


def _retention(proj, ret_norm, batch, seq, n_heads):
    t = proj.shape[0]
    width = n_heads * HEAD_DIM
    nb = width // LANES
    ts = min(RET_TS, seq)
    ns = seq // ts
    tabs = _retention_tables(seq, n_heads)
    cos_t, sin_t, intra, qd, kd, sd, same = tabs
    c = RET_CHUNK

    def col(group):
        return pl.BlockSpec((ts, LANES), lambda b, hp, si: (b * ns + si, group * nb + hp))

    in_specs = [col(0), col(1), col(2), col(3),
                pl.BlockSpec((ts, LANES), lambda b, hp, si: (si, 0)),
                pl.BlockSpec((ts, LANES), lambda b, hp, si: (si, 0)),
                pl.BlockSpec((None, 2, c, c), lambda b, hp, si: (hp, 0, 0, 0)),
                pl.BlockSpec((None, c, LANES), lambda b, hp, si: (hp, 0, 0)),
                pl.BlockSpec((None, c, LANES), lambda b, hp, si: (hp, 0, 0)),
                pl.BlockSpec((None, LANES, LANES), lambda b, hp, si: (hp, 0, 0)),
                pl.BlockSpec((LANES, LANES), lambda b, hp, si: (0, 0)),
                pl.BlockSpec((1, LANES), lambda b, hp, si: (0, hp))]
    return pl.pallas_call(
        functools.partial(_retention_kernel, ts=ts),
        grid=(batch, nb, ns),
        in_specs=in_specs,
        out_specs=pl.BlockSpec((ts, LANES), lambda b, hp, si: (b * ns + si, hp)),
        out_shape=jax.ShapeDtypeStruct((t, width), BF16),
        scratch_shapes=[pltpu.VMEM((LANES, LANES), F32)],
        compiler_params=_cparams(("parallel", "parallel", "arbitrary")),
        name="retention",
    )(proj, proj, proj, proj, cos_t, sin_t, intra, qd, kd, sd, same, ret_norm.reshape(1, width))


def _head_masks():
    lane = lax.broadcasted_iota(jnp.int32, (1, LANES), 1)
    head0 = lane < HEAD_DIM
    return lane, head0, [head0, jnp.logical_not(head0)]


def _untranspose_heads(acc_t, head0):
    return jnp.where(head0, acc_t[0].T, acc_t[1].T)


def _two_stage_blocks(qi, sub, scores_to, apply_from, carry, upper_diag_first):
    top = 2 * qi + 1
    first, second = ((top, sub), (top - 1, 0)) if upper_diag_first else ((top - 1, 0), (top, sub))
    scores_to(0, *first)
    scores_to(1, *second)
    carry = apply_from(0, first[0], carry)

    def pair(i, carry):
        jb = top - 2 - 2 * i
        scores_to(0, jb, None)
        carry = apply_from(1, jnp.where(i == 0, second[0], jb + 1), carry)
        scores_to(1, jb - 1, None)
        return apply_from(0, jb, carry)

    carry = lax.fori_loop(0, qi, pair, carry)
    return apply_from(1, jnp.where(qi == 0, second[0], 0), carry)


def _sb_kernel(q_ref, k_ref, vt_ref, o_ref, d0_ref, d1_ref, tot0_ref, tot1_ref, *, tq, sub):
    qi = pl.program_id(2)
    d_refs = (d0_ref, d1_ref)
    tot_refs = (tot0_ref, tot1_ref)
    _, head0, hmasks = _head_masks()
    key = lax.broadcasted_iota(jnp.int32, (sub, tq), 0)
    qry = lax.broadcasted_iota(jnp.int32, (sub, tq), 1)
    r = lax.broadcasted_iota(jnp.int32, (sub, sub), 0)
    c = lax.broadcasted_iota(jnp.int32, (sub, sub), 1)
    suffix = jnp.where(c >= r, 1.0, 0.0).astype(BF16)
    qf = q_ref[...].astype(F32) * (HEAD_DIM ** -0.5 * LOG2E)
    qh = [jnp.where(m, qf, 0.0).astype(BF16) for m in hmasks]

    sign_bit = jnp.uint32(0x80000000)

    def scores_to(buf, jb, diag_off):
        k = k_ref[jb]
        for hd in range(HEADS_PER_BLOCK):
            z = _dot_nt(k, qh[hd])
            if diag_off is not None:
                z = jnp.where((key + diag_off) < qry, z, -jnp.inf)
            neg_abs = lax.bitcast_convert_type(lax.bitcast_convert_type(z, jnp.uint32) | sign_bit, F32)
            fail = jnp.maximum(z, 0.0) + jnp.log2(1.0 + jnp.exp2(neg_abs))
            tail = _dot(suffix, fail.astype(BF16))
            d_refs[buf][hd] = z - tail
            tot_refs[buf][hd] = tail[0:1, :]

    def apply_from(buf, jb, carry):
        vt = vt_ref[jb]
        out = []
        for hd in range(HEADS_PER_BLOCK):
            later, acc = carry[hd]
            w = jnp.exp2(d_refs[buf][hd] - later)
            out.append((later + tot_refs[buf][hd], acc + _dot(vt, w.astype(BF16))))
        return out

    carry = [(jnp.zeros((1, tq), F32), jnp.zeros((LANES, tq), F32)) for _ in range(HEADS_PER_BLOCK)]
    carry = _two_stage_blocks(qi, sub, scores_to, apply_from, carry, upper_diag_first=True)
    o_ref[...] = _untranspose_heads([carry[0][1], carry[1][1]], head0).astype(o_ref.dtype)


def _stick_breaking(proj, v_t, batch, seq, n_heads, first_group):
    t = proj.shape[0]
    width = n_heads * HEAD_DIM
    nb = width // LANES
    tq = min(ATT_T, seq)
    sub = tq // 2
    nq = seq // tq
    nk = seq // sub
    assert v_t.shape[2] == sub
    proj_k = proj.reshape(t // sub, sub, proj.shape[1])
    score_buf = pltpu.VMEM((HEADS_PER_BLOCK, sub, tq), F32)
    total_buf = pltpu.VMEM((HEADS_PER_BLOCK, 1, tq), F32)
    q_spec = pl.BlockSpec((tq, LANES), lambda b, hp, qi: (b * nq + qi, first_group * nb + hp))
    k_spec = pl.BlockSpec((nk, sub, LANES), lambda b, hp, qi: (b, 0, (first_group + 1) * nb + hp))
    v_spec = pl.BlockSpec((nk, LANES, sub), lambda b, hp, qi: (b, hp, 0))
    return pl.pallas_call(
        functools.partial(_sb_kernel, tq=tq, sub=sub),
        grid=(batch, nb, nq),
        in_specs=[q_spec, k_spec, v_spec],
        out_specs=pl.BlockSpec((tq, LANES), lambda b, hp, qi: (b * nq + qi, hp)),
        out_shape=jax.ShapeDtypeStruct((t, width), BF16),
        scratch_shapes=[score_buf, score_buf, total_buf, total_buf],
        compiler_params=_cparams(("parallel", "parallel", "arbitrary")),
        name="stick_breaking",
    )(proj, proj_k, v_t)


def _forget_cum_kernel(f_ref, b_ref, col_ref, *, seq):
    x = f_ref[...] + b_ref[...]
    log_f = jnp.minimum(x, 0.0) - jnp.log(1.0 + jnp.exp(-jnp.abs(x)))
    xt = log_f.T
    pos = lax.broadcasted_iota(jnp.int32, xt.shape, 1)
    shift = 1
    while shift < seq:
        xt = xt + jnp.where(pos >= shift, pltpu.roll(xt, shift, 1), 0.0)
        shift *= 2
    col_ref[...] = xt.T


def _forget_cum(f_logit, b_forget, batch, seq):
    t = f_logit.shape[0]
    return pl.pallas_call(
        functools.partial(_forget_cum_kernel, seq=seq),
        grid=(batch,),
        in_specs=[pl.BlockSpec((seq, LANES), lambda b: (b, 0)),
                  pl.BlockSpec((1, LANES), lambda b: (0, 0))],
        out_specs=pl.BlockSpec((seq, LANES), lambda b: (b, 0)),
        out_shape=jax.ShapeDtypeStruct((t, LANES), F32),
        compiler_params=_cparams(("parallel",)),
        name="forget_cum",
    )(f_logit, b_forget)


def _bias_lanes(hd):
    return HEAD_DIM * (1 - hd)


def _with_bias_lanes(x, bias, lane, hmask, hd, is_query):
    pieces = []
    rest = bias
    for _ in range(N_BIAS_PIECES):
        p = rest.astype(BF16).astype(F32)
        pieces.append(p)
        rest = rest - p
    base = _bias_lanes(hd)
    out = jnp.where(hmask, x, 0.0)
    for i, p in enumerate(pieces):
        piece_lane = base + i if is_query else base + N_BIAS_PIECES + i
        one_lane = base + N_BIAS_PIECES + i if is_query else base + i
        out = jnp.where(lane == piece_lane, p if is_query else -p, out)
        out = jnp.where(lane == one_lane, 1.0, out)
    return out.astype(BF16)


def _fox_kernel(q_ref, k_ref, vt_ref, cq_ref, ck_ref, o_ref, kp_ref, vp_ref, s0_ref, s1_ref,
                max0_ref, max1_ref, *, tq, sub, seq):
    hp = pl.program_id(1)
    qi = pl.program_id(2)
    s_refs = (s0_ref, s1_ref)
    max_refs = (max0_ref, max1_ref)
    nk = seq // sub
    lane, head0, hmasks = _head_masks()
    ones_row = [HEAD_DIM, 0]

    def head_bias(cum, hd):
        sel = lane == HEADS_PER_BLOCK * hp + hd
        return jnp.sum(jnp.where(sel, cum, 0.0), axis=1, keepdims=True) * LOG2E

    @pl.when(qi == 0)
    def _():
        kf = k_ref[...].astype(F32)
        ck = ck_ref[...]
        row = lax.broadcasted_iota(jnp.int32, (LANES, sub), 0)
        for hd in range(HEADS_PER_BLOCK):
            kp_ref[hd] = _with_bias_lanes(kf, head_bias(ck, hd), lane, hmasks[hd], hd, False)
            in_head = (row < HEAD_DIM) if hd == 0 else (row >= HEAD_DIM)
            for jb in range(nk):
                vt = vt_ref[jb]
                vp_ref[hd, jb] = jnp.where(in_head, vt, jnp.where(row == ones_row[hd], 1.0, 0.0).astype(BF16))

    qf = q_ref[...].astype(F32) * (HEAD_DIM ** -0.5 * LOG2E)
    cq = cq_ref[...]
    qh = [_with_bias_lanes(qf, head_bias(cq, hd), lane, hmasks[hd], hd, True)
          for hd in range(HEADS_PER_BLOCK)]
    key = lax.broadcasted_iota(jnp.int32, (sub, tq), 0)
    qry = lax.broadcasted_iota(jnp.int32, (sub, tq), 1)

    def scores_to(buf, jb, diag_off):
        ks = pl.multiple_of(jb * sub, sub)
        for hd in range(HEADS_PER_BLOCK):
            s = _dot_nt(kp_ref[hd, pl.ds(ks, sub), :], qh[hd])
            if diag_off is not None:
                s = jnp.where((key + diag_off) <= qry, s, -jnp.inf)
            s_refs[buf][hd] = s
            max_refs[buf][hd] = jnp.max(s, axis=0, keepdims=True)

    def apply_from(buf, jb, carry):
        out = []
        for hd in range(HEADS_PER_BLOCK):
            m, acc = carry[hd]
            m_new = jnp.maximum(m, max_refs[buf][hd])
            p = jnp.exp2(s_refs[buf][hd] - m_new).astype(BF16)
            out.append((m_new, jnp.exp2(m - m_new) * acc + _dot(vp_ref[hd, jb], p)))
        return out

    carry = [(jnp.full((1, tq), -jnp.inf, F32), jnp.zeros((LANES, tq), F32))
             for _ in range(HEADS_PER_BLOCK)]
    carry = _two_stage_blocks(qi, sub, scores_to, apply_from, carry, upper_diag_first=False)
    normed = []
    for hd in range(HEADS_PER_BLOCK):
        acc = carry[hd][1]
        normed.append(acc * (1.0 / acc[ones_row[hd]:ones_row[hd] + 1, :]))
    o_ref[...] = _untranspose_heads(normed, head0).astype(o_ref.dtype)


def _forgetting_attention(proj, v_t, cum_cols, batch, seq, n_heads):
    t = proj.shape[0]
    width = n_heads * HEAD_DIM
    nb = width // LANES
    tq = min(ATT_T, seq)
    sub = tq // 2
    nq = seq // tq
    nk = seq // sub
    assert v_t.shape[2] == sub
    score_buf = pltpu.VMEM((HEADS_PER_BLOCK, sub, tq), F32)
    max_buf = pltpu.VMEM((HEADS_PER_BLOCK, 1, tq), F32)
    q_spec = pl.BlockSpec((tq, LANES), lambda b, hp, qi: (b * nq + qi, hp))
    k_spec = pl.BlockSpec((seq, LANES), lambda b, hp, qi: (b, nb + hp))
    v_spec = pl.BlockSpec((nk, LANES, sub), lambda b, hp, qi: (b, hp, 0))
    cq_spec = pl.BlockSpec((tq, LANES), lambda b, hp, qi: (b * nq + qi, 0))
    ck_spec = pl.BlockSpec((seq, LANES), lambda b, hp, qi: (b, 0))
    return pl.pallas_call(
        functools.partial(_fox_kernel, tq=tq, sub=sub, seq=seq),
        grid=(batch, nb, nq),
        in_specs=[q_spec, k_spec, v_spec, cq_spec, ck_spec],
        out_specs=pl.BlockSpec((tq, LANES), lambda b, hp, qi: (b * nq + qi, hp)),
        out_shape=jax.ShapeDtypeStruct((t, width), BF16),
        scratch_shapes=[pltpu.VMEM((HEADS_PER_BLOCK, seq, LANES), BF16),
                        pltpu.VMEM((HEADS_PER_BLOCK, nk, LANES, sub), BF16),
                        score_buf, score_buf, max_buf, max_buf],
        compiler_params=_cparams(("parallel", "parallel", "arbitrary")),
        name="forgetting_attention",
    )(proj, proj, v_t, cum_cols, cum_cols)


def _swiglu_partial(x, wg_ref, wu_ref, wd_ref):
    g = _dot(x, wg_ref[...])
    u = _dot(x, wu_ref[...])
    a = g * (1.0 / (1.0 + jnp.exp(-g))) * u
    return _dot(a.astype(BF16), wd_ref[...])


def _ffn_dense_kernel(x_ref, wg_ref, wu_ref, wd_ref, h_ref, gain_ref, ho_ref, xn_ref, acc_ref):
    f = pl.program_id(1)

    @pl.when(f == 0)
    def _():
        acc_ref[...] = h_ref[...]

    acc_ref[...] += _swiglu_partial(x_ref[...], wg_ref, wu_ref, wd_ref)

    @pl.when(f == pl.num_programs(1) - 1)
    def _():
        h_new = acc_ref[...]
        ho_ref[...] = h_new
        xn_ref[...] = _rms(h_new, gain_ref[...]).astype(xn_ref.dtype)


def _ffn_dense(xn, wg, wu, wd, h, next_gain):
    t, d = h.shape
    dff = wg.shape[1]
    tm = min(FFN_TM, t)
    tf = dff // FFN_NF
    row = pl.BlockSpec((tm, d), lambda i, f: (i, 0))
    return pl.pallas_call(
        _ffn_dense_kernel,
        grid=(t // tm, FFN_NF),
        in_specs=[row,
                  pl.BlockSpec((d, tf), lambda i, f: (0, f)),
                  pl.BlockSpec((d, tf), lambda i, f: (0, f)),
                  pl.BlockSpec((tf, d), lambda i, f: (f, 0)),
                  row,
                  pl.BlockSpec((1, d), lambda i, f: (0, 0))],
        out_specs=[row, row],
        out_shape=[jax.ShapeDtypeStruct((t, d), F32), jax.ShapeDtypeStruct((t, d), BF16)],
        scratch_shapes=[pltpu.VMEM((tm, d), F32)],
        compiler_params=_cparams(("parallel", "arbitrary")),
        name="ffn_dense",
    )(xn, wg, wu, wd, h, next_gain.reshape(1, d))


def _ffn_grouped_kernel(te_ref, nu_ref, x_ref, wg_ref, wu_ref, wd_ref, o_ref, acc_ref):
    i = pl.program_id(0)
    f = pl.program_id(1)
    used = i < nu_ref[0]

    @pl.when(jnp.logical_and(used, f == 0))
    def _():
        acc_ref[...] = jnp.zeros_like(acc_ref)

    @pl.when(used)
    def _():
        acc_ref[...] += _swiglu_partial(x_ref[...].astype(BF16), wg_ref, wu_ref, wd_ref)

    @pl.when(f == pl.num_programs(1) - 1)
    def _():
        @pl.when(used)
        def _():
            o_ref[...] = acc_ref[...]

        @pl.when(jnp.logical_not(used))
        def _():
            o_ref[...] = jnp.zeros_like(o_ref)


def _ffn_grouped(x_sorted, wg, wu, wd, tile_expert, n_used):
    r, d = x_sorted.shape
    dff = wg.shape[2]
    tm = FFN_TM
    tf = dff // FFN_NF
    grid_spec = pltpu.PrefetchScalarGridSpec(
        num_scalar_prefetch=2,
        grid=(r // tm, FFN_NF),
        in_specs=[pl.BlockSpec((tm, d), lambda i, f, te, nu: (i, 0)),
                  pl.BlockSpec((None, d, tf), lambda i, f, te, nu: (te[i], 0, f)),
                  pl.BlockSpec((None, d, tf), lambda i, f, te, nu: (te[i], 0, f)),
                  pl.BlockSpec((None, tf, d), lambda i, f, te, nu: (te[i], f, 0))],
        out_specs=pl.BlockSpec((tm, d), lambda i, f, te, nu: (i, 0)),
        scratch_shapes=[pltpu.VMEM((tm, d), F32)],
    )
    return pl.pallas_call(
        _ffn_grouped_kernel,
        grid_spec=grid_spec,
        out_shape=jax.ShapeDtypeStruct((r, d), F32),
        compiler_params=_cparams(("arbitrary", "arbitrary")),
        name="ffn_grouped",
    )(tile_expert, n_used, x_sorted, wg, wu, wd)


def _router_kernel(h_ref, gain_ref, wr_ref, xn_ref, route_ref):
    xn = _rms(h_ref[...], gain_ref[...])
    xn_ref[...] = xn
    logits = jnp.dot(xn, wr_ref[...], precision=lax.Precision.HIGHEST, preferred_element_type=F32)
    lane = lax.broadcasted_iota(jnp.int32, logits.shape, 1)
    lane_f = lane.astype(F32)
    lg = jnp.where(lane < N_EXPERTS, logits, -jnp.inf)
    v1 = jnp.max(lg, axis=1, keepdims=True)
    i1 = jnp.min(jnp.where(lg == v1, lane_f, float(LANES)), axis=1, keepdims=True)
    lg2 = jnp.where(lane_f == i1, -jnp.inf, lg)
    v2 = jnp.max(lg2, axis=1, keepdims=True)
    i2 = jnp.min(jnp.where(lg2 == v2, lane_f, float(LANES)), axis=1, keepdims=True)
    e2 = jnp.exp(v2 - v1)
    g1 = 1.0 / (1.0 + e2)
    g2 = e2 / (1.0 + e2)
    route_ref[...] = jnp.where(lane == 0, i1, jnp.where(lane == 1, i2,
                               jnp.where(lane == 2, g1, jnp.where(lane == 3, g2, 0.0))))


def _router(h, gain, w_router):
    t, d = h.shape
    tm = min(ROW_TM, t)
    wr = jnp.zeros((d, LANES), F32).at[:, :N_EXPERTS].set(w_router)
    row = pl.BlockSpec((tm, d), lambda i: (i, 0))
    return pl.pallas_call(
        _router_kernel,
        grid=(t // tm,),
        in_specs=[row, pl.BlockSpec((1, d), lambda i: (0, 0)),
                  pl.BlockSpec((d, LANES), lambda i: (0, 0))],
        out_specs=[row, pl.BlockSpec((tm, LANES), lambda i: (i, 0))],
        out_shape=[jax.ShapeDtypeStruct((t, d), F32), jax.ShapeDtypeStruct((t, LANES), F32)],
        compiler_params=_cparams(("parallel",)),
        name="router",
    )(h, gain.reshape(1, d), wr)


def _scatter_kernel(dest_ref, x_ref, init_hbm, o_hbm, sem, *, tm):
    del init_hbm

    def row_copy(r, kk):
        return pltpu.make_async_copy(x_ref.at[pl.ds(r, 1), :],
                                     o_hbm.at[pl.ds(dest_ref[0, 0, TOP_K * r + kk], 1), :], sem)

    def issue(r, carry):
        for kk in range(TOP_K):
            row_copy(r, kk).start()
        return carry

    lax.fori_loop(0, tm, issue, 0)

    def wait(r, carry):
        for kk in range(TOP_K):
            row_copy(r, kk).wait()
        return carry

    lax.fori_loop(0, tm, wait, 0)


def _scatter_rows(xn, dest, n_rows):
    t, d = xn.shape
    tm = min(GATHER_TM, t)
    nt = t // tm
    dest3 = dest.reshape(nt, 1, TOP_K * tm)
    init = jnp.zeros((n_rows, d), F32)
    return pl.pallas_call(
        functools.partial(_scatter_kernel, tm=tm),
        grid=(nt,),
        in_specs=[pl.BlockSpec((1, 1, TOP_K * tm), lambda i: (i, 0, 0), memory_space=pltpu.SMEM),
                  pl.BlockSpec((tm, d), lambda i: (i, 0)),
                  pl.BlockSpec(memory_space=pl.ANY)],
        out_specs=pl.BlockSpec(memory_space=pl.ANY),
        out_shape=jax.ShapeDtypeStruct((n_rows, d), F32),
        scratch_shapes=[pltpu.SemaphoreType.DMA(())],
        input_output_aliases={2: 0},
        compiler_params=_cparams(("arbitrary",)),
        name="moe_scatter",
    )(dest3, xn, init)


def _combine_kernel(dcur_ref, dnext_ref, y_hbm, h_ref, route_ref, gain_ref, o_ref, buf, sems, *, tm):
    i = pl.program_id(0)
    n = pl.num_programs(0)
    slot = i % 2

    def row_copy(d_ref, s, r, kk):
        return pltpu.make_async_copy(y_hbm.at[pl.ds(d_ref[0, 0, TOP_K * r + kk], 1), :],
                                     buf.at[s, kk, pl.ds(r, 1), :], sems.at[s])

    def issue(d_ref, s):
        def body(r, carry):
            for kk in range(TOP_K):
                row_copy(d_ref, s, r, kk).start()
            return carry
        lax.fori_loop(0, tm, body, 0)

    @pl.when(i == 0)
    def _():
        issue(dcur_ref, slot)

    @pl.when(i + 1 < n)
    def _():
        issue(dnext_ref, 1 - slot)

    def wait(r, carry):
        for kk in range(TOP_K):
            row_copy(dcur_ref, slot, r, kk).wait()
        return carry
    lax.fori_loop(0, tm, wait, 0)

    route = route_ref[...]
    lane = lax.broadcasted_iota(jnp.int32, route.shape, 1)
    g1 = jnp.sum(jnp.where(lane == 2, route, 0.0), axis=1, keepdims=True)
    g2 = jnp.sum(jnp.where(lane == 3, route, 0.0), axis=1, keepdims=True)
    out = h_ref[...] + (g1 * buf[slot, 0] + g2 * buf[slot, 1])
    o_ref[...] = _rms(out, gain_ref[...])


def _combine(y_sorted, dest, h, route, final_gain):
    t, d = h.shape
    tm = min(GATHER_TM, t)
    nt = t // tm
    dest3 = dest.reshape(nt, 1, TOP_K * tm)
    row = pl.BlockSpec((tm, d), lambda i: (i, 0))
    smem = functools.partial(pl.BlockSpec, (1, 1, TOP_K * tm), memory_space=pltpu.SMEM)
    return pl.pallas_call(
        functools.partial(_combine_kernel, tm=tm),
        grid=(nt,),
        in_specs=[smem(index_map=lambda i: (i, 0, 0)),
                  smem(index_map=lambda i: (jnp.minimum(i + 1, nt - 1), 0, 0)),
                  pl.BlockSpec(memory_space=pl.ANY),
                  row,
                  pl.BlockSpec((tm, LANES), lambda i: (i, 0)),
                  pl.BlockSpec((1, d), lambda i: (0, 0))],
        out_specs=row,
        out_shape=jax.ShapeDtypeStruct((t, d), F32),
        scratch_shapes=[pltpu.VMEM((2, TOP_K, tm, d), F32), pltpu.SemaphoreType.DMA((2,))],
        compiler_params=_cparams(("arbitrary",)),
        name="moe_combine",
    )(dest3, dest3, y_sorted, h, route, final_gain.reshape(1, d))


def _routing_plan(expert_idx, tm):
    t = expert_idx.shape[0]
    flat = expert_idx.reshape(-1)
    onehot = (flat[:, None] == jnp.arange(N_EXPERTS, dtype=jnp.int32)[None, :]).astype(jnp.int32)
    running = jnp.cumsum(onehot, axis=0)
    rank = jnp.sum((running - onehot) * onehot, axis=1)
    counts = running[-1]
    tiles = (counts + tm - 1) // tm
    tile_end = jnp.cumsum(tiles)
    group_start = (tile_end - tiles) * tm
    dest = jnp.sum(onehot * group_start[None, :], axis=1) + rank
    n_tiles = (t * TOP_K) // tm + N_EXPERTS
    n_used = tile_end[-1]
    tile_id = jnp.minimum(jnp.arange(n_tiles, dtype=jnp.int32), n_used - 1)
    tile_expert = jnp.sum((tile_id[:, None] >= tile_end[None, :]).astype(jnp.int32), axis=1)
    return dest.reshape(t, TOP_K).astype(jnp.int32), tile_expert.astype(jnp.int32), \
        n_used.reshape(1).astype(jnp.int32), n_tiles * tm


def kernel(x, attn_norm_even, w_in_even, ret_norm_even, w_out_even, ffn_norm_even, w_gate_even, w_up_even, w_down_even, attn_norm_odd, w_in_odd, b_forget_odd, w_out_odd, ffn_norm_odd, w_router_odd, w_gate_moe_odd, w_up_moe_odd, w_down_moe_odd, final_norm):
    batch, seq, d = x.shape
    t = batch * seq
    n_ret = d // (2 * HEAD_DIM)
    n_sb = d // (2 * HEAD_DIM)
    n_fox = d // HEAD_DIM
    ret_width = n_ret * HEAD_DIM
    sb_width = n_sb * HEAD_DIM
    fox_width = n_fox * HEAD_DIM
    sub = min(ATT_T, seq) // 2
    h0 = x.reshape(t, d)

    w_in = w_in_even[0]
    n_direct = 4 * ret_width + 2 * sb_width
    proj, xn0 = _norm_mm(h0, attn_norm_even[0], w_in[:, :n_direct].astype(BF16), BF16)
    v_sb_t = _mm_nt(xn0, w_in[:, n_direct:].T.astype(BF16), sub)
    y_ret = _retention(proj, ret_norm_even[0], batch, seq, n_ret)
    y_sb = _stick_breaking(proj, v_sb_t, batch, seq, n_sb, first_group=4)
    w_out = w_out_even[0].astype(BF16)
    h1, xn1 = _outproj([y_ret, y_sb], [w_out[:ret_width], w_out[ret_width:]], h0, ffn_norm_even[0])
    h2, xn2 = _ffn_dense(xn1, w_gate_even[0].astype(BF16), w_up_even[0].astype(BF16),
                         w_down_even[0].astype(BF16), h1, attn_norm_odd[0])

    w_in = w_in_odd[0]
    proj = _mm(xn2, w_in[:, :2 * fox_width].astype(BF16), BF16)
    v_fox_t = _mm_nt(xn2, w_in[:, 2 * fox_width:3 * fox_width].T.astype(BF16), sub)
    w_f = jnp.zeros((d, LANES), BF16).at[:, :n_fox].set(w_in[:, 3 * fox_width:].astype(BF16))
    f_logit = _mm(xn2, w_f, F32)
    b_f = jnp.zeros((1, LANES), F32).at[0, :n_fox].set(b_forget_odd[0])
    cum_cols = _forget_cum(f_logit, b_f, batch, seq)
    y_fox = _forgetting_attention(proj, v_fox_t, cum_cols, batch, seq, n_fox)
    h3 = _outproj([y_fox], [w_out_odd[0].astype(BF16)], h2)

    xn3, route = _router(h3, ffn_norm_odd[0], w_router_odd[0])
    expert_idx = route[:, :TOP_K].astype(jnp.int32)
    dest, tile_expert, n_used, n_rows = _routing_plan(expert_idx, FFN_TM)
    x_sorted = _scatter_rows(xn3, dest, n_rows)
    y_sorted = _ffn_grouped(x_sorted, w_gate_moe_odd[0].astype(BF16), w_up_moe_odd[0].astype(BF16),
                            w_down_moe_odd[0].astype(BF16), tile_expert, n_used)
    out = _combine(y_sorted, dest, h3, route, final_norm)
    return out.reshape(batch, seq, d)
```

```python
import functools

import jax
import jax.numpy as jnp
import numpy as np
from jax import lax
from jax.experimental import pallas as pl
from jax.experimental.pallas import tpu as pltpu

F32 = jnp.float32
BF16 = jnp.bfloat16

LANES = 128
HEAD_DIM = 64
HEADS_PER_BLOCK = LANES // HEAD_DIM
N_EXPERTS = 8
TOP_K = 2
ROPE_BASE = 10000.0
NORM_EPS = 1e-6
GROUP_NORM_EPS = 1e-5
LOG2E = 1.4426950408889634
RET_CHUNK = 256
VMEM_LIMIT = 56 * 1024 * 1024

MM_TM = 1024
MM_TN = 1024
ROW_TM = 512
ATT_T = 512
ATT_COLS = 2
RET_TS = 512
FFN_TM = 256
FFN_NF = 1
GATHER_TM = 256
N_BIAS_PIECES = 3


def _cparams(sem, vmem=VMEM_LIMIT):
    return pltpu.CompilerParams(dimension_semantics=sem, vmem_limit_bytes=vmem)


def _rms(xf, gain_row):
    ms = jnp.mean(xf * xf, axis=-1, keepdims=True)
    return xf * lax.rsqrt(ms + NORM_EPS) * gain_row


def _dot(a, b):
    return jnp.dot(a, b, preferred_element_type=F32)


def _dot_nt(a, b):
    return lax.dot_general(a, b, (((1,), (1,)), ((), ())), preferred_element_type=F32)


def _dot_tn(a, b):
    return lax.dot_general(a, b, (((0,), (0,)), ((), ())), preferred_element_type=F32)


def _norm_mm_kernel(x_ref, g_ref, w_ref, o_ref, xn_ref):
    @pl.when(pl.program_id(1) == 0)
    def _():
        xn_ref[...] = _rms(x_ref[...], g_ref[...]).astype(BF16)

    o_ref[...] = _dot(xn_ref[...], w_ref[...]).astype(o_ref.dtype)


def _norm_mm(x, gain, w, out_dtype):
    t, k = x.shape
    n = w.shape[1]
    tm, tn = min(MM_TM, t), min(MM_TN, n)
    return pl.pallas_call(
        _norm_mm_kernel,
        grid=(t // tm, n // tn),
        in_specs=[pl.BlockSpec((tm, k), lambda i, j: (i, 0)),
                  pl.BlockSpec((1, k), lambda i, j: (0, 0)),
                  pl.BlockSpec((k, tn), lambda i, j: (0, j))],
        out_specs=[pl.BlockSpec((tm, tn), lambda i, j: (i, j)),
                   pl.BlockSpec((tm, k), lambda i, j: (i, 0))],
        out_shape=[jax.ShapeDtypeStruct((t, n), out_dtype), jax.ShapeDtypeStruct((t, k), BF16)],
        compiler_params=_cparams(("parallel", "arbitrary")),
        name="norm_mm",
    )(x, gain.reshape(1, k), w)


def _mm_kernel(x_ref, w_ref, o_ref):
    o_ref[...] = _dot(x_ref[...], w_ref[...]).astype(o_ref.dtype)


def _mm(x, w, out_dtype):
    t, k = x.shape
    n = w.shape[1]
    tm, tn = min(MM_TM, t), min(MM_TN, n)
    return pl.pallas_call(
        _mm_kernel,
        grid=(t // tm, n // tn),
        in_specs=[pl.BlockSpec((tm, k), lambda i, j: (i, 0)),
                  pl.BlockSpec((k, tn), lambda i, j: (0, j))],
        out_specs=pl.BlockSpec((tm, tn), lambda i, j: (i, j)),
        out_shape=jax.ShapeDtypeStruct((t, n), out_dtype),
        compiler_params=_cparams(("parallel", "parallel")),
        name="mm",
    )(x, w)


def _mm_nt_kernel(x_ref, wt_ref, o_ref, *, sub):
    res = _dot_nt(wt_ref[...], x_ref[...]).astype(o_ref.dtype)
    for s in range(o_ref.shape[0]):
        o_ref[s] = res[:, s * sub:(s + 1) * sub]


def _mm_nt(x, wt, sub):
    t, k = x.shape
    n = wt.shape[0]
    tm, tn = min(MM_TM, t), min(MM_TN, n)
    return pl.pallas_call(
        functools.partial(_mm_nt_kernel, sub=sub),
        grid=(t // tm, n // tn),
        in_specs=[pl.BlockSpec((tm, k), lambda i, j: (i, 0)),
                  pl.BlockSpec((tn, k), lambda i, j: (j, 0))],
        out_specs=pl.BlockSpec((tm // sub, tn, sub), lambda i, j: (i, j, 0)),
        out_shape=jax.ShapeDtypeStruct((t // sub, n, sub), BF16),
        compiler_params=_cparams(("parallel", "parallel")),
        name="mm_nt",
    )(x, wt)


def _outproj_kernel(*refs, n_in, with_norm):
    ys = refs[:n_in]
    ws = refs[n_in:2 * n_in]
    h_ref = refs[2 * n_in]
    pos = 2 * n_in + 1
    acc = h_ref[...]
    for y_ref, w_ref in zip(ys, ws):
        acc = acc + _dot(y_ref[...], w_ref[...])
    if with_norm:
        g_ref, ho_ref, xn_ref = refs[pos], refs[pos + 1], refs[pos + 2]
        ho_ref[...] = acc
        xn_ref[...] = _rms(acc, g_ref[...]).astype(xn_ref.dtype)
    else:
        refs[pos][...] = acc


def _outproj(ys, ws, h, gain=None):
    t, d = h.shape
    tm = min(ROW_TM, t)
    n_in = len(ys)
    with_norm = gain is not None
    in_specs = [pl.BlockSpec((tm, y.shape[1]), lambda i: (i, 0)) for y in ys]
    in_specs += [pl.BlockSpec(w.shape, lambda i: (0, 0)) for w in ws]
    in_specs += [pl.BlockSpec((tm, d), lambda i: (i, 0))]
    args = list(ys) + list(ws) + [h]
    row_spec = pl.BlockSpec((tm, d), lambda i: (i, 0))
    if with_norm:
        in_specs += [pl.BlockSpec((1, d), lambda i: (0, 0))]
        args += [gain.reshape(1, d)]
        out_specs = [row_spec, row_spec]
        out_shape = [jax.ShapeDtypeStruct((t, d), F32), jax.ShapeDtypeStruct((t, d), BF16)]
    else:
        out_specs = row_spec
        out_shape = jax.ShapeDtypeStruct((t, d), F32)
    return pl.pallas_call(
        functools.partial(_outproj_kernel, n_in=n_in, with_norm=with_norm),
        grid=(t // tm,),
        in_specs=in_specs,
        out_specs=out_specs,
        out_shape=out_shape,
        compiler_params=_cparams(("parallel",)),
        name="outproj",
    )(*args)


def _retention_tables(seq, n_heads):
    half = HEAD_DIM // 2
    lane = np.arange(LANES)
    inv_freq = ROPE_BASE ** (-jnp.arange(half, dtype=F32) / half)
    ang = jnp.arange(seq, dtype=F32)[:, None] * inv_freq[None, :]
    cos, sin = jnp.cos(ang), jnp.sin(ang)
    cos_t = jnp.tile(cos, (1, LANES // half))
    sign = np.where((lane % HEAD_DIM) < half, -1.0, 1.0).astype(np.float32)
    sin_t = jnp.tile(sin, (1, LANES // half)) * sign[None, :]
    c = RET_CHUNK
    log_gamma = jnp.log(1.0 - 2.0 ** (-5.0 - jnp.arange(n_heads, dtype=F32)))
    pos = jnp.arange(c, dtype=F32)
    diff = pos[:, None] - pos[None, :]
    intra = jnp.where(diff >= 0.0,
                      jnp.exp(log_gamma[:, None, None] * jnp.maximum(diff, 0.0)), 0.0)
    intra = intra.reshape(n_heads // 2, 2, c, c)
    q_decay = jnp.exp(log_gamma[:, None] * (pos + 1.0))
    k_decay = jnp.exp(log_gamma[:, None] * (c - 1.0 - pos))
    chunk_decay = jnp.exp(log_gamma * c)

    def per_lane(tab):
        tab = tab.reshape(n_heads // 2, 2, c)
        return jnp.repeat(tab.transpose(0, 2, 1), HEAD_DIM, axis=2)

    head_of = lane // HEAD_DIM
    same = (head_of[:, None] == head_of[None, :]).astype(np.float32)
    cd = chunk_decay.reshape(n_heads // 2, 2)
    cd_rows = jnp.repeat(cd, HEAD_DIM, axis=1)
    state_decay = cd_rows[:, :, None] * same[None]
    return cos_t, sin_t, intra, per_lane(q_decay), per_lane(k_decay), state_decay, jnp.asarray(same)


def _retention_kernel(q_ref, k_ref, v_ref, g_ref, cos_ref, sin_ref, intra_ref, qd_ref, kd_ref,
                      sd_ref, same_ref, rn_ref, o_ref, state_ref, *, ts):
    @pl.when(pl.program_id(2) == 0)
    def _():
        state_ref[...] = jnp.zeros_like(state_ref)

    c = RET_CHUNK
    lane = lax.broadcasted_iota(jnp.int32, (1, LANES), 1)
    first_half = (lane % HEAD_DIM) < (HEAD_DIM // 2)
    head0 = lane < HEAD_DIM

    def rot(t, cos, sin):
        swapped = jnp.where(first_half, pltpu.roll(t, LANES - HEAD_DIM // 2, 1),
                            pltpu.roll(t, HEAD_DIM // 2, 1))
        return t * cos + swapped * sin

    for ci in range(ts // c):
        rows = slice(ci * c, (ci + 1) * c)
        cos, sin = cos_ref[rows, :], sin_ref[rows, :]
        q = rot(q_ref[rows, :].astype(F32), cos, sin)
        k = rot(k_ref[rows, :].astype(F32), cos, sin) * (HEAD_DIM ** -0.5)
        v = v_ref[rows, :]
        kb = k.astype(BF16)
        inner = []
        for hd in range(HEADS_PER_BLOCK):
            hmask = head0 if hd == 0 else jnp.logical_not(head0)
            qh = jnp.where(hmask, q, 0.0).astype(BF16)
            scores = _dot_nt(qh, kb) * intra_ref[hd]
            inner.append(_dot(scores.astype(BF16), v))
        state = state_ref[...]
        cross = _dot((q * qd_ref[...]).astype(BF16), state.astype(BF16))
        y = jnp.where(head0, inner[0], inner[1]) + cross
        kv = _dot_tn((k * kd_ref[...]).astype(BF16), v)
        state_ref[...] = state * sd_ref[...] + kv * same_ref[...]

        s0 = jnp.sum(jnp.where(head0, y, 0.0), axis=1, keepdims=True)
        s1 = jnp.sum(jnp.where(head0, 0.0, y), axis=1, keepdims=True)
        d = y - jnp.where(head0, s0, s1) * (1.0 / HEAD_DIM)
        dd = d * d
        v0 = jnp.sum(jnp.where(head0, dd, 0.0), axis=1, keepdims=True)
        v1 = jnp.sum(jnp.where(head0, 0.0, dd), axis=1, keepdims=True)
        var = jnp.where(head0, v0, v1) * (1.0 / HEAD_DIM)
        g = g_ref[rows, :].astype(F32)
        silu = g * (1.0 / (1.0 + jnp.exp(-g)))
        o_ref[rows, :] = (d * lax.rsqrt(var + GROUP_NORM_EPS) * rn_ref[...] * silu).astype(o_ref.dtype)


def _retention(proj, ret_norm, batch, seq, n_heads):
    t = proj.shape[0]
    width = n_heads * HEAD_DIM
    nb = width // LANES
    ts = min(RET_TS, seq)
    ns = seq // ts
    tabs = _retention_tables(seq, n_heads)
    cos_t, sin_t, intra, qd, kd, sd, same = tabs
    c = RET_CHUNK

    def col(group):
        return pl.BlockSpec((ts, LANES), lambda b, hp, si: (b * ns + si, group * nb + hp))

    in_specs = [col(0), col(1), col(2), col(3),
                pl.BlockSpec((ts, LANES), lambda b, hp, si: (si, 0)),
                pl.BlockSpec((ts, LANES), lambda b, hp, si: (si, 0)),
                pl.BlockSpec((None, 2, c, c), lambda b, hp, si: (hp, 0, 0, 0)),
                pl.BlockSpec((None, c, LANES), lambda b, hp, si: (hp, 0, 0)),
                pl.BlockSpec((None, c, LANES), lambda b, hp, si: (hp, 0, 0)),
                pl.BlockSpec((None, LANES, LANES), lambda b, hp, si: (hp, 0, 0)),
                pl.BlockSpec((LANES, LANES), lambda b, hp, si: (0, 0)),
                pl.BlockSpec((1, LANES), lambda b, hp, si: (0, hp))]
    return pl.pallas_call(
        functools.partial(_retention_kernel, ts=ts),
        grid=(batch, nb, ns),
        in_specs=in_specs,
        out_specs=pl.BlockSpec((ts, LANES), lambda b, hp, si: (b * ns + si, hp)),
        out_shape=jax.ShapeDtypeStruct((t, width), BF16),
        scratch_shapes=[pltpu.VMEM((LANES, LANES), F32)],
        compiler_params=_cparams(("parallel", "parallel", "arbitrary")),
        name="retention",
    )(proj, proj, proj, proj, cos_t, sin_t, intra, qd, kd, sd, same, ret_norm.reshape(1, width))


def _head_masks():
    lane = lax.broadcasted_iota(jnp.int32, (1, LANES), 1)
    head0 = lane < HEAD_DIM
    return lane, head0, [head0, jnp.logical_not(head0)]


def _col_block(x, hd):
    cb = hd // HEADS_PER_BLOCK
    return x[:, cb * LANES:(cb + 1) * LANES]


def _row_block(x, hd):
    cb = hd // HEADS_PER_BLOCK
    return x[cb * LANES:(cb + 1) * LANES]


def _store_heads(o_ref, acc_t, head0):
    for cb in range(ATT_COLS):
        pair = jnp.where(head0, acc_t[HEADS_PER_BLOCK * cb].T, acc_t[HEADS_PER_BLOCK * cb + 1].T)
        o_ref[:, cb * LANES:(cb + 1) * LANES] = pair.astype(o_ref.dtype)


def _two_stage_blocks(qi, sub, scores_to, apply_from, carry, upper_diag_first):
    top = 2 * qi + 1
    first, second = ((top, sub), (top - 1, 0)) if upper_diag_first else ((top - 1, 0), (top, sub))
    scores_to(0, *first)
    scores_to(1, *second)
    carry = apply_from(0, first[0], carry)

    def pair(i, carry):
        jb = top - 2 - 2 * i
        scores_to(0, jb, None)
        carry = apply_from(1, jnp.where(i == 0, second[0], jb + 1), carry)
        scores_to(1, jb - 1, None)
        return apply_from(0, jb, carry)

    carry = lax.fori_loop(0, qi, pair, carry)
    return apply_from(1, jnp.where(qi == 0, second[0], 0), carry)


def _sb_kernel(q_ref, k_ref, vt_ref, o_ref, d0_ref, d1_ref, tot0_ref, tot1_ref, *, tq, sub):
    qi = pl.program_id(2)
    d_refs = (d0_ref, d1_ref)
    tot_refs = (tot0_ref, tot1_ref)
    _, head0, hmasks = _head_masks()
    key = lax.broadcasted_iota(jnp.int32, (sub, tq), 0)
    qry = lax.broadcasted_iota(jnp.int32, (sub, tq), 1)
    r = lax.broadcasted_iota(jnp.int32, (sub, sub), 0)
    c = lax.broadcasted_iota(jnp.int32, (sub, sub), 1)
    suffix = jnp.where(c >= r, 1.0, 0.0).astype(BF16)
    qf = q_ref[...].astype(F32) * (HEAD_DIM ** -0.5 * LOG2E)
    heads = range(ATT_COLS * HEADS_PER_BLOCK)
    qh = [jnp.where(hmasks[hd % HEADS_PER_BLOCK], _col_block(qf, hd), 0.0).astype(BF16) for hd in heads]

    sign_bit = jnp.uint32(0x80000000)

    def scores_to(buf, jb, diag_off):
        k = k_ref[jb]
        for hd in heads:
            z = _dot_nt(_col_block(k, hd), qh[hd])
            if diag_off is not None:
                z = jnp.where((key + diag_off) < qry, z, -jnp.inf)
            neg_abs = lax.bitcast_convert_type(lax.bitcast_convert_type(z, jnp.uint32) | sign_bit, F32)
            fail = jnp.maximum(z, 0.0) + jnp.log2(1.0 + jnp.exp2(neg_abs))
            tail = _dot(suffix, fail.astype(BF16))
            d_refs[buf][hd] = z - tail
            tot_refs[buf][hd] = tail[0:1, :]

    def apply_from(buf, jb, carry):
        vt = vt_ref[jb]
        out = []
        for hd in heads:
            later, acc = carry[hd]
            w = jnp.exp2(d_refs[buf][hd] - later)
            out.append((later + tot_refs[buf][hd], acc + _dot(_row_block(vt, hd), w.astype(BF16))))
        return out

    carry = [(jnp.zeros((1, tq), F32), jnp.zeros((LANES, tq), F32)) for _ in heads]
    carry = _two_stage_blocks(qi, sub, scores_to, apply_from, carry, upper_diag_first=True)
    _store_heads(o_ref, [c[1] for c in carry], head0)


def _stick_breaking(proj, v_t, batch, seq, n_heads, first_group):
    t = proj.shape[0]
    width = n_heads * HEAD_DIM
    nb = width // LANES
    tq = min(ATT_T, seq)
    sub = tq // 2
    nq = seq // tq
    nk = seq // sub
    assert v_t.shape[2] == sub
    proj_k = proj.reshape(t // sub, sub, proj.shape[1])
    n_step = ATT_COLS * HEADS_PER_BLOCK
    cols = ATT_COLS * LANES
    ng = nb // ATT_COLS
    score_buf = pltpu.VMEM((n_step, sub, tq), F32)
    total_buf = pltpu.VMEM((n_step, 1, tq), F32)
    q_spec = pl.BlockSpec((tq, cols), lambda b, hp, qi: (b * nq + qi, first_group * ng + hp))
    k_spec = pl.BlockSpec((nk, sub, cols), lambda b, hp, qi: (b, 0, (first_group + 1) * ng + hp))
    v_spec = pl.BlockSpec((nk, cols, sub), lambda b, hp, qi: (b, hp, 0))
    return pl.pallas_call(
        functools.partial(_sb_kernel, tq=tq, sub=sub),
        grid=(batch, ng, nq),
        in_specs=[q_spec, k_spec, v_spec],
        out_specs=pl.BlockSpec((tq, cols), lambda b, hp, qi: (b * nq + qi, hp)),
        out_shape=jax.ShapeDtypeStruct((t, width), BF16),
        scratch_shapes=[score_buf, score_buf, total_buf, total_buf],
        compiler_params=_cparams(("parallel", "parallel", "arbitrary")),
        name="stick_breaking",
    )(proj, proj_k, v_t)


def _forget_cum_kernel(f_ref, b_ref, col_ref, *, seq):
    x = f_ref[...] + b_ref[...]
    log_f = jnp.minimum(x, 0.0) - jnp.log(1.0 + jnp.exp(-jnp.abs(x)))
    xt = log_f.T
    pos = lax.broadcasted_iota(jnp.int32, xt.shape, 1)
    shift = 1
    while shift < seq:
        xt = xt + jnp.where(pos >= shift, pltpu.roll(xt, shift, 1), 0.0)
        shift *= 2
    col_ref[...] = xt.T


def _forget_cum(f_logit, b_forget, batch, seq):
    t = f_logit.shape[0]
    return pl.pallas_call(
        functools.partial(_forget_cum_kernel, seq=seq),
        grid=(batch,),
        in_specs=[pl.BlockSpec((seq, LANES), lambda b: (b, 0)),
                  pl.BlockSpec((1, LANES), lambda b: (0, 0))],
        out_specs=pl.BlockSpec((seq, LANES), lambda b: (b, 0)),
        out_shape=jax.ShapeDtypeStruct((t, LANES), F32),
        compiler_params=_cparams(("parallel",)),
        name="forget_cum",
    )(f_logit, b_forget)


def _bias_lanes(hd):
    return HEAD_DIM * (1 - hd)


def _with_bias_lanes(x, bias, lane, hmask, hd, is_query):
    pieces = []
    rest = bias
    for _ in range(N_BIAS_PIECES):
        p = rest.astype(BF16).astype(F32)
        pieces.append(p)
        rest = rest - p
    base = _bias_lanes(hd)
    out = jnp.where(hmask, x, 0.0)
    for i, p in enumerate(pieces):
        piece_lane = base + i if is_query else base + N_BIAS_PIECES + i
        one_lane = base + N_BIAS_PIECES + i if is_query else base + i
        out = jnp.where(lane == piece_lane, p if is_query else -p, out)
        out = jnp.where(lane == one_lane, 1.0, out)
    return out.astype(BF16)


def _fox_kernel(q_ref, k_ref, vt_ref, cq_ref, ck_ref, o_ref, kp_ref, vp_ref, s0_ref, s1_ref,
                max0_ref, max1_ref, *, tq, sub, seq):
    hp = pl.program_id(1)
    qi = pl.program_id(2)
    s_refs = (s0_ref, s1_ref)
    max_refs = (max0_ref, max1_ref)
    nk = seq // sub
    lane, head0, hmasks = _head_masks()
    heads = range(ATT_COLS * HEADS_PER_BLOCK)
    ones_row = [HEAD_DIM, 0]

    def head_bias(cum, hd):
        sel = lane == ATT_COLS * HEADS_PER_BLOCK * hp + hd
        return jnp.sum(jnp.where(sel, cum, 0.0), axis=1, keepdims=True) * LOG2E

    @pl.when(qi == 0)
    def _():
        kf = k_ref[...].astype(F32)
        ck = ck_ref[...]
        row = lax.broadcasted_iota(jnp.int32, (LANES, sub), 0)
        for hd in heads:
            side = hd % HEADS_PER_BLOCK
            kp_ref[hd] = _with_bias_lanes(_col_block(kf, hd), head_bias(ck, hd), lane, hmasks[side], side, False)
            in_head = (row < HEAD_DIM) if side == 0 else (row >= HEAD_DIM)
            ones = jnp.where(row == ones_row[side], 1.0, 0.0).astype(BF16)
            for jb in range(nk):
                vp_ref[hd, jb] = jnp.where(in_head, _row_block(vt_ref[jb], hd), ones)

    qf = q_ref[...].astype(F32) * (HEAD_DIM ** -0.5 * LOG2E)
    cq = cq_ref[...]
    qh = [_with_bias_lanes(_col_block(qf, hd), head_bias(cq, hd), lane, hmasks[hd % HEADS_PER_BLOCK],
                           hd % HEADS_PER_BLOCK, True) for hd in heads]
    key = lax.broadcasted_iota(jnp.int32, (sub, tq), 0)
    qry = lax.broadcasted_iota(jnp.int32, (sub, tq), 1)

    def scores_to(buf, jb, diag_off):
        ks = pl.multiple_of(jb * sub, sub)
        for hd in heads:
            s = _dot_nt(kp_ref[hd, pl.ds(ks, sub), :], qh[hd])
            if diag_off is not None:
                s = jnp.where((key + diag_off) <= qry, s, -jnp.inf)
            s_refs[buf][hd] = s
            max_refs[buf][hd] = jnp.max(s, axis=0, keepdims=True)

    def apply_from(buf, jb, carry):
        out = []
        for hd in heads:
            m, acc = carry[hd]
            m_new = jnp.maximum(m, max_refs[buf][hd])
            p = jnp.exp2(s_refs[buf][hd] - m_new).astype(BF16)
            out.append((m_new, jnp.exp2(m - m_new) * acc + _dot(vp_ref[hd, jb], p)))
        return out

    carry = [(jnp.full((1, tq), -jnp.inf, F32), jnp.zeros((LANES, tq), F32)) for _ in heads]
    carry = _two_stage_blocks(qi, sub, scores_to, apply_from, carry, upper_diag_first=False)
    normed = []
    for hd in heads:
        acc = carry[hd][1]
        r = ones_row[hd % HEADS_PER_BLOCK]
        normed.append(acc * (1.0 / acc[r:r + 1, :]))
    _store_heads(o_ref, normed, head0)


def _forgetting_attention(proj, v_t, cum_cols, batch, seq, n_heads):
    t = proj.shape[0]
    width = n_heads * HEAD_DIM
    nb = width // LANES
    tq = min(ATT_T, seq)
    sub = tq // 2
    nq = seq // tq
    nk = seq // sub
    assert v_t.shape[2] == sub
    n_step = ATT_COLS * HEADS_PER_BLOCK
    cols = ATT_COLS * LANES
    ng = nb // ATT_COLS
    score_buf = pltpu.VMEM((n_step, sub, tq), F32)
    max_buf = pltpu.VMEM((n_step, 1, tq), F32)
    q_spec = pl.BlockSpec((tq, cols), lambda b, hp, qi: (b * nq + qi, hp))
    k_spec = pl.BlockSpec((seq, cols), lambda b, hp, qi: (b, ng + hp))
    v_spec = pl.BlockSpec((nk, cols, sub), lambda b, hp, qi: (b, hp, 0))
    cq_spec = pl.BlockSpec((tq, LANES), lambda b, hp, qi: (b * nq + qi, 0))
    ck_spec = pl.BlockSpec((seq, LANES), lambda b, hp, qi: (b, 0))
    return pl.pallas_call(
        functools.partial(_fox_kernel, tq=tq, sub=sub, seq=seq),
        grid=(batch, ng, nq),
        in_specs=[q_spec, k_spec, v_spec, cq_spec, ck_spec],
        out_specs=pl.BlockSpec((tq, cols), lambda b, hp, qi: (b * nq + qi, hp)),
        out_shape=jax.ShapeDtypeStruct((t, width), BF16),
        scratch_shapes=[pltpu.VMEM((n_step, seq, LANES), BF16),
                        pltpu.VMEM((n_step, nk, LANES, sub), BF16),
                        score_buf, score_buf, max_buf, max_buf],
        compiler_params=_cparams(("parallel", "parallel", "arbitrary")),
        name="forgetting_attention",
    )(proj, proj, v_t, cum_cols, cum_cols)


def _swiglu_partial(x, wg_ref, wu_ref, wd_ref):
    g = _dot(x, wg_ref[...])
    u = _dot(x, wu_ref[...])
    a = g * (1.0 / (1.0 + jnp.exp(-g))) * u
    return _dot(a.astype(BF16), wd_ref[...])


def _ffn_dense_kernel(x_ref, wg_ref, wu_ref, wd_ref, h_ref, gain_ref, ho_ref, xn_ref, acc_ref):
    f = pl.program_id(1)

    @pl.when(f == 0)
    def _():
        acc_ref[...] = h_ref[...]

    acc_ref[...] += _swiglu_partial(x_ref[...], wg_ref, wu_ref, wd_ref)

    @pl.when(f == pl.num_programs(1) - 1)
    def _():
        h_new = acc_ref[...]
        ho_ref[...] = h_new
        xn_ref[...] = _rms(h_new, gain_ref[...]).astype(xn_ref.dtype)


def _ffn_dense(xn, wg, wu, wd, h, next_gain):
    t, d = h.shape
    dff = wg.shape[1]
    tm = min(FFN_TM, t)
    tf = dff // FFN_NF
    row = pl.BlockSpec((tm, d), lambda i, f: (i, 0))
    return pl.pallas_call(
        _ffn_dense_kernel,
        grid=(t // tm, FFN_NF),
        in_specs=[row,
                  pl.BlockSpec((d, tf), lambda i, f: (0, f)),
                  pl.BlockSpec((d, tf), lambda i, f: (0, f)),
                  pl.BlockSpec((tf, d), lambda i, f: (f, 0)),
                  row,
                  pl.BlockSpec((1, d), lambda i, f: (0, 0))],
        out_specs=[row, row],
        out_shape=[jax.ShapeDtypeStruct((t, d), F32), jax.ShapeDtypeStruct((t, d), BF16)],
        scratch_shapes=[pltpu.VMEM((tm, d), F32)],
        compiler_params=_cparams(("parallel", "arbitrary")),
        name="ffn_dense",
    )(xn, wg, wu, wd, h, next_gain.reshape(1, d))


def _ffn_grouped_kernel(te_ref, nu_ref, x_ref, wg_ref, wu_ref, wd_ref, o_ref, acc_ref):
    i = pl.program_id(0)
    f = pl.program_id(1)
    used = i < nu_ref[0]

    @pl.when(jnp.logical_and(used, f == 0))
    def _():
        acc_ref[...] = jnp.zeros_like(acc_ref)

    @pl.when(used)
    def _():
        acc_ref[...] += _swiglu_partial(x_ref[...].astype(BF16), wg_ref, wu_ref, wd_ref)

    @pl.when(f == pl.num_programs(1) - 1)
    def _():
        @pl.when(used)
        def _():
            o_ref[...] = acc_ref[...]

        @pl.when(jnp.logical_not(used))
        def _():
            o_ref[...] = jnp.zeros_like(o_ref)


def _ffn_grouped(x_sorted, wg, wu, wd, tile_expert, n_used):
    r, d = x_sorted.shape
    dff = wg.shape[2]
    tm = FFN_TM
    tf = dff // FFN_NF
    grid_spec = pltpu.PrefetchScalarGridSpec(
        num_scalar_prefetch=2,
        grid=(r // tm, FFN_NF),
        in_specs=[pl.BlockSpec((tm, d), lambda i, f, te, nu: (i, 0)),
                  pl.BlockSpec((None, d, tf), lambda i, f, te, nu: (te[i], 0, f)),
                  pl.BlockSpec((None, d, tf), lambda i, f, te, nu: (te[i], 0, f)),
                  pl.BlockSpec((None, tf, d), lambda i, f, te, nu: (te[i], f, 0))],
        out_specs=pl.BlockSpec((tm, d), lambda i, f, te, nu: (i, 0)),
        scratch_shapes=[pltpu.VMEM((tm, d), F32)],
    )
    return pl.pallas_call(
        _ffn_grouped_kernel,
        grid_spec=grid_spec,
        out_shape=jax.ShapeDtypeStruct((r, d), F32),
        compiler_params=_cparams(("arbitrary", "arbitrary")),
        name="ffn_grouped",
    )(tile_expert, n_used, x_sorted, wg, wu, wd)


def _router_kernel(h_ref, gain_ref, wr_ref, xn_ref, route_ref):
    xn = _rms(h_ref[...], gain_ref[...])
    xn_ref[...] = xn
    logits = jnp.dot(xn, wr_ref[...], precision=lax.Precision.HIGHEST, preferred_element_type=F32)
    lane = lax.broadcasted_iota(jnp.int32, logits.shape, 1)
    lane_f = lane.astype(F32)
    lg = jnp.where(lane < N_EXPERTS, logits, -jnp.inf)
    v1 = jnp.max(lg, axis=1, keepdims=True)
    i1 = jnp.min(jnp.where(lg == v1, lane_f, float(LANES)), axis=1, keepdims=True)
    lg2 = jnp.where(lane_f == i1, -jnp.inf, lg)
    v2 = jnp.max(lg2, axis=1, keepdims=True)
    i2 = jnp.min(jnp.where(lg2 == v2, lane_f, float(LANES)), axis=1, keepdims=True)
    e2 = jnp.exp(v2 - v1)
    g1 = 1.0 / (1.0 + e2)
    g2 = e2 / (1.0 + e2)
    route_ref[...] = jnp.where(lane == 0, i1, jnp.where(lane == 1, i2,
                               jnp.where(lane == 2, g1, jnp.where(lane == 3, g2, 0.0))))


def _router(h, gain, w_router):
    t, d = h.shape
    tm = min(ROW_TM, t)
    wr = jnp.zeros((d, LANES), F32).at[:, :N_EXPERTS].set(w_router)
    row = pl.BlockSpec((tm, d), lambda i: (i, 0))
    return pl.pallas_call(
        _router_kernel,
        grid=(t // tm,),
        in_specs=[row, pl.BlockSpec((1, d), lambda i: (0, 0)),
                  pl.BlockSpec((d, LANES), lambda i: (0, 0))],
        out_specs=[row, pl.BlockSpec((tm, LANES), lambda i: (i, 0))],
        out_shape=[jax.ShapeDtypeStruct((t, d), F32), jax.ShapeDtypeStruct((t, LANES), F32)],
        compiler_params=_cparams(("parallel",)),
        name="router",
    )(h, gain.reshape(1, d), wr)


def _scatter_kernel(dest_ref, x_ref, init_hbm, o_hbm, sem, *, tm):
    del init_hbm

    def row_copy(r, kk):
        return pltpu.make_async_copy(x_ref.at[pl.ds(r, 1), :],
                                     o_hbm.at[pl.ds(dest_ref[0, 0, TOP_K * r + kk], 1), :], sem)

    def issue(r, carry):
        for kk in range(TOP_K):
            row_copy(r, kk).start()
        return carry

    lax.fori_loop(0, tm, issue, 0)

    def wait(r, carry):
        for kk in range(TOP_K):
            row_copy(r, kk).wait()
        return carry

    lax.fori_loop(0, tm, wait, 0)


def _scatter_rows(xn, dest, n_rows):
    t, d = xn.shape
    tm = min(GATHER_TM, t)
    nt = t // tm
    dest3 = dest.reshape(nt, 1, TOP_K * tm)
    init = jnp.zeros((n_rows, d), F32)
    return pl.pallas_call(
        functools.partial(_scatter_kernel, tm=tm),
        grid=(nt,),
        in_specs=[pl.BlockSpec((1, 1, TOP_K * tm), lambda i: (i, 0, 0), memory_space=pltpu.SMEM),
                  pl.BlockSpec((tm, d), lambda i: (i, 0)),
                  pl.BlockSpec(memory_space=pl.ANY)],
        out_specs=pl.BlockSpec(memory_space=pl.ANY),
        out_shape=jax.ShapeDtypeStruct((n_rows, d), F32),
        scratch_shapes=[pltpu.SemaphoreType.DMA(())],
        input_output_aliases={2: 0},
        compiler_params=_cparams(("arbitrary",)),
        name="moe_scatter",
    )(dest3, xn, init)


def _combine_kernel(dcur_ref, dnext_ref, y_hbm, h_ref, route_ref, gain_ref, o_ref, buf, sems, *, tm):
    i = pl.program_id(0)
    n = pl.num_programs(0)
    slot = i % 2

    def row_copy(d_ref, s, r, kk):
        return pltpu.make_async_copy(y_hbm.at[pl.ds(d_ref[0, 0, TOP_K * r + kk], 1), :],
                                     buf.at[s, kk, pl.ds(r, 1), :], sems.at[s])

    def issue(d_ref, s):
        def body(r, carry):
            for kk in range(TOP_K):
                row_copy(d_ref, s, r, kk).start()
            return carry
        lax.fori_loop(0, tm, body, 0)

    @pl.when(i == 0)
    def _():
        issue(dcur_ref, slot)

    @pl.when(i + 1 < n)
    def _():
        issue(dnext_ref, 1 - slot)

    def wait(r, carry):
        for kk in range(TOP_K):
            row_copy(dcur_ref, slot, r, kk).wait()
        return carry
    lax.fori_loop(0, tm, wait, 0)

    route = route_ref[...]
    lane = lax.broadcasted_iota(jnp.int32, route.shape, 1)
    g1 = jnp.sum(jnp.where(lane == 2, route, 0.0), axis=1, keepdims=True)
    g2 = jnp.sum(jnp.where(lane == 3, route, 0.0), axis=1, keepdims=True)
    out = h_ref[...] + (g1 * buf[slot, 0] + g2 * buf[slot, 1])
    o_ref[...] = _rms(out, gain_ref[...])


def _combine(y_sorted, dest, h, route, final_gain):
    t, d = h.shape
    tm = min(GATHER_TM, t)
    nt = t // tm
    dest3 = dest.reshape(nt, 1, TOP_K * tm)
    row = pl.BlockSpec((tm, d), lambda i: (i, 0))
    smem = functools.partial(pl.BlockSpec, (1, 1, TOP_K * tm), memory_space=pltpu.SMEM)
    return pl.pallas_call(
        functools.partial(_combine_kernel, tm=tm),
        grid=(nt,),
        in_specs=[smem(index_map=lambda i: (i, 0, 0)),
                  smem(index_map=lambda i: (jnp.minimum(i + 1, nt - 1), 0, 0)),
                  pl.BlockSpec(memory_space=pl.ANY),
                  row,
                  pl.BlockSpec((tm, LANES), lambda i: (i, 0)),
                  pl.BlockSpec((1, d), lambda i: (0, 0))],
        out_specs=row,
        out_shape=jax.ShapeDtypeStruct((t, d), F32),
        scratch_shapes=[pltpu.VMEM((2, TOP_K, tm, d), F32), pltpu.SemaphoreType.DMA((2,))],
        compiler_params=_cparams(("arbitrary",)),
        name="moe_combine",
    )(dest3, dest3, y_sorted, h, route, final_gain.reshape(1, d))


def _routing_plan(expert_idx, tm):
    t = expert_idx.shape[0]
    flat = expert_idx.reshape(-1)
    onehot = (flat[:, None] == jnp.arange(N_EXPERTS, dtype=jnp.int32)[None, :]).astype(jnp.int32)
    running = jnp.cumsum(onehot, axis=0)
    rank = jnp.sum((running - onehot) * onehot, axis=1)
    counts = running[-1]
    tiles = (counts + tm - 1) // tm
    tile_end = jnp.cumsum(tiles)
    group_start = (tile_end - tiles) * tm
    dest = jnp.sum(onehot * group_start[None, :], axis=1) + rank
    n_tiles = (t * TOP_K) // tm + N_EXPERTS
    n_used = tile_end[-1]
    tile_id = jnp.minimum(jnp.arange(n_tiles, dtype=jnp.int32), n_used - 1)
    tile_expert = jnp.sum((tile_id[:, None] >= tile_end[None, :]).astype(jnp.int32), axis=1)
    return dest.reshape(t, TOP_K).astype(jnp.int32), tile_expert.astype(jnp.int32), \
        n_used.reshape(1).astype(jnp.int32), n_tiles * tm


def kernel(x, attn_norm_even, w_in_even, ret_norm_even, w_out_even, ffn_norm_even, w_gate_even, w_up_even, w_down_even, attn_norm_odd, w_in_odd, b_forget_odd, w_out_odd, ffn_norm_odd, w_router_odd, w_gate_moe_odd, w_up_moe_odd, w_down_moe_odd, final_norm):
    batch, seq, d = x.shape
    t = batch * seq
    n_ret = d // (2 * HEAD_DIM)
    n_sb = d // (2 * HEAD_DIM)
    n_fox = d // HEAD_DIM
    ret_width = n_ret * HEAD_DIM
    sb_width = n_sb * HEAD_DIM
    fox_width = n_fox * HEAD_DIM
    sub = min(ATT_T, seq) // 2
    h0 = x.reshape(t, d)

    w_in = w_in_even[0]
    n_direct = 4 * ret_width + 2 * sb_width
    proj, xn0 = _norm_mm(h0, attn_norm_even[0], w_in[:, :n_direct].astype(BF16), BF16)
    v_sb_t = _mm_nt(xn0, w_in[:, n_direct:].T.astype(BF16), sub)
    y_ret = _retention(proj, ret_norm_even[0], batch, seq, n_ret)
    y_sb = _stick_breaking(proj, v_sb_t, batch, seq, n_sb, first_group=4)
    w_out = w_out_even[0].astype(BF16)
    h1, xn1 = _outproj([y_ret, y_sb], [w_out[:ret_width], w_out[ret_width:]], h0, ffn_norm_even[0])
    h2, xn2 = _ffn_dense(xn1, w_gate_even[0].astype(BF16), w_up_even[0].astype(BF16),
                         w_down_even[0].astype(BF16), h1, attn_norm_odd[0])

    w_in = w_in_odd[0]
    proj = _mm(xn2, w_in[:, :2 * fox_width].astype(BF16), BF16)
    v_fox_t = _mm_nt(xn2, w_in[:, 2 * fox_width:3 * fox_width].T.astype(BF16), sub)
    w_f = jnp.zeros((d, LANES), BF16).at[:, :n_fox].set(w_in[:, 3 * fox_width:].astype(BF16))
    f_logit = _mm(xn2, w_f, F32)
    b_f = jnp.zeros((1, LANES), F32).at[0, :n_fox].set(b_forget_odd[0])
    cum_cols = _forget_cum(f_logit, b_f, batch, seq)
    y_fox = _forgetting_attention(proj, v_fox_t, cum_cols, batch, seq, n_fox)
    h3 = _outproj([y_fox], [w_out_odd[0].astype(BF16)], h2)

    xn3, route = _router(h3, ffn_norm_odd[0], w_router_odd[0])
    expert_idx = route[:, :TOP_K].astype(jnp.int32)
    dest, tile_expert, n_used, n_rows = _routing_plan(expert_idx, FFN_TM)
    x_sorted = _scatter_rows(xn3, dest, n_rows)
    y_sorted = _ffn_grouped(x_sorted, w_gate_moe_odd[0].astype(BF16), w_up_moe_odd[0].astype(BF16),
                            w_down_moe_odd[0].astype(BF16), tile_expert, n_used)
    out = _combine(y_sorted, dest, h3, route, final_norm)
    return out.reshape(batch, seq, d)
```

```python
import functools

import jax
import jax.numpy as jnp
import numpy as np
from jax import lax
from jax.experimental import pallas as pl
from jax.experimental.pallas import tpu as pltpu

F32 = jnp.float32
BF16 = jnp.bfloat16

LANES = 128
HEAD_DIM = 64
HEADS_PER_BLOCK = LANES // HEAD_DIM
N_EXPERTS = 8
TOP_K = 2
ROPE_BASE = 10000.0
NORM_EPS = 1e-6
GROUP_NORM_EPS = 1e-5
LOG2E = 1.4426950408889634
RET_CHUNK = 256
VMEM_LIMIT = 56 * 1024 * 1024

MM_TM = 1024
MM_TN = 1024
ROW_TM = 512
ATT_T = 512
ATT_COLS = 2
RET_TS = 512
FFN_TM = 256
FFN_NF = 1
GATHER_TM = 256
DMA_UNROLL = 8
N_BIAS_PIECES = 3
FOX_VALUE_ROWS = HEAD_DIM + 16


def _cparams(sem, vmem=VMEM_LIMIT):
    return pltpu.CompilerParams(dimension_semantics=sem, vmem_limit_bytes=vmem)


def _rms(xf, gain_row):
    ms = jnp.mean(xf * xf, axis=-1, keepdims=True)
    return xf * lax.rsqrt(ms + NORM_EPS) * gain_row


def _dot(a, b):
    return jnp.dot(a, b, preferred_element_type=F32)


def _dot_nt(a, b):
    return lax.dot_general(a, b, (((1,), (1,)), ((), ())), preferred_element_type=F32)


def _dot_tn(a, b):
    return lax.dot_general(a, b, (((0,), (0,)), ((), ())), preferred_element_type=F32)


def _norm_mm_kernel(x_ref, g_ref, w_ref, o_ref, xn_ref):
    @pl.when(pl.program_id(1) == 0)
    def _():
        xn_ref[...] = _rms(x_ref[...], g_ref[...]).astype(BF16)

    o_ref[...] = _dot(xn_ref[...], w_ref[...]).astype(o_ref.dtype)


def _norm_mm(x, gain, w, out_dtype):
    t, k = x.shape
    n = w.shape[1]
    tm, tn = min(MM_TM, t), min(MM_TN, n)
    return pl.pallas_call(
        _norm_mm_kernel,
        grid=(t // tm, n // tn),
        in_specs=[pl.BlockSpec((tm, k), lambda i, j: (i, 0)),
                  pl.BlockSpec((1, k), lambda i, j: (0, 0)),
                  pl.BlockSpec((k, tn), lambda i, j: (0, j))],
        out_specs=[pl.BlockSpec((tm, tn), lambda i, j: (i, j)),
                   pl.BlockSpec((tm, k), lambda i, j: (i, 0))],
        out_shape=[jax.ShapeDtypeStruct((t, n), out_dtype), jax.ShapeDtypeStruct((t, k), BF16)],
        compiler_params=_cparams(("parallel", "arbitrary")),
        name="norm_mm",
    )(x, gain.reshape(1, k), w)


def _mm_kernel(x_ref, w_ref, o_ref):
    o_ref[...] = _dot(x_ref[...], w_ref[...]).astype(o_ref.dtype)


def _mm(x, w, out_dtype):
    t, k = x.shape
    n = w.shape[1]
    tm, tn = min(MM_TM, t), min(MM_TN, n)
    return pl.pallas_call(
        _mm_kernel,
        grid=(t // tm, n // tn),
        in_specs=[pl.BlockSpec((tm, k), lambda i, j: (i, 0)),
                  pl.BlockSpec((k, tn), lambda i, j: (0, j))],
        out_specs=pl.BlockSpec((tm, tn), lambda i, j: (i, j)),
        out_shape=jax.ShapeDtypeStruct((t, n), out_dtype),
        compiler_params=_cparams(("parallel", "parallel")),
        name="mm",
    )(x, w)


def _mm_nt_kernel(x_ref, wt_ref, o_ref, *, sub):
    res = _dot_nt(wt_ref[...], x_ref[...]).astype(o_ref.dtype)
    for s in range(o_ref.shape[0]):
        o_ref[s] = res[:, s * sub:(s + 1) * sub]


def _mm_nt(x, wt, sub):
    t, k = x.shape
    n = wt.shape[0]
    tm, tn = min(MM_TM, t), min(MM_TN, n)
    return pl.pallas_call(
        functools.partial(_mm_nt_kernel, sub=sub),
        grid=(t // tm, n // tn),
        in_specs=[pl.BlockSpec((tm, k), lambda i, j: (i, 0)),
                  pl.BlockSpec((tn, k), lambda i, j: (j, 0))],
        out_specs=pl.BlockSpec((tm // sub, tn, sub), lambda i, j: (i, j, 0)),
        out_shape=jax.ShapeDtypeStruct((t // sub, n, sub), BF16),
        compiler_params=_cparams(("parallel", "parallel")),
        name="mm_nt",
    )(x, wt)


def _outproj_kernel(*refs, n_in, with_norm):
    ys = refs[:n_in]
    ws = refs[n_in:2 * n_in]
    h_ref = refs[2 * n_in]
    pos = 2 * n_in + 1
    acc = h_ref[...]
    for y_ref, w_ref in zip(ys, ws):
        acc = acc + _dot(y_ref[...], w_ref[...])
    if with_norm:
        g_ref, ho_ref, xn_ref = refs[pos], refs[pos + 1], refs[pos + 2]
        ho_ref[...] = acc
        xn_ref[...] = _rms(acc, g_ref[...]).astype(xn_ref.dtype)
    else:
        refs[pos][...] = acc


def _outproj(ys, ws, h, gain=None):
    t, d = h.shape
    tm = min(ROW_TM, t)
    n_in = len(ys)
    with_norm = gain is not None
    in_specs = [pl.BlockSpec((tm, y.shape[1]), lambda i: (i, 0)) for y in ys]
    in_specs += [pl.BlockSpec(w.shape, lambda i: (0, 0)) for w in ws]
    in_specs += [pl.BlockSpec((tm, d), lambda i: (i, 0))]
    args = list(ys) + list(ws) + [h]
    row_spec = pl.BlockSpec((tm, d), lambda i: (i, 0))
    if with_norm:
        in_specs += [pl.BlockSpec((1, d), lambda i: (0, 0))]
        args += [gain.reshape(1, d)]
        out_specs = [row_spec, row_spec]
        out_shape = [jax.ShapeDtypeStruct((t, d), F32), jax.ShapeDtypeStruct((t, d), BF16)]
    else:
        out_specs = row_spec
        out_shape = jax.ShapeDtypeStruct((t, d), F32)
    return pl.pallas_call(
        functools.partial(_outproj_kernel, n_in=n_in, with_norm=with_norm),
        grid=(t // tm,),
        in_specs=in_specs,
        out_specs=out_specs,
        out_shape=out_shape,
        compiler_params=_cparams(("parallel",)),
        name="outproj",
    )(*args)


def _retention_tables(seq, n_heads):
    half = HEAD_DIM // 2
    lane = np.arange(LANES)
    inv_freq = ROPE_BASE ** (-jnp.arange(half, dtype=F32) / half)
    ang = jnp.arange(seq, dtype=F32)[:, None] * inv_freq[None, :]
    cos, sin = jnp.cos(ang), jnp.sin(ang)
    cos_t = jnp.tile(cos, (1, LANES // half))
    sign = np.where((lane % HEAD_DIM) < half, -1.0, 1.0).astype(np.float32)
    sin_t = jnp.tile(sin, (1, LANES // half)) * sign[None, :]
    c = RET_CHUNK
    log_gamma = jnp.log(1.0 - 2.0 ** (-5.0 - jnp.arange(n_heads, dtype=F32)))
    pos = jnp.arange(c, dtype=F32)
    diff = pos[:, None] - pos[None, :]
    intra = jnp.where(diff >= 0.0,
                      jnp.exp(log_gamma[:, None, None] * jnp.maximum(diff, 0.0)), 0.0)
    intra = intra.reshape(n_heads // 2, 2, c, c)
    q_decay = jnp.exp(log_gamma[:, None] * (pos + 1.0))
    k_decay = jnp.exp(log_gamma[:, None] * (c - 1.0 - pos))
    chunk_decay = jnp.exp(log_gamma * c)

    def per_lane(tab):
        tab = tab.reshape(n_heads // 2, 2, c)
        return jnp.repeat(tab.transpose(0, 2, 1), HEAD_DIM, axis=2)

    head_of = lane // HEAD_DIM
    same = (head_of[:, None] == head_of[None, :]).astype(np.float32)
    cd = chunk_decay.reshape(n_heads // 2, 2)
    cd_rows = jnp.repeat(cd, HEAD_DIM, axis=1)
    state_decay = cd_rows[:, :, None] * same[None]
    return cos_t, sin_t, intra, per_lane(q_decay), per_lane(k_decay), state_decay, jnp.asarray(same)


def _retention_kernel(q_ref, k_ref, v_ref, g_ref, cos_ref, sin_ref, intra_ref, qd_ref, kd_ref,
                      sd_ref, same_ref, rn_ref, o_ref, state_ref, *, ts):
    @pl.when(pl.program_id(2) == 0)
    def _():
        state_ref[...] = jnp.zeros_like(state_ref)

    c = RET_CHUNK
    lane = lax.broadcasted_iota(jnp.int32, (1, LANES), 1)
    first_half = (lane % HEAD_DIM) < (HEAD_DIM // 2)
    head0 = lane < HEAD_DIM

    def rot(t, cos, sin):
        swapped = jnp.where(first_half, pltpu.roll(t, LANES - HEAD_DIM // 2, 1),
                            pltpu.roll(t, HEAD_DIM // 2, 1))
        return t * cos + swapped * sin

    for ci in range(ts // c):
        rows = slice(ci * c, (ci + 1) * c)
        cos, sin = cos_ref[rows, :], sin_ref[rows, :]
        q = rot(q_ref[rows, :].astype(F32), cos, sin)
        k = rot(k_ref[rows, :].astype(F32), cos, sin) * (HEAD_DIM ** -0.5)
        v = v_ref[rows, :]
        kb = k.astype(BF16)
        inner = []
        for hd in range(HEADS_PER_BLOCK):
            hmask = head0 if hd == 0 else jnp.logical_not(head0)
            qh = jnp.where(hmask, q, 0.0).astype(BF16)
            scores = _dot_nt(qh, kb) * intra_ref[hd]
            inner.append(_dot(scores.astype(BF16), v))
        state = state_ref[...]
        cross = _dot((q * qd_ref[...]).astype(BF16), state.astype(BF16))
        y = jnp.where(head0, inner[0], inner[1]) + cross
        kv = _dot_tn((k * kd_ref[...]).astype(BF16), v)
        state_ref[...] = state * sd_ref[...] + kv * same_ref[...]

        s0 = jnp.sum(jnp.where(head0, y, 0.0), axis=1, keepdims=True)
        s1 = jnp.sum(jnp.where(head0, 0.0, y), axis=1, keepdims=True)
        d = y - jnp.where(head0, s0, s1) * (1.0 / HEAD_DIM)
        dd = d * d
        v0 = jnp.sum(jnp.where(head0, dd, 0.0), axis=1, keepdims=True)
        v1 = jnp.sum(jnp.where(head0, 0.0, dd), axis=1, keepdims=True)
        var = jnp.where(head0, v0, v1) * (1.0 / HEAD_DIM)
        g = g_ref[rows, :].astype(F32)
        silu = g * (1.0 / (1.0 + jnp.exp(-g)))
        o_ref[rows, :] = (d * lax.rsqrt(var + GROUP_NORM_EPS) * rn_ref[...] * silu).astype(o_ref.dtype)


def _retention(proj, ret_norm, batch, seq, n_heads):
    t = proj.shape[0]
    width = n_heads * HEAD_DIM
    nb = width // LANES
    ts = min(RET_TS, seq)
    ns = seq // ts
    tabs = _retention_tables(seq, n_heads)
    cos_t, sin_t, intra, qd, kd, sd, same = tabs
    c = RET_CHUNK

    def col(group):
        return pl.BlockSpec((ts, LANES), lambda b, hp, si: (b * ns + si, group * nb + hp))

    in_specs = [col(0), col(1), col(2), col(3),
                pl.BlockSpec((ts, LANES), lambda b, hp, si: (si, 0)),
                pl.BlockSpec((ts, LANES), lambda b, hp, si: (si, 0)),
                pl.BlockSpec((None, 2, c, c), lambda b, hp, si: (hp, 0, 0, 0)),
                pl.BlockSpec((None, c, LANES), lambda b, hp, si: (hp, 0, 0)),
                pl.BlockSpec((None, c, LANES), lambda b, hp, si: (hp, 0, 0)),
                pl.BlockSpec((None, LANES, LANES), lambda b, hp, si: (hp, 0, 0)),
                pl.BlockSpec((LANES, LANES), lambda b, hp, si: (0, 0)),
                pl.BlockSpec((1, LANES), lambda b, hp, si: (0, hp))]
    return pl.pallas_call(
        functools.partial(_retention_kernel, ts=ts),
        grid=(batch, nb, ns),
        in_specs=in_specs,
        out_specs=pl.BlockSpec((ts, LANES), lambda b, hp, si: (b * ns + si, hp)),
        out_shape=jax.ShapeDtypeStruct((t, width), BF16),
        scratch_shapes=[pltpu.VMEM((LANES, LANES), F32)],
        compiler_params=_cparams(("parallel", "parallel", "arbitrary")),
        name="retention",
    )(proj, proj, proj, proj, cos_t, sin_t, intra, qd, kd, sd, same, ret_norm.reshape(1, width))


def _head_masks():
    lane = lax.broadcasted_iota(jnp.int32, (1, LANES), 1)
    head0 = lane < HEAD_DIM
    return lane, [head0, jnp.logical_not(head0)]


def _col_block(x, hd):
    cb = hd // HEADS_PER_BLOCK
    return x[:, cb * LANES:(cb + 1) * LANES]


def _head_rows(x, hd):
    return x[hd * HEAD_DIM:(hd + 1) * HEAD_DIM]


def _store_heads(o_ref, acc_t):
    for cb in range(ATT_COLS):
        pair = jnp.concatenate(acc_t[HEADS_PER_BLOCK * cb:HEADS_PER_BLOCK * (cb + 1)], axis=0)
        o_ref[:, cb * LANES:(cb + 1) * LANES] = pair.T.astype(o_ref.dtype)


def _two_stage_blocks(qi, sub, scores_to, apply_from, carry, upper_diag_first):
    top = 2 * qi + 1
    first, second = ((top, sub), (top - 1, 0)) if upper_diag_first else ((top - 1, 0), (top, sub))
    scores_to(0, *first)
    scores_to(1, *second)
    carry = apply_from(0, first[0], carry)

    def pair(i, carry):
        jb = top - 2 - 2 * i
        scores_to(0, jb, None)
        carry = apply_from(1, jnp.where(i == 0, second[0], jb + 1), carry)
        scores_to(1, jb - 1, None)
        return apply_from(0, jb, carry)

    carry = lax.fori_loop(0, qi, pair, carry)
    return apply_from(1, jnp.where(qi == 0, second[0], 0), carry)


def _sb_kernel(q_ref, k_ref, vt_ref, o_ref, d0_ref, d1_ref, tot0_ref, tot1_ref, *, tq, sub):
    qi = pl.program_id(2)
    d_refs = (d0_ref, d1_ref)
    tot_refs = (tot0_ref, tot1_ref)
    _, hmasks = _head_masks()
    key = lax.broadcasted_iota(jnp.int32, (sub, tq), 0)
    qry = lax.broadcasted_iota(jnp.int32, (sub, tq), 1)
    r = lax.broadcasted_iota(jnp.int32, (sub, sub), 0)
    c = lax.broadcasted_iota(jnp.int32, (sub, sub), 1)
    suffix = jnp.where(c >= r, 1.0, 0.0).astype(BF16)
    qf = q_ref[...].astype(F32) * (HEAD_DIM ** -0.5 * LOG2E)
    heads = range(ATT_COLS * HEADS_PER_BLOCK)
    qh = [jnp.where(hmasks[hd % HEADS_PER_BLOCK], _col_block(qf, hd), 0.0).astype(BF16) for hd in heads]

    sign_bit = jnp.uint32(0x80000000)

    def scores_to(buf, jb, diag_off):
        k = k_ref[jb]
        for hd in heads:
            z = _dot_nt(_col_block(k, hd), qh[hd])
            if diag_off is not None:
                z = jnp.where((key + diag_off) < qry, z, -jnp.inf)
            neg_abs = lax.bitcast_convert_type(lax.bitcast_convert_type(z, jnp.uint32) | sign_bit, F32)
            fail = jnp.maximum(z, 0.0) + jnp.log2(1.0 + jnp.exp2(neg_abs))
            tail = _dot(suffix, fail.astype(BF16))
            d_refs[buf][hd] = z - tail
            tot_refs[buf][hd] = tail[0:1, :]

    def apply_from(buf, jb, carry):
        vt = vt_ref[jb]
        out = []
        for hd in heads:
            later, acc = carry[hd]
            w = jnp.exp2(d_refs[buf][hd] - later)
            out.append((later + tot_refs[buf][hd], acc + _dot(_head_rows(vt, hd), w.astype(BF16))))
        return out

    carry = [(jnp.zeros((1, tq), F32), jnp.zeros((HEAD_DIM, tq), F32)) for _ in heads]
    carry = _two_stage_blocks(qi, sub, scores_to, apply_from, carry, upper_diag_first=True)
    _store_heads(o_ref, [c[1] for c in carry])


def _stick_breaking(proj, v_t, batch, seq, n_heads, first_group):
    t = proj.shape[0]
    width = n_heads * HEAD_DIM
    nb = width // LANES
    tq = min(ATT_T, seq)
    sub = tq // 2
    nq = seq // tq
    nk = seq // sub
    assert v_t.shape[2] == sub
    proj_k = proj.reshape(t // sub, sub, proj.shape[1])
    n_step = ATT_COLS * HEADS_PER_BLOCK
    cols = ATT_COLS * LANES
    ng = nb // ATT_COLS
    score_buf = pltpu.VMEM((n_step, sub, tq), F32)
    total_buf = pltpu.VMEM((n_step, 1, tq), F32)
    q_spec = pl.BlockSpec((tq, cols), lambda b, hp, qi: (b * nq + qi, first_group * ng + hp))
    k_spec = pl.BlockSpec((nk, sub, cols), lambda b, hp, qi: (b, 0, (first_group + 1) * ng + hp))
    v_spec = pl.BlockSpec((nk, cols, sub), lambda b, hp, qi: (b, hp, 0))
    return pl.pallas_call(
        functools.partial(_sb_kernel, tq=tq, sub=sub),
        grid=(batch, ng, nq),
        in_specs=[q_spec, k_spec, v_spec],
        out_specs=pl.BlockSpec((tq, cols), lambda b, hp, qi: (b * nq + qi, hp)),
        out_shape=jax.ShapeDtypeStruct((t, width), BF16),
        scratch_shapes=[score_buf, score_buf, total_buf, total_buf],
        compiler_params=_cparams(("parallel", "parallel", "arbitrary")),
        name="stick_breaking",
    )(proj, proj_k, v_t)


def _forget_cum_kernel(f_ref, b_ref, col_ref, *, seq):
    x = f_ref[...] + b_ref[...]
    log_f = jnp.minimum(x, 0.0) - jnp.log(1.0 + jnp.exp(-jnp.abs(x)))
    xt = log_f.T
    pos = lax.broadcasted_iota(jnp.int32, xt.shape, 1)
    shift = 1
    while shift < seq:
        xt = xt + jnp.where(pos >= shift, pltpu.roll(xt, shift, 1), 0.0)
        shift *= 2
    col_ref[...] = xt.T


def _forget_cum(f_logit, b_forget, batch, seq):
    t = f_logit.shape[0]
    return pl.pallas_call(
        functools.partial(_forget_cum_kernel, seq=seq),
        grid=(batch,),
        in_specs=[pl.BlockSpec((seq, LANES), lambda b: (b, 0)),
                  pl.BlockSpec((1, LANES), lambda b: (0, 0))],
        out_specs=pl.BlockSpec((seq, LANES), lambda b: (b, 0)),
        out_shape=jax.ShapeDtypeStruct((t, LANES), F32),
        compiler_params=_cparams(("parallel",)),
        name="forget_cum",
    )(f_logit, b_forget)


def _bias_lanes(hd):
    return HEAD_DIM * (1 - hd)


def _with_bias_lanes(x, bias, lane, hmask, hd, is_query):
    pieces = []
    rest = bias
    for _ in range(N_BIAS_PIECES):
        p = rest.astype(BF16).astype(F32)
        pieces.append(p)
        rest = rest - p
    base = _bias_lanes(hd)
    out = jnp.where(hmask, x, 0.0)
    for i, p in enumerate(pieces):
        piece_lane = base + i if is_query else base + N_BIAS_PIECES + i
        one_lane = base + N_BIAS_PIECES + i if is_query else base + i
        out = jnp.where(lane == piece_lane, p if is_query else -p, out)
        out = jnp.where(lane == one_lane, 1.0, out)
    return out.astype(BF16)


def _fox_kernel(q_ref, k_ref, vt_ref, cq_ref, ck_ref, o_ref, kp_ref, vp_ref, s0_ref, s1_ref,
                max0_ref, max1_ref, *, tq, sub, seq):
    hp = pl.program_id(1)
    qi = pl.program_id(2)
    s_refs = (s0_ref, s1_ref)
    max_refs = (max0_ref, max1_ref)
    nk = seq // sub
    lane, hmasks = _head_masks()
    heads = range(ATT_COLS * HEADS_PER_BLOCK)

    def head_bias(cum, hd):
        sel = lane == ATT_COLS * HEADS_PER_BLOCK * hp + hd
        return jnp.sum(jnp.where(sel, cum, 0.0), axis=1, keepdims=True) * LOG2E

    @pl.when(qi == 0)
    def _():
        kf = k_ref[...].astype(F32)
        ck = ck_ref[...]
        extra = lax.broadcasted_iota(jnp.int32, (FOX_VALUE_ROWS - HEAD_DIM, sub), 0)
        ones = jnp.where(extra == 0, 1.0, 0.0).astype(BF16)
        for hd in heads:
            side = hd % HEADS_PER_BLOCK
            kp_ref[hd] = _with_bias_lanes(_col_block(kf, hd), head_bias(ck, hd), lane, hmasks[side], side, False)
            for jb in range(nk):
                vp_ref[hd, jb, 0:HEAD_DIM] = _head_rows(vt_ref[jb], hd)
                vp_ref[hd, jb, HEAD_DIM:FOX_VALUE_ROWS] = ones

    qf = q_ref[...].astype(F32) * (HEAD_DIM ** -0.5 * LOG2E)
    cq = cq_ref[...]
    qh = [_with_bias_lanes(_col_block(qf, hd), head_bias(cq, hd), lane, hmasks[hd % HEADS_PER_BLOCK],
                           hd % HEADS_PER_BLOCK, True) for hd in heads]
    key = lax.broadcasted_iota(jnp.int32, (sub, tq), 0)
    qry = lax.broadcasted_iota(jnp.int32, (sub, tq), 1)

    def scores_to(buf, jb, diag_off):
        ks = pl.multiple_of(jb * sub, sub)
        for hd in heads:
            s = _dot_nt(kp_ref[hd, pl.ds(ks, sub), :], qh[hd])
            if diag_off is not None:
                s = jnp.where((key + diag_off) <= qry, s, -jnp.inf)
            s_refs[buf][hd] = s
            max_refs[buf][hd] = jnp.max(s, axis=0, keepdims=True)

    def apply_from(buf, jb, carry):
        out = []
        for hd in heads:
            m, acc = carry[hd]
            m_new = jnp.maximum(m, max_refs[buf][hd])
            p = jnp.exp2(s_refs[buf][hd] - m_new).astype(BF16)
            out.append((m_new, jnp.exp2(m - m_new) * acc + _dot(vp_ref[hd, jb], p)))
        return out

    carry = [(jnp.full((1, tq), -jnp.inf, F32), jnp.zeros((FOX_VALUE_ROWS, tq), F32)) for _ in heads]
    carry = _two_stage_blocks(qi, sub, scores_to, apply_from, carry, upper_diag_first=False)
    normed = []
    for hd in heads:
        acc = carry[hd][1]
        normed.append(acc[0:HEAD_DIM] * (1.0 / acc[HEAD_DIM:HEAD_DIM + 1, :]))
    _store_heads(o_ref, normed)


def _forgetting_attention(proj, v_t, cum_cols, batch, seq, n_heads):
    t = proj.shape[0]
    width = n_heads * HEAD_DIM
    nb = width // LANES
    tq = min(ATT_T, seq)
    sub = tq // 2
    nq = seq // tq
    nk = seq // sub
    assert v_t.shape[2] == sub
    n_step = ATT_COLS * HEADS_PER_BLOCK
    cols = ATT_COLS * LANES
    ng = nb // ATT_COLS
    score_buf = pltpu.VMEM((n_step, sub, tq), F32)
    max_buf = pltpu.VMEM((n_step, 1, tq), F32)
    q_spec = pl.BlockSpec((tq, cols), lambda b, hp, qi: (b * nq + qi, hp))
    k_spec = pl.BlockSpec((seq, cols), lambda b, hp, qi: (b, ng + hp))
    v_spec = pl.BlockSpec((nk, cols, sub), lambda b, hp, qi: (b, hp, 0))
    cq_spec = pl.BlockSpec((tq, LANES), lambda b, hp, qi: (b * nq + qi, 0))
    ck_spec = pl.BlockSpec((seq, LANES), lambda b, hp, qi: (b, 0))
    return pl.pallas_call(
        functools.partial(_fox_kernel, tq=tq, sub=sub, seq=seq),
        grid=(batch, ng, nq),
        in_specs=[q_spec, k_spec, v_spec, cq_spec, ck_spec],
        out_specs=pl.BlockSpec((tq, cols), lambda b, hp, qi: (b * nq + qi, hp)),
        out_shape=jax.ShapeDtypeStruct((t, width), BF16),
        scratch_shapes=[pltpu.VMEM((n_step, seq, LANES), BF16),
                        pltpu.VMEM((n_step, nk, FOX_VALUE_ROWS, sub), BF16),
                        score_buf, score_buf, max_buf, max_buf],
        compiler_params=_cparams(("parallel", "parallel", "arbitrary")),
        name="forgetting_attention",
    )(proj, proj, v_t, cum_cols, cum_cols)


def _swiglu_partial(x, wg_ref, wu_ref, wd_ref):
    g = _dot(x, wg_ref[...])
    u = _dot(x, wu_ref[...])
    a = g * (1.0 / (1.0 + jnp.exp(-g))) * u
    return _dot(a.astype(BF16), wd_ref[...])


def _ffn_dense_kernel(x_ref, wg_ref, wu_ref, wd_ref, h_ref, gain_ref, ho_ref, xn_ref, acc_ref):
    f = pl.program_id(1)

    @pl.when(f == 0)
    def _():
        acc_ref[...] = h_ref[...]

    acc_ref[...] += _swiglu_partial(x_ref[...], wg_ref, wu_ref, wd_ref)

    @pl.when(f == pl.num_programs(1) - 1)
    def _():
        h_new = acc_ref[...]
        ho_ref[...] = h_new
        xn_ref[...] = _rms(h_new, gain_ref[...]).astype(xn_ref.dtype)


def _ffn_dense(xn, wg, wu, wd, h, next_gain):
    t, d = h.shape
    dff = wg.shape[1]
    tm = min(FFN_TM, t)
    tf = dff // FFN_NF
    row = pl.BlockSpec((tm, d), lambda i, f: (i, 0))
    return pl.pallas_call(
        _ffn_dense_kernel,
        grid=(t // tm, FFN_NF),
        in_specs=[row,
                  pl.BlockSpec((d, tf), lambda i, f: (0, f)),
                  pl.BlockSpec((d, tf), lambda i, f: (0, f)),
                  pl.BlockSpec((tf, d), lambda i, f: (f, 0)),
                  row,
                  pl.BlockSpec((1, d), lambda i, f: (0, 0))],
        out_specs=[row, row],
        out_shape=[jax.ShapeDtypeStruct((t, d), F32), jax.ShapeDtypeStruct((t, d), BF16)],
        scratch_shapes=[pltpu.VMEM((tm, d), F32)],
        compiler_params=_cparams(("parallel", "arbitrary")),
        name="ffn_dense",
    )(xn, wg, wu, wd, h, next_gain.reshape(1, d))


def _ffn_grouped_kernel(te_ref, nu_ref, x_ref, wg_ref, wu_ref, wd_ref, o_ref, acc_ref):
    i = pl.program_id(0)
    f = pl.program_id(1)
    used = i < nu_ref[0]

    @pl.when(jnp.logical_and(used, f == 0))
    def _():
        acc_ref[...] = jnp.zeros_like(acc_ref)

    @pl.when(used)
    def _():
        acc_ref[...] += _swiglu_partial(x_ref[...].astype(BF16), wg_ref, wu_ref, wd_ref)

    @pl.when(f == pl.num_programs(1) - 1)
    def _():
        @pl.when(used)
        def _():
            o_ref[...] = acc_ref[...]

        @pl.when(jnp.logical_not(used))
        def _():
            o_ref[...] = jnp.zeros_like(o_ref)


def _ffn_grouped(x_sorted, wg, wu, wd, tile_expert, n_used):
    r, d = x_sorted.shape
    dff = wg.shape[2]
    tm = FFN_TM
    tf = dff // FFN_NF
    grid_spec = pltpu.PrefetchScalarGridSpec(
        num_scalar_prefetch=2,
        grid=(r // tm, FFN_NF),
        in_specs=[pl.BlockSpec((tm, d), lambda i, f, te, nu: (i, 0)),
                  pl.BlockSpec((None, d, tf), lambda i, f, te, nu: (te[i], 0, f)),
                  pl.BlockSpec((None, d, tf), lambda i, f, te, nu: (te[i], 0, f)),
                  pl.BlockSpec((None, tf, d), lambda i, f, te, nu: (te[i], f, 0))],
        out_specs=pl.BlockSpec((tm, d), lambda i, f, te, nu: (i, 0)),
        scratch_shapes=[pltpu.VMEM((tm, d), F32)],
    )
    return pl.pallas_call(
        _ffn_grouped_kernel,
        grid_spec=grid_spec,
        out_shape=jax.ShapeDtypeStruct((r, d), F32),
        compiler_params=_cparams(("arbitrary", "arbitrary")),
        name="ffn_grouped",
    )(tile_expert, n_used, x_sorted, wg, wu, wd)


def _router_kernel(h_ref, gain_ref, wr_ref, xn_ref, route_ref):
    xn = _rms(h_ref[...], gain_ref[...])
    xn_ref[...] = xn
    logits = jnp.dot(xn, wr_ref[...], precision=lax.Precision.HIGHEST, preferred_element_type=F32)
    lane = lax.broadcasted_iota(jnp.int32, logits.shape, 1)
    lane_f = lane.astype(F32)
    lg = jnp.where(lane < N_EXPERTS, logits, -jnp.inf)
    v1 = jnp.max(lg, axis=1, keepdims=True)
    i1 = jnp.min(jnp.where(lg == v1, lane_f, float(LANES)), axis=1, keepdims=True)
    lg2 = jnp.where(lane_f == i1, -jnp.inf, lg)
    v2 = jnp.max(lg2, axis=1, keepdims=True)
    i2 = jnp.min(jnp.where(lg2 == v2, lane_f, float(LANES)), axis=1, keepdims=True)
    e2 = jnp.exp(v2 - v1)
    g1 = 1.0 / (1.0 + e2)
    g2 = e2 / (1.0 + e2)
    route_ref[...] = jnp.where(lane == 0, i1, jnp.where(lane == 1, i2,
                               jnp.where(lane == 2, g1, jnp.where(lane == 3, g2, 0.0))))


def _router(h, gain, w_router):
    t, d = h.shape
    tm = min(ROW_TM, t)
    wr = jnp.zeros((d, LANES), F32).at[:, :N_EXPERTS].set(w_router)
    row = pl.BlockSpec((tm, d), lambda i: (i, 0))
    return pl.pallas_call(
        _router_kernel,
        grid=(t // tm,),
        in_specs=[row, pl.BlockSpec((1, d), lambda i: (0, 0)),
                  pl.BlockSpec((d, LANES), lambda i: (0, 0))],
        out_specs=[row, pl.BlockSpec((tm, LANES), lambda i: (i, 0))],
        out_shape=[jax.ShapeDtypeStruct((t, d), F32), jax.ShapeDtypeStruct((t, LANES), F32)],
        compiler_params=_cparams(("parallel",)),
        name="router",
    )(h, gain.reshape(1, d), wr)


def _scatter_kernel(dest_ref, x_ref, init_hbm, o_hbm, sem, *, tm):
    del init_hbm

    def row_copy(r, kk):
        return pltpu.make_async_copy(x_ref.at[pl.ds(r, 1), :],
                                     o_hbm.at[pl.ds(dest_ref[0, 0, TOP_K * r + kk], 1), :], sem)

    def issue(r, carry):
        for kk in range(TOP_K):
            row_copy(r, kk).start(priority=kk)
        return carry

    lax.fori_loop(0, tm, issue, 0, unroll=DMA_UNROLL)

    for _ in range(TOP_K):
        pltpu.make_async_copy(x_ref, o_hbm.at[pl.ds(0, tm), :], sem).wait()


def _scatter_rows(xn, dest, n_rows):
    t, d = xn.shape
    tm = min(GATHER_TM, t)
    nt = t // tm
    dest3 = dest.reshape(nt, 1, TOP_K * tm)
    init = jnp.zeros((n_rows, d), F32)
    return pl.pallas_call(
        functools.partial(_scatter_kernel, tm=tm),
        grid=(nt,),
        in_specs=[pl.BlockSpec((1, 1, TOP_K * tm), lambda i: (i, 0, 0), memory_space=pltpu.SMEM),
                  pl.BlockSpec((tm, d), lambda i: (i, 0)),
                  pl.BlockSpec(memory_space=pl.ANY)],
        out_specs=pl.BlockSpec(memory_space=pl.ANY),
        out_shape=jax.ShapeDtypeStruct((n_rows, d), F32),
        scratch_shapes=[pltpu.SemaphoreType.DMA(())],
        input_output_aliases={2: 0},
        compiler_params=_cparams(("arbitrary",)),
        name="moe_scatter",
    )(dest3, xn, init)


def _combine_kernel(dcur_ref, dnext_ref, y_hbm, h_ref, route_ref, gain_ref, o_ref, buf, sems, *, tm):
    i = pl.program_id(0)
    n = pl.num_programs(0)
    slot = i % 2

    def row_copy(d_ref, s, r, kk):
        return pltpu.make_async_copy(y_hbm.at[pl.ds(d_ref[0, 0, TOP_K * r + kk], 1), :],
                                     buf.at[s, kk, pl.ds(r, 1), :], sems.at[s])

    def issue(d_ref, s):
        def body(r, carry):
            for kk in range(TOP_K):
                row_copy(d_ref, s, r, kk).start(priority=kk)
            return carry
        lax.fori_loop(0, tm, body, 0, unroll=DMA_UNROLL)

    @pl.when(i == 0)
    def _():
        issue(dcur_ref, slot)

    @pl.when(i + 1 < n)
    def _():
        issue(dnext_ref, 1 - slot)

    for kk in range(TOP_K):
        pltpu.make_async_copy(y_hbm.at[pl.ds(0, tm), :], buf.at[slot, kk], sems.at[slot]).wait()

    route = route_ref[...]
    lane = lax.broadcasted_iota(jnp.int32, route.shape, 1)
    g1 = jnp.sum(jnp.where(lane == 2, route, 0.0), axis=1, keepdims=True)
    g2 = jnp.sum(jnp.where(lane == 3, route, 0.0), axis=1, keepdims=True)
    out = h_ref[...] + (g1 * buf[slot, 0] + g2 * buf[slot, 1])
    o_ref[...] = _rms(out, gain_ref[...])


def _combine(y_sorted, dest, h, route, final_gain):
    t, d = h.shape
    tm = min(GATHER_TM, t)
    nt = t // tm
    dest3 = dest.reshape(nt, 1, TOP_K * tm)
    row = pl.BlockSpec((tm, d), lambda i: (i, 0))
    smem = functools.partial(pl.BlockSpec, (1, 1, TOP_K * tm), memory_space=pltpu.SMEM)
    return pl.pallas_call(
        functools.partial(_combine_kernel, tm=tm),
        grid=(nt,),
        in_specs=[smem(index_map=lambda i: (i, 0, 0)),
                  smem(index_map=lambda i: (jnp.minimum(i + 1, nt - 1), 0, 0)),
                  pl.BlockSpec(memory_space=pl.ANY),
                  row,
                  pl.BlockSpec((tm, LANES), lambda i: (i, 0)),
                  pl.BlockSpec((1, d), lambda i: (0, 0))],
        out_specs=row,
        out_shape=jax.ShapeDtypeStruct((t, d), F32),
        scratch_shapes=[pltpu.VMEM((2, TOP_K, tm, d), F32), pltpu.SemaphoreType.DMA((2,))],
        compiler_params=_cparams(("arbitrary",)),
        name="moe_combine",
    )(dest3, dest3, y_sorted, h, route, final_gain.reshape(1, d))


def _routing_plan(expert_idx, tm):
    t = expert_idx.shape[0]
    flat = expert_idx.reshape(-1)
    onehot = (flat[:, None] == jnp.arange(N_EXPERTS, dtype=jnp.int32)[None, :]).astype(jnp.int32)
    running = jnp.cumsum(onehot, axis=0)
    rank = jnp.sum((running - onehot) * onehot, axis=1)
    counts = running[-1]
    tiles = (counts + tm - 1) // tm
    tile_end = jnp.cumsum(tiles)
    group_start = (tile_end - tiles) * tm
    dest = jnp.sum(onehot * group_start[None, :], axis=1) + rank
    n_tiles = (t * TOP_K) // tm + N_EXPERTS
    n_used = tile_end[-1]
    tile_id = jnp.minimum(jnp.arange(n_tiles, dtype=jnp.int32), n_used - 1)
    tile_expert = jnp.sum((tile_id[:, None] >= tile_end[None, :]).astype(jnp.int32), axis=1)
    return dest.reshape(t, TOP_K).astype(jnp.int32), tile_expert.astype(jnp.int32), \
        n_used.reshape(1).astype(jnp.int32), n_tiles * tm


def kernel(x, attn_norm_even, w_in_even, ret_norm_even, w_out_even, ffn_norm_even, w_gate_even, w_up_even, w_down_even, attn_norm_odd, w_in_odd, b_forget_odd, w_out_odd, ffn_norm_odd, w_router_odd, w_gate_moe_odd, w_up_moe_odd, w_down_moe_odd, final_norm):
    batch, seq, d = x.shape
    t = batch * seq
    n_ret = d // (2 * HEAD_DIM)
    n_sb = d // (2 * HEAD_DIM)
    n_fox = d // HEAD_DIM
    ret_width = n_ret * HEAD_DIM
    sb_width = n_sb * HEAD_DIM
    fox_width = n_fox * HEAD_DIM
    sub = min(ATT_T, seq) // 2
    h0 = x.reshape(t, d)

    w_in = w_in_even[0]
    n_direct = 4 * ret_width + 2 * sb_width
    proj, xn0 = _norm_mm(h0, attn_norm_even[0], w_in[:, :n_direct].astype(BF16), BF16)
    v_sb_t = _mm_nt(xn0, w_in[:, n_direct:].T.astype(BF16), sub)
    y_ret = _retention(proj, ret_norm_even[0], batch, seq, n_ret)
    y_sb = _stick_breaking(proj, v_sb_t, batch, seq, n_sb, first_group=4)
    w_out = w_out_even[0].astype(BF16)
    h1, xn1 = _outproj([y_ret, y_sb], [w_out[:ret_width], w_out[ret_width:]], h0, ffn_norm_even[0])
    h2, xn2 = _ffn_dense(xn1, w_gate_even[0].astype(BF16), w_up_even[0].astype(BF16),
                         w_down_even[0].astype(BF16), h1, attn_norm_odd[0])

    w_in = w_in_odd[0]
    proj = _mm(xn2, w_in[:, :2 * fox_width].astype(BF16), BF16)
    v_fox_t = _mm_nt(xn2, w_in[:, 2 * fox_width:3 * fox_width].T.astype(BF16), sub)
    w_f = jnp.zeros((d, LANES), BF16).at[:, :n_fox].set(w_in[:, 3 * fox_width:].astype(BF16))
    f_logit = _mm(xn2, w_f, F32)
    b_f = jnp.zeros((1, LANES), F32).at[0, :n_fox].set(b_forget_odd[0])
    cum_cols = _forget_cum(f_logit, b_f, batch, seq)
    y_fox = _forgetting_attention(proj, v_fox_t, cum_cols, batch, seq, n_fox)
    h3 = _outproj([y_fox], [w_out_odd[0].astype(BF16)], h2)

    xn3, route = _router(h3, ffn_norm_odd[0], w_router_odd[0])
    expert_idx = route[:, :TOP_K].astype(jnp.int32)
    dest, tile_expert, n_used, n_rows = _routing_plan(expert_idx, FFN_TM)
    x_sorted = _scatter_rows(xn3, dest, n_rows)
    y_sorted = _ffn_grouped(x_sorted, w_gate_moe_odd[0].astype(BF16), w_up_moe_odd[0].astype(BF16),
                            w_down_moe_odd[0].astype(BF16), tile_expert, n_used)
    out = _combine(y_sorted, dest, h3, route, final_norm)
    return out.reshape(batch, seq, d)
```

```python
import functools

import jax
import jax.numpy as jnp
import numpy as np
from jax import lax
from jax.experimental import pallas as pl
from jax.experimental.pallas import tpu as pltpu

F32 = jnp.float32
BF16 = jnp.bfloat16

LANES = 128
HEAD_DIM = 64
HEADS_PER_BLOCK = LANES // HEAD_DIM
N_EXPERTS = 8
TOP_K = 2
ROPE_BASE = 10000.0
NORM_EPS = 1e-6
GROUP_NORM_EPS = 1e-5
LOG2E = 1.4426950408889634
UNDERFLOW_BITS = 160.0
RET_CHUNK = 256
VMEM_LIMIT = 56 * 1024 * 1024

MM_TM = 1024
MM_TN = 1024
ROW_TM = 512
ATT_T = 512
ATT_COLS = 2
RET_TS = 512
FFN_TM = 256
FFN_NF = 1
GATHER_TM = 256
DMA_UNROLL = 8
N_BIAS_PIECES = 3
FOX_VALUE_ROWS = HEAD_DIM + 16


def _cparams(sem, vmem=VMEM_LIMIT):
    return pltpu.CompilerParams(dimension_semantics=sem, vmem_limit_bytes=vmem)


def _rms(xf, gain_row):
    ms = jnp.mean(xf * xf, axis=-1, keepdims=True)
    return xf * lax.rsqrt(ms + NORM_EPS) * gain_row


def _dot(a, b):
    return jnp.dot(a, b, preferred_element_type=F32)


def _dot_nt(a, b):
    return lax.dot_general(a, b, (((1,), (1,)), ((), ())), preferred_element_type=F32)


def _dot_tn(a, b):
    return lax.dot_general(a, b, (((0,), (0,)), ((), ())), preferred_element_type=F32)


def _norm_mm_kernel(x_ref, g_ref, w_ref, o_ref, xn_ref):
    @pl.when(pl.program_id(1) == 0)
    def _():
        xn_ref[...] = _rms(x_ref[...], g_ref[...]).astype(BF16)

    o_ref[...] = _dot(xn_ref[...], w_ref[...]).astype(o_ref.dtype)


def _norm_mm(x, gain, w, out_dtype):
    t, k = x.shape
    n = w.shape[1]
    tm, tn = min(MM_TM, t), min(MM_TN, n)
    return pl.pallas_call(
        _norm_mm_kernel,
        grid=(t // tm, n // tn),
        in_specs=[pl.BlockSpec((tm, k), lambda i, j: (i, 0)),
                  pl.BlockSpec((1, k), lambda i, j: (0, 0)),
                  pl.BlockSpec((k, tn), lambda i, j: (0, j))],
        out_specs=[pl.BlockSpec((tm, tn), lambda i, j: (i, j)),
                   pl.BlockSpec((tm, k), lambda i, j: (i, 0))],
        out_shape=[jax.ShapeDtypeStruct((t, n), out_dtype), jax.ShapeDtypeStruct((t, k), BF16)],
        compiler_params=_cparams(("parallel", "arbitrary")),
        name="norm_mm",
    )(x, gain.reshape(1, k), w)


def _mm_kernel(x_ref, w_ref, o_ref):
    o_ref[...] = _dot(x_ref[...], w_ref[...]).astype(o_ref.dtype)


def _mm(x, w, out_dtype):
    t, k = x.shape
    n = w.shape[1]
    tm, tn = min(MM_TM, t), min(MM_TN, n)
    return pl.pallas_call(
        _mm_kernel,
        grid=(t // tm, n // tn),
        in_specs=[pl.BlockSpec((tm, k), lambda i, j: (i, 0)),
                  pl.BlockSpec((k, tn), lambda i, j: (0, j))],
        out_specs=pl.BlockSpec((tm, tn), lambda i, j: (i, j)),
        out_shape=jax.ShapeDtypeStruct((t, n), out_dtype),
        compiler_params=_cparams(("parallel", "parallel")),
        name="mm",
    )(x, w)


def _mm_nt_kernel(x_ref, wt_ref, o_ref, *, sub):
    res = _dot_nt(wt_ref[...], x_ref[...]).astype(o_ref.dtype)
    for s in range(o_ref.shape[0]):
        o_ref[s] = res[:, s * sub:(s + 1) * sub]


def _mm_nt(x, wt, sub):
    t, k = x.shape
    n = wt.shape[0]
    tm, tn = min(MM_TM, t), min(MM_TN, n)
    return pl.pallas_call(
        functools.partial(_mm_nt_kernel, sub=sub),
        grid=(t // tm, n // tn),
        in_specs=[pl.BlockSpec((tm, k), lambda i, j: (i, 0)),
                  pl.BlockSpec((tn, k), lambda i, j: (j, 0))],
        out_specs=pl.BlockSpec((tm // sub, tn, sub), lambda i, j: (i, j, 0)),
        out_shape=jax.ShapeDtypeStruct((t // sub, n, sub), BF16),
        compiler_params=_cparams(("parallel", "parallel")),
        name="mm_nt",
    )(x, wt)


def _outproj_kernel(*refs, n_in, with_norm):
    ys = refs[:n_in]
    ws = refs[n_in:2 * n_in]
    h_ref = refs[2 * n_in]
    pos = 2 * n_in + 1
    acc = h_ref[...]
    for y_ref, w_ref in zip(ys, ws):
        acc = acc + _dot(y_ref[...], w_ref[...])
    if with_norm:
        g_ref, ho_ref, xn_ref = refs[pos], refs[pos + 1], refs[pos + 2]
        ho_ref[...] = acc
        xn_ref[...] = _rms(acc, g_ref[...]).astype(xn_ref.dtype)
    else:
        refs[pos][...] = acc


def _outproj(ys, ws, h, gain=None):
    t, d = h.shape
    tm = min(ROW_TM, t)
    n_in = len(ys)
    with_norm = gain is not None
    in_specs = [pl.BlockSpec((tm, y.shape[1]), lambda i: (i, 0)) for y in ys]
    in_specs += [pl.BlockSpec(w.shape, lambda i: (0, 0)) for w in ws]
    in_specs += [pl.BlockSpec((tm, d), lambda i: (i, 0))]
    args = list(ys) + list(ws) + [h]
    row_spec = pl.BlockSpec((tm, d), lambda i: (i, 0))
    if with_norm:
        in_specs += [pl.BlockSpec((1, d), lambda i: (0, 0))]
        args += [gain.reshape(1, d)]
        out_specs = [row_spec, row_spec]
        out_shape = [jax.ShapeDtypeStruct((t, d), F32), jax.ShapeDtypeStruct((t, d), BF16)]
    else:
        out_specs = row_spec
        out_shape = jax.ShapeDtypeStruct((t, d), F32)
    return pl.pallas_call(
        functools.partial(_outproj_kernel, n_in=n_in, with_norm=with_norm),
        grid=(t // tm,),
        in_specs=in_specs,
        out_specs=out_specs,
        out_shape=out_shape,
        compiler_params=_cparams(("parallel",)),
        name="outproj",
    )(*args)


def _retention_tables(seq, n_heads):
    half = HEAD_DIM // 2
    lane = np.arange(LANES)
    inv_freq = ROPE_BASE ** (-jnp.arange(half, dtype=F32) / half)
    ang = jnp.arange(seq, dtype=F32)[:, None] * inv_freq[None, :]
    cos, sin = jnp.cos(ang), jnp.sin(ang)
    cos_t = jnp.tile(cos, (1, LANES // half))
    sign = np.where((lane % HEAD_DIM) < half, -1.0, 1.0).astype(np.float32)
    sin_t = jnp.tile(sin, (1, LANES // half)) * sign[None, :]
    c = RET_CHUNK
    log_gamma = jnp.log(1.0 - 2.0 ** (-5.0 - jnp.arange(n_heads, dtype=F32)))
    pos = jnp.arange(c, dtype=F32)
    diff = pos[:, None] - pos[None, :]
    intra = jnp.where(diff >= 0.0,
                      jnp.exp(log_gamma[:, None, None] * jnp.maximum(diff, 0.0)), 0.0)
    intra = intra.reshape(n_heads // 2, 2, c, c)
    q_decay = jnp.exp(log_gamma[:, None] * (pos + 1.0))
    k_decay = jnp.exp(log_gamma[:, None] * (c - 1.0 - pos))
    chunk_decay = jnp.exp(log_gamma * c)

    def per_lane(tab):
        tab = tab.reshape(n_heads // 2, 2, c)
        return jnp.repeat(tab.transpose(0, 2, 1), HEAD_DIM, axis=2)

    head_of = lane // HEAD_DIM
    same = (head_of[:, None] == head_of[None, :]).astype(np.float32)
    cd = chunk_decay.reshape(n_heads // 2, 2)
    cd_rows = jnp.repeat(cd, HEAD_DIM, axis=1)
    state_decay = cd_rows[:, :, None] * same[None]
    return cos_t, sin_t, intra, per_lane(q_decay), per_lane(k_decay), state_decay, jnp.asarray(same)


def _retention_kernel(q_ref, k_ref, v_ref, g_ref, cos_ref, sin_ref, intra_ref, qd_ref, kd_ref,
                      sd_ref, same_ref, rn_ref, o_ref, state_ref, *, ts):
    @pl.when(pl.program_id(2) == 0)
    def _():
        state_ref[...] = jnp.zeros_like(state_ref)

    c = RET_CHUNK
    lane = lax.broadcasted_iota(jnp.int32, (1, LANES), 1)
    first_half = (lane % HEAD_DIM) < (HEAD_DIM // 2)
    head0 = lane < HEAD_DIM

    def rot(t, cos, sin):
        swapped = jnp.where(first_half, pltpu.roll(t, LANES - HEAD_DIM // 2, 1),
                            pltpu.roll(t, HEAD_DIM // 2, 1))
        return t * cos + swapped * sin

    for ci in range(ts // c):
        rows = slice(ci * c, (ci + 1) * c)
        cos, sin = cos_ref[rows, :], sin_ref[rows, :]
        q = rot(q_ref[rows, :].astype(F32), cos, sin)
        k = rot(k_ref[rows, :].astype(F32), cos, sin) * (HEAD_DIM ** -0.5)
        v = v_ref[rows, :]
        kb = k.astype(BF16)
        inner = []
        for hd in range(HEADS_PER_BLOCK):
            hmask = head0 if hd == 0 else jnp.logical_not(head0)
            qh = jnp.where(hmask, q, 0.0).astype(BF16)
            scores = _dot_nt(qh, kb) * intra_ref[hd]
            inner.append(_dot(scores.astype(BF16), v))
        state = state_ref[...]
        cross = _dot((q * qd_ref[...]).astype(BF16), state.astype(BF16))
        y = jnp.where(head0, inner[0], inner[1]) + cross
        kv = _dot_tn((k * kd_ref[...]).astype(BF16), v)
        state_ref[...] = state * sd_ref[...] + kv * same_ref[...]

        s0 = jnp.sum(jnp.where(head0, y, 0.0), axis=1, keepdims=True)
        s1 = jnp.sum(jnp.where(head0, 0.0, y), axis=1, keepdims=True)
        d = y - jnp.where(head0, s0, s1) * (1.0 / HEAD_DIM)
        dd = d * d
        v0 = jnp.sum(jnp.where(head0, dd, 0.0), axis=1, keepdims=True)
        v1 = jnp.sum(jnp.where(head0, 0.0, dd), axis=1, keepdims=True)
        var = jnp.where(head0, v0, v1) * (1.0 / HEAD_DIM)
        g = g_ref[rows, :].astype(F32)
        silu = g * (1.0 / (1.0 + jnp.exp(-g)))
        o_ref[rows, :] = (d * lax.rsqrt(var + GROUP_NORM_EPS) * rn_ref[...] * silu).astype(o_ref.dtype)


def _retention(proj, ret_norm, batch, seq, n_heads):
    t = proj.shape[0]
    width = n_heads * HEAD_DIM
    nb = width // LANES
    ts = min(RET_TS, seq)
    ns = seq // ts
    tabs = _retention_tables(seq, n_heads)
    cos_t, sin_t, intra, qd, kd, sd, same = tabs
    c = RET_CHUNK

    def col(group):
        return pl.BlockSpec((ts, LANES), lambda b, hp, si: (b * ns + si, group * nb + hp))

    in_specs = [col(0), col(1), col(2), col(3),
                pl.BlockSpec((ts, LANES), lambda b, hp, si: (si, 0)),
                pl.BlockSpec((ts, LANES), lambda b, hp, si: (si, 0)),
                pl.BlockSpec((None, 2, c, c), lambda b, hp, si: (hp, 0, 0, 0)),
                pl.BlockSpec((None, c, LANES), lambda b, hp, si: (hp, 0, 0)),
                pl.BlockSpec((None, c, LANES), lambda b, hp, si: (hp, 0, 0)),
                pl.BlockSpec((None, LANES, LANES), lambda b, hp, si: (hp, 0, 0)),
                pl.BlockSpec((LANES, LANES), lambda b, hp, si: (0, 0)),
                pl.BlockSpec((1, LANES), lambda b, hp, si: (0, hp))]
    return pl.pallas_call(
        functools.partial(_retention_kernel, ts=ts),
        grid=(batch, nb, ns),
        in_specs=in_specs,
        out_specs=pl.BlockSpec((ts, LANES), lambda b, hp, si: (b * ns + si, hp)),
        out_shape=jax.ShapeDtypeStruct((t, width), BF16),
        scratch_shapes=[pltpu.VMEM((LANES, LANES), F32)],
        compiler_params=_cparams(("parallel", "parallel", "arbitrary")),
        name="retention",
    )(proj, proj, proj, proj, cos_t, sin_t, intra, qd, kd, sd, same, ret_norm.reshape(1, width))


def _head_masks():
    lane = lax.broadcasted_iota(jnp.int32, (1, LANES), 1)
    head0 = lane < HEAD_DIM
    return lane, [head0, jnp.logical_not(head0)]


def _col_block(x, hd):
    cb = hd // HEADS_PER_BLOCK
    return x[:, cb * LANES:(cb + 1) * LANES]


def _head_rows(x, hd):
    return x[hd * HEAD_DIM:(hd + 1) * HEAD_DIM]


def _store_heads(o_ref, acc_t):
    for cb in range(ATT_COLS):
        pair = jnp.concatenate(acc_t[HEADS_PER_BLOCK * cb:HEADS_PER_BLOCK * (cb + 1)], axis=0)
        o_ref[:, cb * LANES:(cb + 1) * LANES] = pair.T.astype(o_ref.dtype)


def _two_stage_blocks(qi, sub, scores_to, apply_from, carry, upper_diag_first, rest_is_zero=None):
    top = 2 * qi + 1
    first, second = ((top, sub), (top - 1, 0)) if upper_diag_first else ((top - 1, 0), (top, sub))
    scores_to(0, *first)
    scores_to(1, *second)
    carry = apply_from(0, first[0], carry)

    def more(state):
        i, carry = state
        if rest_is_zero is None:
            return i < qi
        return jnp.logical_and(i < qi, jnp.logical_not(rest_is_zero(carry)))

    def pair(state):
        i, carry = state
        jb = top - 2 - 2 * i
        scores_to(0, jb, None)
        carry = apply_from(1, jnp.where(i == 0, second[0], jb + 1), carry)
        scores_to(1, jb - 1, None)
        return i + 1, apply_from(0, jb, carry)

    n_pairs, carry = lax.while_loop(more, pair, (jnp.int32(0), carry))
    last = jnp.where(n_pairs == 0, second[0], top - 1 - 2 * n_pairs)
    return apply_from(1, last, carry)


def _sb_kernel(q_ref, k_ref, vt_ref, o_ref, d0_ref, d1_ref, tot0_ref, tot1_ref, *, tq, sub):
    qi = pl.program_id(2)
    d_refs = (d0_ref, d1_ref)
    tot_refs = (tot0_ref, tot1_ref)
    _, hmasks = _head_masks()
    key = lax.broadcasted_iota(jnp.int32, (sub, tq), 0)
    qry = lax.broadcasted_iota(jnp.int32, (sub, tq), 1)
    r = lax.broadcasted_iota(jnp.int32, (sub, sub), 0)
    c = lax.broadcasted_iota(jnp.int32, (sub, sub), 1)
    suffix = jnp.where(c >= r, 1.0, 0.0).astype(BF16)
    qf = q_ref[...].astype(F32) * (HEAD_DIM ** -0.5 * LOG2E)
    heads = range(ATT_COLS * HEADS_PER_BLOCK)
    qh = [jnp.where(hmasks[hd % HEADS_PER_BLOCK], _col_block(qf, hd), 0.0).astype(BF16) for hd in heads]

    sign_bit = jnp.uint32(0x80000000)

    def scores_to(buf, jb, diag_off):
        k = k_ref[jb]
        for hd in heads:
            z = _dot_nt(_col_block(k, hd), qh[hd])
            if diag_off is not None:
                z = jnp.where((key + diag_off) < qry, z, -jnp.inf)
            neg_abs = lax.bitcast_convert_type(lax.bitcast_convert_type(z, jnp.uint32) | sign_bit, F32)
            fail = jnp.maximum(z, 0.0) + jnp.log2(1.0 + jnp.exp2(neg_abs))
            tail = _dot(suffix, fail.astype(BF16))
            d_refs[buf][hd] = jnp.minimum(z - tail, 0.0)
            tot_refs[buf][hd] = tail[0:1, :]

    def apply_from(buf, jb, carry):
        vt = vt_ref[jb]
        out = []
        for hd in heads:
            later, acc = carry[hd]
            w = jnp.exp2(d_refs[buf][hd] - later)
            out.append((later + tot_refs[buf][hd], acc + _dot(_head_rows(vt, hd), w.astype(BF16))))
        return out

    def rest_is_zero(carry):
        least = carry[0][0]
        for hd in heads[1:]:
            least = jnp.minimum(least, carry[hd][0])
        return jnp.min(least) > UNDERFLOW_BITS

    carry = [(jnp.zeros((1, tq), F32), jnp.zeros((HEAD_DIM, tq), F32)) for _ in heads]
    carry = _two_stage_blocks(qi, sub, scores_to, apply_from, carry, upper_diag_first=True,
                              rest_is_zero=rest_is_zero)
    _store_heads(o_ref, [c[1] for c in carry])


def _stick_breaking(proj, v_t, batch, seq, n_heads, first_group):
    t = proj.shape[0]
    width = n_heads * HEAD_DIM
    nb = width // LANES
    tq = min(ATT_T, seq)
    sub = tq // 2
    nq = seq // tq
    nk = seq // sub
    assert v_t.shape[2] == sub
    proj_k = proj.reshape(t // sub, sub, proj.shape[1])
    n_step = ATT_COLS * HEADS_PER_BLOCK
    cols = ATT_COLS * LANES
    ng = nb // ATT_COLS
    score_buf = pltpu.VMEM((n_step, sub, tq), F32)
    total_buf = pltpu.VMEM((n_step, 1, tq), F32)
    q_spec = pl.BlockSpec((tq, cols), lambda b, hp, qi: (b * nq + qi, first_group * ng + hp))
    k_spec = pl.BlockSpec((nk, sub, cols), lambda b, hp, qi: (b, 0, (first_group + 1) * ng + hp))
    v_spec = pl.BlockSpec((nk, cols, sub), lambda b, hp, qi: (b, hp, 0))
    return pl.pallas_call(
        functools.partial(_sb_kernel, tq=tq, sub=sub),
        grid=(batch, ng, nq),
        in_specs=[q_spec, k_spec, v_spec],
        out_specs=pl.BlockSpec((tq, cols), lambda b, hp, qi: (b * nq + qi, hp)),
        out_shape=jax.ShapeDtypeStruct((t, width), BF16),
        scratch_shapes=[score_buf, score_buf, total_buf, total_buf],
        compiler_params=_cparams(("parallel", "parallel", "arbitrary")),
        name="stick_breaking",
    )(proj, proj_k, v_t)


def _forget_cum_kernel(f_ref, b_ref, col_ref, *, seq):
    x = f_ref[...] + b_ref[...]
    log_f = jnp.minimum(x, 0.0) - jnp.log(1.0 + jnp.exp(-jnp.abs(x)))
    xt = log_f.T
    pos = lax.broadcasted_iota(jnp.int32, xt.shape, 1)
    shift = 1
    while shift < seq:
        xt = xt + jnp.where(pos >= shift, pltpu.roll(xt, shift, 1), 0.0)
        shift *= 2
    col_ref[...] = xt.T


def _forget_cum(f_logit, b_forget, batch, seq):
    t = f_logit.shape[0]
    return pl.pallas_call(
        functools.partial(_forget_cum_kernel, seq=seq),
        grid=(batch,),
        in_specs=[pl.BlockSpec((seq, LANES), lambda b: (b, 0)),
                  pl.BlockSpec((1, LANES), lambda b: (0, 0))],
        out_specs=pl.BlockSpec((seq, LANES), lambda b: (b, 0)),
        out_shape=jax.ShapeDtypeStruct((t, LANES), F32),
        compiler_params=_cparams(("parallel",)),
        name="forget_cum",
    )(f_logit, b_forget)


def _bias_lanes(hd):
    return HEAD_DIM * (1 - hd)


def _with_bias_lanes(x, bias, lane, hmask, hd, is_query):
    pieces = []
    rest = bias
    for _ in range(N_BIAS_PIECES):
        p = rest.astype(BF16).astype(F32)
        pieces.append(p)
        rest = rest - p
    base = _bias_lanes(hd)
    out = jnp.where(hmask, x, 0.0)
    for i, p in enumerate(pieces):
        piece_lane = base + i if is_query else base + N_BIAS_PIECES + i
        one_lane = base + N_BIAS_PIECES + i if is_query else base + i
        out = jnp.where(lane == piece_lane, p if is_query else -p, out)
        out = jnp.where(lane == one_lane, 1.0, out)
    return out.astype(BF16)


def _fox_kernel(q_ref, k_ref, vt_ref, cq_ref, ck_ref, o_ref, kp_ref, vp_ref, s0_ref, s1_ref,
                max0_ref, max1_ref, *, tq, sub, seq):
    hp = pl.program_id(1)
    qi = pl.program_id(2)
    s_refs = (s0_ref, s1_ref)
    max_refs = (max0_ref, max1_ref)
    nk = seq // sub
    lane, hmasks = _head_masks()
    heads = range(ATT_COLS * HEADS_PER_BLOCK)

    def head_bias(cum, hd):
        sel = lane == ATT_COLS * HEADS_PER_BLOCK * hp + hd
        return jnp.sum(jnp.where(sel, cum, 0.0), axis=1, keepdims=True) * LOG2E

    @pl.when(qi == 0)
    def _():
        kf = k_ref[...].astype(F32)
        ck = ck_ref[...]
        extra = lax.broadcasted_iota(jnp.int32, (FOX_VALUE_ROWS - HEAD_DIM, sub), 0)
        ones = jnp.where(extra == 0, 1.0, 0.0).astype(BF16)
        for hd in heads:
            side = hd % HEADS_PER_BLOCK
            kp_ref[hd] = _with_bias_lanes(_col_block(kf, hd), head_bias(ck, hd), lane, hmasks[side], side, False)
            for jb in range(nk):
                vp_ref[hd, jb, 0:HEAD_DIM] = _head_rows(vt_ref[jb], hd)
                vp_ref[hd, jb, HEAD_DIM:FOX_VALUE_ROWS] = ones

    qf = q_ref[...].astype(F32) * (HEAD_DIM ** -0.5 * LOG2E)
    cq = cq_ref[...]
    qh = [_with_bias_lanes(_col_block(qf, hd), head_bias(cq, hd), lane, hmasks[hd % HEADS_PER_BLOCK],
                           hd % HEADS_PER_BLOCK, True) for hd in heads]
    key = lax.broadcasted_iota(jnp.int32, (sub, tq), 0)
    qry = lax.broadcasted_iota(jnp.int32, (sub, tq), 1)

    def scores_to(buf, jb, diag_off):
        ks = pl.multiple_of(jb * sub, sub)
        for hd in heads:
            s = _dot_nt(kp_ref[hd, pl.ds(ks, sub), :], qh[hd])
            if diag_off is not None:
                s = jnp.where((key + diag_off) <= qry, s, -jnp.inf)
            s_refs[buf][hd] = s
            max_refs[buf][hd] = jnp.max(s, axis=0, keepdims=True)

    def apply_from(buf, jb, carry):
        out = []
        for hd in heads:
            m, acc = carry[hd]
            m_new = jnp.maximum(m, max_refs[buf][hd])
            p = jnp.exp2(s_refs[buf][hd] - m_new).astype(BF16)
            out.append((m_new, jnp.exp2(m - m_new) * acc + _dot(vp_ref[hd, jb], p)))
        return out

    carry = [(jnp.full((1, tq), -jnp.inf, F32), jnp.zeros((FOX_VALUE_ROWS, tq), F32)) for _ in heads]
    carry = _two_stage_blocks(qi, sub, scores_to, apply_from, carry, upper_diag_first=False)
    normed = []
    for hd in heads:
        acc = carry[hd][1]
        normed.append(acc[0:HEAD_DIM] * (1.0 / acc[HEAD_DIM:HEAD_DIM + 1, :]))
    _store_heads(o_ref, normed)


def _forgetting_attention(proj, v_t, cum_cols, batch, seq, n_heads):
    t = proj.shape[0]
    width = n_heads * HEAD_DIM
    nb = width // LANES
    tq = min(ATT_T, seq)
    sub = tq // 2
    nq = seq // tq
    nk = seq // sub
    assert v_t.shape[2] == sub
    n_step = ATT_COLS * HEADS_PER_BLOCK
    cols = ATT_COLS * LANES
    ng = nb // ATT_COLS
    score_buf = pltpu.VMEM((n_step, sub, tq), F32)
    max_buf = pltpu.VMEM((n_step, 1, tq), F32)
    q_spec = pl.BlockSpec((tq, cols), lambda b, hp, qi: (b * nq + qi, hp))
    k_spec = pl.BlockSpec((seq, cols), lambda b, hp, qi: (b, ng + hp))
    v_spec = pl.BlockSpec((nk, cols, sub), lambda b, hp, qi: (b, hp, 0))
    cq_spec = pl.BlockSpec((tq, LANES), lambda b, hp, qi: (b * nq + qi, 0))
    ck_spec = pl.BlockSpec((seq, LANES), lambda b, hp, qi: (b, 0))
    return pl.pallas_call(
        functools.partial(_fox_kernel, tq=tq, sub=sub, seq=seq),
        grid=(batch, ng, nq),
        in_specs=[q_spec, k_spec, v_spec, cq_spec, ck_spec],
        out_specs=pl.BlockSpec((tq, cols), lambda b, hp, qi: (b * nq + qi, hp)),
        out_shape=jax.ShapeDtypeStruct((t, width), BF16),
        scratch_shapes=[pltpu.VMEM((n_step, seq, LANES), BF16),
                        pltpu.VMEM((n_step, nk, FOX_VALUE_ROWS, sub), BF16),
                        score_buf, score_buf, max_buf, max_buf],
        compiler_params=_cparams(("parallel", "parallel", "arbitrary")),
        name="forgetting_attention",
    )(proj, proj, v_t, cum_cols, cum_cols)


def _swiglu_partial(x, wg_ref, wu_ref, wd_ref):
    g = _dot(x, wg_ref[...])
    u = _dot(x, wu_ref[...])
    a = g * (1.0 / (1.0 + jnp.exp(-g))) * u
    return _dot(a.astype(BF16), wd_ref[...])


def _ffn_dense_kernel(x_ref, wg_ref, wu_ref, wd_ref, h_ref, gain_ref, ho_ref, xn_ref, acc_ref):
    f = pl.program_id(1)

    @pl.when(f == 0)
    def _():
        acc_ref[...] = h_ref[...]

    acc_ref[...] += _swiglu_partial(x_ref[...], wg_ref, wu_ref, wd_ref)

    @pl.when(f == pl.num_programs(1) - 1)
    def _():
        h_new = acc_ref[...]
        ho_ref[...] = h_new
        xn_ref[...] = _rms(h_new, gain_ref[...]).astype(xn_ref.dtype)


def _ffn_dense(xn, wg, wu, wd, h, next_gain):
    t, d = h.shape
    dff = wg.shape[1]
    tm = min(FFN_TM, t)
    tf = dff // FFN_NF
    row = pl.BlockSpec((tm, d), lambda i, f: (i, 0))
    return pl.pallas_call(
        _ffn_dense_kernel,
        grid=(t // tm, FFN_NF),
        in_specs=[row,
                  pl.BlockSpec((d, tf), lambda i, f: (0, f)),
                  pl.BlockSpec((d, tf), lambda i, f: (0, f)),
                  pl.BlockSpec((tf, d), lambda i, f: (f, 0)),
                  row,
                  pl.BlockSpec((1, d), lambda i, f: (0, 0))],
        out_specs=[row, row],
        out_shape=[jax.ShapeDtypeStruct((t, d), F32), jax.ShapeDtypeStruct((t, d), BF16)],
        scratch_shapes=[pltpu.VMEM((tm, d), F32)],
        compiler_params=_cparams(("parallel", "arbitrary")),
        name="ffn_dense",
    )(xn, wg, wu, wd, h, next_gain.reshape(1, d))


def _ffn_grouped_kernel(te_ref, nu_ref, x_ref, wg_ref, wu_ref, wd_ref, o_ref, acc_ref):
    i = pl.program_id(0)
    f = pl.program_id(1)
    used = i < nu_ref[0]

    @pl.when(jnp.logical_and(used, f == 0))
    def _():
        acc_ref[...] = jnp.zeros_like(acc_ref)

    @pl.when(used)
    def _():
        acc_ref[...] += _swiglu_partial(x_ref[...].astype(BF16), wg_ref, wu_ref, wd_ref)

    @pl.when(f == pl.num_programs(1) - 1)
    def _():
        @pl.when(used)
        def _():
            o_ref[...] = acc_ref[...]

        @pl.when(jnp.logical_not(used))
        def _():
            o_ref[...] = jnp.zeros_like(o_ref)


def _ffn_grouped(x_sorted, wg, wu, wd, tile_expert, n_used):
    r, d = x_sorted.shape
    dff = wg.shape[2]
    tm = FFN_TM
    tf = dff // FFN_NF
    grid_spec = pltpu.PrefetchScalarGridSpec(
        num_scalar_prefetch=2,
        grid=(r // tm, FFN_NF),
        in_specs=[pl.BlockSpec((tm, d), lambda i, f, te, nu: (i, 0)),
                  pl.BlockSpec((None, d, tf), lambda i, f, te, nu: (te[i], 0, f)),
                  pl.BlockSpec((None, d, tf), lambda i, f, te, nu: (te[i], 0, f)),
                  pl.BlockSpec((None, tf, d), lambda i, f, te, nu: (te[i], f, 0))],
        out_specs=pl.BlockSpec((tm, d), lambda i, f, te, nu: (i, 0)),
        scratch_shapes=[pltpu.VMEM((tm, d), F32)],
    )
    return pl.pallas_call(
        _ffn_grouped_kernel,
        grid_spec=grid_spec,
        out_shape=jax.ShapeDtypeStruct((r, d), F32),
        compiler_params=_cparams(("arbitrary", "arbitrary")),
        name="ffn_grouped",
    )(tile_expert, n_used, x_sorted, wg, wu, wd)


def _router_kernel(h_ref, gain_ref, wr_ref, xn_ref, route_ref):
    xn = _rms(h_ref[...], gain_ref[...])
    xn_ref[...] = xn
    logits = jnp.dot(xn, wr_ref[...], precision=lax.Precision.HIGHEST, preferred_element_type=F32)
    lane = lax.broadcasted_iota(jnp.int32, logits.shape, 1)
    lane_f = lane.astype(F32)
    lg = jnp.where(lane < N_EXPERTS, logits, -jnp.inf)
    v1 = jnp.max(lg, axis=1, keepdims=True)
    i1 = jnp.min(jnp.where(lg == v1, lane_f, float(LANES)), axis=1, keepdims=True)
    lg2 = jnp.where(lane_f == i1, -jnp.inf, lg)
    v2 = jnp.max(lg2, axis=1, keepdims=True)
    i2 = jnp.min(jnp.where(lg2 == v2, lane_f, float(LANES)), axis=1, keepdims=True)
    e2 = jnp.exp(v2 - v1)
    g1 = 1.0 / (1.0 + e2)
    g2 = e2 / (1.0 + e2)
    route_ref[...] = jnp.where(lane == 0, i1, jnp.where(lane == 1, i2,
                               jnp.where(lane == 2, g1, jnp.where(lane == 3, g2, 0.0))))


def _router(h, gain, w_router):
    t, d = h.shape
    tm = min(ROW_TM, t)
    wr = jnp.zeros((d, LANES), F32).at[:, :N_EXPERTS].set(w_router)
    row = pl.BlockSpec((tm, d), lambda i: (i, 0))
    return pl.pallas_call(
        _router_kernel,
        grid=(t // tm,),
        in_specs=[row, pl.BlockSpec((1, d), lambda i: (0, 0)),
                  pl.BlockSpec((d, LANES), lambda i: (0, 0))],
        out_specs=[row, pl.BlockSpec((tm, LANES), lambda i: (i, 0))],
        out_shape=[jax.ShapeDtypeStruct((t, d), F32), jax.ShapeDtypeStruct((t, LANES), F32)],
        compiler_params=_cparams(("parallel",)),
        name="router",
    )(h, gain.reshape(1, d), wr)


def _scatter_kernel(dest_ref, x_ref, init_hbm, o_hbm, sem, *, tm):
    del init_hbm

    def row_copy(r, kk):
        return pltpu.make_async_copy(x_ref.at[pl.ds(r, 1), :],
                                     o_hbm.at[pl.ds(dest_ref[0, 0, TOP_K * r + kk], 1), :], sem)

    def issue(r, carry):
        for kk in range(TOP_K):
            row_copy(r, kk).start(priority=kk)
        return carry

    lax.fori_loop(0, tm, issue, 0, unroll=DMA_UNROLL)

    for _ in range(TOP_K):
        pltpu.make_async_copy(x_ref, o_hbm.at[pl.ds(0, tm), :], sem).wait()


def _scatter_rows(xn, dest, n_rows):
    t, d = xn.shape
    tm = min(GATHER_TM, t)
    nt = t // tm
    dest3 = dest.reshape(nt, 1, TOP_K * tm)
    init = jnp.zeros((n_rows, d), F32)
    return pl.pallas_call(
        functools.partial(_scatter_kernel, tm=tm),
        grid=(nt,),
        in_specs=[pl.BlockSpec((1, 1, TOP_K * tm), lambda i: (i, 0, 0), memory_space=pltpu.SMEM),
                  pl.BlockSpec((tm, d), lambda i: (i, 0)),
                  pl.BlockSpec(memory_space=pl.ANY)],
        out_specs=pl.BlockSpec(memory_space=pl.ANY),
        out_shape=jax.ShapeDtypeStruct((n_rows, d), F32),
        scratch_shapes=[pltpu.SemaphoreType.DMA(())],
        input_output_aliases={2: 0},
        compiler_params=_cparams(("arbitrary",)),
        name="moe_scatter",
    )(dest3, xn, init)


def _combine_kernel(dcur_ref, dnext_ref, y_hbm, h_ref, route_ref, gain_ref, o_ref, buf, sems, *, tm):
    i = pl.program_id(0)
    n = pl.num_programs(0)
    slot = i % 2

    def row_copy(d_ref, s, r, kk):
        return pltpu.make_async_copy(y_hbm.at[pl.ds(d_ref[0, 0, TOP_K * r + kk], 1), :],
                                     buf.at[s, kk, pl.ds(r, 1), :], sems.at[s])

    def issue(d_ref, s):
        def body(r, carry):
            for kk in range(TOP_K):
                row_copy(d_ref, s, r, kk).start(priority=kk)
            return carry
        lax.fori_loop(0, tm, body, 0, unroll=DMA_UNROLL)

    @pl.when(i == 0)
    def _():
        issue(dcur_ref, slot)

    @pl.when(i + 1 < n)
    def _():
        issue(dnext_ref, 1 - slot)

    for kk in range(TOP_K):
        pltpu.make_async_copy(y_hbm.at[pl.ds(0, tm), :], buf.at[slot, kk], sems.at[slot]).wait()

    route = route_ref[...]
    lane = lax.broadcasted_iota(jnp.int32, route.shape, 1)
    g1 = jnp.sum(jnp.where(lane == 2, route, 0.0), axis=1, keepdims=True)
    g2 = jnp.sum(jnp.where(lane == 3, route, 0.0), axis=1, keepdims=True)
    out = h_ref[...] + (g1 * buf[slot, 0] + g2 * buf[slot, 1])
    o_ref[...] = _rms(out, gain_ref[...])


def _combine(y_sorted, dest, h, route, final_gain):
    t, d = h.shape
    tm = min(GATHER_TM, t)
    nt = t // tm
    dest3 = dest.reshape(nt, 1, TOP_K * tm)
    row = pl.BlockSpec((tm, d), lambda i: (i, 0))
    smem = functools.partial(pl.BlockSpec, (1, 1, TOP_K * tm), memory_space=pltpu.SMEM)
    return pl.pallas_call(
        functools.partial(_combine_kernel, tm=tm),
        grid=(nt,),
        in_specs=[smem(index_map=lambda i: (i, 0, 0)),
                  smem(index_map=lambda i: (jnp.minimum(i + 1, nt - 1), 0, 0)),
                  pl.BlockSpec(memory_space=pl.ANY),
                  row,
                  pl.BlockSpec((tm, LANES), lambda i: (i, 0)),
                  pl.BlockSpec((1, d), lambda i: (0, 0))],
        out_specs=row,
        out_shape=jax.ShapeDtypeStruct((t, d), F32),
        scratch_shapes=[pltpu.VMEM((2, TOP_K, tm, d), F32), pltpu.SemaphoreType.DMA((2,))],
        compiler_params=_cparams(("arbitrary",)),
        name="moe_combine",
    )(dest3, dest3, y_sorted, h, route, final_gain.reshape(1, d))


def _routing_plan(expert_idx, tm):
    t = expert_idx.shape[0]
    flat = expert_idx.reshape(-1)
    onehot = (flat[:, None] == jnp.arange(N_EXPERTS, dtype=jnp.int32)[None, :]).astype(jnp.int32)
    running = jnp.cumsum(onehot, axis=0)
    rank = jnp.sum((running - onehot) * onehot, axis=1)
    counts = running[-1]
    tiles = (counts + tm - 1) // tm
    tile_end = jnp.cumsum(tiles)
    group_start = (tile_end - tiles) * tm
    dest = jnp.sum(onehot * group_start[None, :], axis=1) + rank
    n_tiles = (t * TOP_K) // tm + N_EXPERTS
    n_used = tile_end[-1]
    tile_id = jnp.minimum(jnp.arange(n_tiles, dtype=jnp.int32), n_used - 1)
    tile_expert = jnp.sum((tile_id[:, None] >= tile_end[None, :]).astype(jnp.int32), axis=1)
    return dest.reshape(t, TOP_K).astype(jnp.int32), tile_expert.astype(jnp.int32), \
        n_used.reshape(1).astype(jnp.int32), n_tiles * tm


def kernel(x, attn_norm_even, w_in_even, ret_norm_even, w_out_even, ffn_norm_even, w_gate_even, w_up_even, w_down_even, attn_norm_odd, w_in_odd, b_forget_odd, w_out_odd, ffn_norm_odd, w_router_odd, w_gate_moe_odd, w_up_moe_odd, w_down_moe_odd, final_norm):
    batch, seq, d = x.shape
    t = batch * seq
    n_ret = d // (2 * HEAD_DIM)
    n_sb = d // (2 * HEAD_DIM)
    n_fox = d // HEAD_DIM
    ret_width = n_ret * HEAD_DIM
    sb_width = n_sb * HEAD_DIM
    fox_width = n_fox * HEAD_DIM
    sub = min(ATT_T, seq) // 2
    h0 = x.reshape(t, d)

    w_in = w_in_even[0]
    n_direct = 4 * ret_width + 2 * sb_width
    proj, xn0 = _norm_mm(h0, attn_norm_even[0], w_in[:, :n_direct].astype(BF16), BF16)
    v_sb_t = _mm_nt(xn0, w_in[:, n_direct:].T.astype(BF16), sub)
    y_ret = _retention(proj, ret_norm_even[0], batch, seq, n_ret)
    y_sb = _stick_breaking(proj, v_sb_t, batch, seq, n_sb, first_group=4)
    w_out = w_out_even[0].astype(BF16)
    h1, xn1 = _outproj([y_ret, y_sb], [w_out[:ret_width], w_out[ret_width:]], h0, ffn_norm_even[0])
    h2, xn2 = _ffn_dense(xn1, w_gate_even[0].astype(BF16), w_up_even[0].astype(BF16),
                         w_down_even[0].astype(BF16), h1, attn_norm_odd[0])

    w_in = w_in_odd[0]
    proj = _mm(xn2, w_in[:, :2 * fox_width].astype(BF16), BF16)
    v_fox_t = _mm_nt(xn2, w_in[:, 2 * fox_width:3 * fox_width].T.astype(BF16), sub)
    w_f = jnp.zeros((d, LANES), BF16).at[:, :n_fox].set(w_in[:, 3 * fox_width:].astype(BF16))
    f_logit = _mm(xn2, w_f, F32)
    b_f = jnp.zeros((1, LANES), F32).at[0, :n_fox].set(b_forget_odd[0])
    cum_cols = _forget_cum(f_logit, b_f, batch, seq)
    y_fox = _forgetting_attention(proj, v_fox_t, cum_cols, batch, seq, n_fox)
    h3 = _outproj([y_fox], [w_out_odd[0].astype(BF16)], h2)

    xn3, route = _router(h3, ffn_norm_odd[0], w_router_odd[0])
    expert_idx = route[:, :TOP_K].astype(jnp.int32)
    dest, tile_expert, n_used, n_rows = _routing_plan(expert_idx, FFN_TM)
    x_sorted = _scatter_rows(xn3, dest, n_rows)
    y_sorted = _ffn_grouped(x_sorted, w_gate_moe_odd[0].astype(BF16), w_up_moe_odd[0].astype(BF16),
                            w_down_moe_odd[0].astype(BF16), tile_expert, n_used)
    out = _combine(y_sorted, dest, h3, route, final_norm)
    return out.reshape(batch, seq, d)
```

```python
import functools

import jax
import jax.numpy as jnp
import numpy as np
from jax import lax
from jax.experimental import pallas as pl
from jax.experimental.pallas import tpu as pltpu

F32 = jnp.float32
BF16 = jnp.bfloat16

LANES = 128
HEAD_DIM = 64
HEADS_PER_BLOCK = LANES // HEAD_DIM
N_EXPERTS = 8
TOP_K = 2
ROPE_BASE = 10000.0
NORM_EPS = 1e-6
GROUP_NORM_EPS = 1e-5
LOG2E = 1.4426950408889634
UNDERFLOW_BITS = 160.0
FOX_MARGIN_BITS = 8.0
NORM_SLACK = 1.02
RET_CHUNK = 256
VMEM_LIMIT = 56 * 1024 * 1024

MM_TM = 1024
MM_TN = 1024
ROW_TM = 512
ATT_T = 512
ATT_COLS = 2
RET_TS = 512
FFN_TM = 256
FFN_NF = 1
GATHER_TM = 256
DMA_UNROLL = 8
N_BIAS_PIECES = 3
FOX_VALUE_ROWS = HEAD_DIM + 16


def _cparams(sem, vmem=VMEM_LIMIT):
    return pltpu.CompilerParams(dimension_semantics=sem, vmem_limit_bytes=vmem)


def _rms(xf, gain_row):
    ms = jnp.mean(xf * xf, axis=-1, keepdims=True)
    return xf * lax.rsqrt(ms + NORM_EPS) * gain_row


def _dot(a, b):
    return jnp.dot(a, b, preferred_element_type=F32)


def _dot_nt(a, b):
    return lax.dot_general(a, b, (((1,), (1,)), ((), ())), preferred_element_type=F32)


def _dot_tn(a, b):
    return lax.dot_general(a, b, (((0,), (0,)), ((), ())), preferred_element_type=F32)


def _norm_mm_kernel(x_ref, g_ref, w_ref, o_ref, xn_ref):
    @pl.when(pl.program_id(1) == 0)
    def _():
        xn_ref[...] = _rms(x_ref[...], g_ref[...]).astype(BF16)

    o_ref[...] = _dot(xn_ref[...], w_ref[...]).astype(o_ref.dtype)


def _norm_mm(x, gain, w, out_dtype):
    t, k = x.shape
    n = w.shape[1]
    tm, tn = min(MM_TM, t), min(MM_TN, n)
    return pl.pallas_call(
        _norm_mm_kernel,
        grid=(t // tm, n // tn),
        in_specs=[pl.BlockSpec((tm, k), lambda i, j: (i, 0)),
                  pl.BlockSpec((1, k), lambda i, j: (0, 0)),
                  pl.BlockSpec((k, tn), lambda i, j: (0, j))],
        out_specs=[pl.BlockSpec((tm, tn), lambda i, j: (i, j)),
                   pl.BlockSpec((tm, k), lambda i, j: (i, 0))],
        out_shape=[jax.ShapeDtypeStruct((t, n), out_dtype), jax.ShapeDtypeStruct((t, k), BF16)],
        compiler_params=_cparams(("parallel", "arbitrary")),
        name="norm_mm",
    )(x, gain.reshape(1, k), w)


def _mm_kernel(x_ref, w_ref, o_ref):
    o_ref[...] = _dot(x_ref[...], w_ref[...]).astype(o_ref.dtype)


def _mm(x, w, out_dtype):
    t, k = x.shape
    n = w.shape[1]
    tm, tn = min(MM_TM, t), min(MM_TN, n)
    return pl.pallas_call(
        _mm_kernel,
        grid=(t // tm, n // tn),
        in_specs=[pl.BlockSpec((tm, k), lambda i, j: (i, 0)),
                  pl.BlockSpec((k, tn), lambda i, j: (0, j))],
        out_specs=pl.BlockSpec((tm, tn), lambda i, j: (i, j)),
        out_shape=jax.ShapeDtypeStruct((t, n), out_dtype),
        compiler_params=_cparams(("parallel", "parallel")),
        name="mm",
    )(x, w)


def _mm_nt_kernel(x_ref, wt_ref, o_ref, *, sub):
    res = _dot_nt(wt_ref[...], x_ref[...]).astype(o_ref.dtype)
    for s in range(o_ref.shape[0]):
        o_ref[s] = res[:, s * sub:(s + 1) * sub]


def _mm_nt(x, wt, sub):
    t, k = x.shape
    n = wt.shape[0]
    tm, tn = min(MM_TM, t), min(MM_TN, n)
    return pl.pallas_call(
        functools.partial(_mm_nt_kernel, sub=sub),
        grid=(t // tm, n // tn),
        in_specs=[pl.BlockSpec((tm, k), lambda i, j: (i, 0)),
                  pl.BlockSpec((tn, k), lambda i, j: (j, 0))],
        out_specs=pl.BlockSpec((tm // sub, tn, sub), lambda i, j: (i, j, 0)),
        out_shape=jax.ShapeDtypeStruct((t // sub, n, sub), BF16),
        compiler_params=_cparams(("parallel", "parallel")),
        name="mm_nt",
    )(x, wt)


def _outproj_kernel(*refs, n_in, with_norm):
    ys = refs[:n_in]
    ws = refs[n_in:2 * n_in]
    h_ref = refs[2 * n_in]
    pos = 2 * n_in + 1
    acc = h_ref[...]
    for y_ref, w_ref in zip(ys, ws):
        acc = acc + _dot(y_ref[...], w_ref[...])
    if with_norm:
        g_ref, ho_ref, xn_ref = refs[pos], refs[pos + 1], refs[pos + 2]
        ho_ref[...] = acc
        xn_ref[...] = _rms(acc, g_ref[...]).astype(xn_ref.dtype)
    else:
        refs[pos][...] = acc


def _outproj(ys, ws, h, gain=None):
    t, d = h.shape
    tm = min(ROW_TM, t)
    n_in = len(ys)
    with_norm = gain is not None
    in_specs = [pl.BlockSpec((tm, y.shape[1]), lambda i: (i, 0)) for y in ys]
    in_specs += [pl.BlockSpec(w.shape, lambda i: (0, 0)) for w in ws]
    in_specs += [pl.BlockSpec((tm, d), lambda i: (i, 0))]
    args = list(ys) + list(ws) + [h]
    row_spec = pl.BlockSpec((tm, d), lambda i: (i, 0))
    if with_norm:
        in_specs += [pl.BlockSpec((1, d), lambda i: (0, 0))]
        args += [gain.reshape(1, d)]
        out_specs = [row_spec, row_spec]
        out_shape = [jax.ShapeDtypeStruct((t, d), F32), jax.ShapeDtypeStruct((t, d), BF16)]
    else:
        out_specs = row_spec
        out_shape = jax.ShapeDtypeStruct((t, d), F32)
    return pl.pallas_call(
        functools.partial(_outproj_kernel, n_in=n_in, with_norm=with_norm),
        grid=(t // tm,),
        in_specs=in_specs,
        out_specs=out_specs,
        out_shape=out_shape,
        compiler_params=_cparams(("parallel",)),
        name="outproj",
    )(*args)


def _retention_tables(seq, n_heads):
    half = HEAD_DIM // 2
    lane = np.arange(LANES)
    inv_freq = ROPE_BASE ** (-jnp.arange(half, dtype=F32) / half)
    ang = jnp.arange(seq, dtype=F32)[:, None] * inv_freq[None, :]
    cos, sin = jnp.cos(ang), jnp.sin(ang)
    cos_t = jnp.tile(cos, (1, LANES // half))
    sign = np.where((lane % HEAD_DIM) < half, -1.0, 1.0).astype(np.float32)
    sin_t = jnp.tile(sin, (1, LANES // half)) * sign[None, :]
    c = RET_CHUNK
    log_gamma = jnp.log(1.0 - 2.0 ** (-5.0 - jnp.arange(n_heads, dtype=F32)))
    pos = jnp.arange(c, dtype=F32)
    diff = pos[:, None] - pos[None, :]
    intra = jnp.where(diff >= 0.0,
                      jnp.exp(log_gamma[:, None, None] * jnp.maximum(diff, 0.0)), 0.0)
    intra = intra.reshape(n_heads // 2, 2, c, c)
    q_decay = jnp.exp(log_gamma[:, None] * (pos + 1.0))
    k_decay = jnp.exp(log_gamma[:, None] * (c - 1.0 - pos))
    chunk_decay = jnp.exp(log_gamma * c)

    def per_lane(tab):
        tab = tab.reshape(n_heads // 2, 2, c)
        return jnp.repeat(tab.transpose(0, 2, 1), HEAD_DIM, axis=2)

    head_of = lane // HEAD_DIM
    same = (head_of[:, None] == head_of[None, :]).astype(np.float32)
    cd = chunk_decay.reshape(n_heads // 2, 2)
    cd_rows = jnp.repeat(cd, HEAD_DIM, axis=1)
    state_decay = cd_rows[:, :, None] * same[None]
    return cos_t, sin_t, intra, per_lane(q_decay), per_lane(k_decay), state_decay, jnp.asarray(same)


def _retention_kernel(q_ref, k_ref, v_ref, g_ref, cos_ref, sin_ref, intra_ref, qd_ref, kd_ref,
                      sd_ref, same_ref, rn_ref, o_ref, state_ref, *, ts):
    @pl.when(pl.program_id(2) == 0)
    def _():
        state_ref[...] = jnp.zeros_like(state_ref)

    c = RET_CHUNK
    lane = lax.broadcasted_iota(jnp.int32, (1, LANES), 1)
    first_half = (lane % HEAD_DIM) < (HEAD_DIM // 2)
    head0 = lane < HEAD_DIM

    def rot(t, cos, sin):
        swapped = jnp.where(first_half, pltpu.roll(t, LANES - HEAD_DIM // 2, 1),
                            pltpu.roll(t, HEAD_DIM // 2, 1))
        return t * cos + swapped * sin

    for ci in range(ts // c):
        rows = slice(ci * c, (ci + 1) * c)
        cos, sin = cos_ref[rows, :], sin_ref[rows, :]
        q = rot(q_ref[rows, :].astype(F32), cos, sin)
        k = rot(k_ref[rows, :].astype(F32), cos, sin) * (HEAD_DIM ** -0.5)
        v = v_ref[rows, :]
        kb = k.astype(BF16)
        inner = []
        for hd in range(HEADS_PER_BLOCK):
            hmask = head0 if hd == 0 else jnp.logical_not(head0)
            qh = jnp.where(hmask, q, 0.0).astype(BF16)
            scores = _dot_nt(qh, kb) * intra_ref[hd]
            inner.append(_dot(scores.astype(BF16), v))
        state = state_ref[...]
        cross = _dot((q * qd_ref[...]).astype(BF16), state.astype(BF16))
        y = jnp.where(head0, inner[0], inner[1]) + cross
        kv = _dot_tn((k * kd_ref[...]).astype(BF16), v)
        state_ref[...] = state * sd_ref[...] + kv * same_ref[...]

        s0 = jnp.sum(jnp.where(head0, y, 0.0), axis=1, keepdims=True)
        s1 = jnp.sum(jnp.where(head0, 0.0, y), axis=1, keepdims=True)
        d = y - jnp.where(head0, s0, s1) * (1.0 / HEAD_DIM)
        dd = d * d
        v0 = jnp.sum(jnp.where(head0, dd, 0.0), axis=1, keepdims=True)
        v1 = jnp.sum(jnp.where(head0, 0.0, dd), axis=1, keepdims=True)
        var = jnp.where(head0, v0, v1) * (1.0 / HEAD_DIM)
        g = g_ref[rows, :].astype(F32)
        silu = g * (1.0 / (1.0 + jnp.exp(-g)))
        o_ref[rows, :] = (d * lax.rsqrt(var + GROUP_NORM_EPS) * rn_ref[...] * silu).astype(o_ref.dtype)


def _retention(proj, ret_norm, batch, seq, n_heads):
    t = proj.shape[0]
    width = n_heads * HEAD_DIM
    nb = width // LANES
    ts = min(RET_TS, seq)
    ns = seq // ts
    tabs = _retention_tables(seq, n_heads)
    cos_t, sin_t, intra, qd, kd, sd, same = tabs
    c = RET_CHUNK

    def col(group):
        return pl.BlockSpec((ts, LANES), lambda b, hp, si: (b * ns + si, group * nb + hp))

    in_specs = [col(0), col(1), col(2), col(3),
                pl.BlockSpec((ts, LANES), lambda b, hp, si: (si, 0)),
                pl.BlockSpec((ts, LANES), lambda b, hp, si: (si, 0)),
                pl.BlockSpec((None, 2, c, c), lambda b, hp, si: (hp, 0, 0, 0)),
                pl.BlockSpec((None, c, LANES), lambda b, hp, si: (hp, 0, 0)),
                pl.BlockSpec((None, c, LANES), lambda b, hp, si: (hp, 0, 0)),
                pl.BlockSpec((None, LANES, LANES), lambda b, hp, si: (hp, 0, 0)),
                pl.BlockSpec((LANES, LANES), lambda b, hp, si: (0, 0)),
                pl.BlockSpec((1, LANES), lambda b, hp, si: (0, hp))]
    return pl.pallas_call(
        functools.partial(_retention_kernel, ts=ts),
        grid=(batch, nb, ns),
        in_specs=in_specs,
        out_specs=pl.BlockSpec((ts, LANES), lambda b, hp, si: (b * ns + si, hp)),
        out_shape=jax.ShapeDtypeStruct((t, width), BF16),
        scratch_shapes=[pltpu.VMEM((LANES, LANES), F32)],
        compiler_params=_cparams(("parallel", "parallel", "arbitrary")),
        name="retention",
    )(proj, proj, proj, proj, cos_t, sin_t, intra, qd, kd, sd, same, ret_norm.reshape(1, width))


def _head_masks():
    lane = lax.broadcasted_iota(jnp.int32, (1, LANES), 1)
    head0 = lane < HEAD_DIM
    return lane, [head0, jnp.logical_not(head0)]


def _col_block(x, hd):
    cb = hd // HEADS_PER_BLOCK
    return x[:, cb * LANES:(cb + 1) * LANES]


def _head_rows(x, hd):
    return x[hd * HEAD_DIM:(hd + 1) * HEAD_DIM]


def _store_heads(o_ref, acc_t):
    for cb in range(ATT_COLS):
        pair = jnp.concatenate(acc_t[HEADS_PER_BLOCK * cb:HEADS_PER_BLOCK * (cb + 1)], axis=0)
        o_ref[:, cb * LANES:(cb + 1) * LANES] = pair.T.astype(o_ref.dtype)


def _two_stage_blocks(qi, sub, scores_to, apply_from, carry, upper_diag_first, rest_is_zero=None):
    top = 2 * qi + 1
    first, second = ((top, sub), (top - 1, 0)) if upper_diag_first else ((top - 1, 0), (top, sub))
    scores_to(0, *first)
    scores_to(1, *second)
    carry = apply_from(0, first[0], carry)

    def more(state):
        i, carry = state
        if rest_is_zero is None:
            return i < qi
        return jnp.logical_and(i < qi, jnp.logical_not(rest_is_zero(carry, top - 2 - 2 * i)))

    def pair(state):
        i, carry = state
        jb = top - 2 - 2 * i
        scores_to(0, jb, None)
        carry = apply_from(1, jnp.where(i == 0, second[0], jb + 1), carry)
        scores_to(1, jb - 1, None)
        return i + 1, apply_from(0, jb, carry)

    n_pairs, carry = lax.while_loop(more, pair, (jnp.int32(0), carry))
    last = jnp.where(n_pairs == 0, second[0], top - 1 - 2 * n_pairs)
    return apply_from(1, last, carry)


def _sb_kernel(q_ref, k_ref, vt_ref, o_ref, d0_ref, d1_ref, tot0_ref, tot1_ref, *, tq, sub):
    qi = pl.program_id(2)
    d_refs = (d0_ref, d1_ref)
    tot_refs = (tot0_ref, tot1_ref)
    _, hmasks = _head_masks()
    key = lax.broadcasted_iota(jnp.int32, (sub, tq), 0)
    qry = lax.broadcasted_iota(jnp.int32, (sub, tq), 1)
    r = lax.broadcasted_iota(jnp.int32, (sub, sub), 0)
    c = lax.broadcasted_iota(jnp.int32, (sub, sub), 1)
    suffix = jnp.where(c >= r, 1.0, 0.0).astype(BF16)
    qf = q_ref[...].astype(F32) * (HEAD_DIM ** -0.5 * LOG2E)
    heads = range(ATT_COLS * HEADS_PER_BLOCK)
    qh = [jnp.where(hmasks[hd % HEADS_PER_BLOCK], _col_block(qf, hd), 0.0).astype(BF16) for hd in heads]

    sign_bit = jnp.uint32(0x80000000)

    def scores_to(buf, jb, diag_off):
        k = k_ref[jb]
        for hd in heads:
            z = _dot_nt(_col_block(k, hd), qh[hd])
            if diag_off is not None:
                z = jnp.where((key + diag_off) < qry, z, -jnp.inf)
            neg_abs = lax.bitcast_convert_type(lax.bitcast_convert_type(z, jnp.uint32) | sign_bit, F32)
            fail = jnp.maximum(z, 0.0) + jnp.log2(1.0 + jnp.exp2(neg_abs))
            tail = _dot(suffix, fail.astype(BF16))
            d_refs[buf][hd] = jnp.minimum(z - tail, 0.0)
            tot_refs[buf][hd] = tail[0:1, :]

    def apply_from(buf, jb, carry):
        vt = vt_ref[jb]
        out = []
        for hd in heads:
            later, acc = carry[hd]
            w = jnp.exp2(d_refs[buf][hd] - later)
            out.append((later + tot_refs[buf][hd], acc + _dot(_head_rows(vt, hd), w.astype(BF16))))
        return out

    def rest_is_zero(carry, _):
        least = carry[0][0]
        for hd in heads[1:]:
            least = jnp.minimum(least, carry[hd][0])
        return jnp.min(least) > UNDERFLOW_BITS

    carry = [(jnp.zeros((1, tq), F32), jnp.zeros((HEAD_DIM, tq), F32)) for _ in heads]
    carry = _two_stage_blocks(qi, sub, scores_to, apply_from, carry, upper_diag_first=True,
                              rest_is_zero=rest_is_zero)
    _store_heads(o_ref, [c[1] for c in carry])


def _stick_breaking(proj, v_t, batch, seq, n_heads, first_group):
    t = proj.shape[0]
    width = n_heads * HEAD_DIM
    nb = width // LANES
    tq = min(ATT_T, seq)
    sub = tq // 2
    nq = seq // tq
    nk = seq // sub
    assert v_t.shape[2] == sub
    proj_k = proj.reshape(t // sub, sub, proj.shape[1])
    n_step = ATT_COLS * HEADS_PER_BLOCK
    cols = ATT_COLS * LANES
    ng = nb // ATT_COLS
    score_buf = pltpu.VMEM((n_step, sub, tq), F32)
    total_buf = pltpu.VMEM((n_step, 1, tq), F32)
    q_spec = pl.BlockSpec((tq, cols), lambda b, hp, qi: (b * nq + qi, first_group * ng + hp))
    k_spec = pl.BlockSpec((nk, sub, cols), lambda b, hp, qi: (b, 0, (first_group + 1) * ng + hp))
    v_spec = pl.BlockSpec((nk, cols, sub), lambda b, hp, qi: (b, hp, 0))
    return pl.pallas_call(
        functools.partial(_sb_kernel, tq=tq, sub=sub),
        grid=(batch, ng, nq),
        in_specs=[q_spec, k_spec, v_spec],
        out_specs=pl.BlockSpec((tq, cols), lambda b, hp, qi: (b * nq + qi, hp)),
        out_shape=jax.ShapeDtypeStruct((t, width), BF16),
        scratch_shapes=[score_buf, score_buf, total_buf, total_buf],
        compiler_params=_cparams(("parallel", "parallel", "arbitrary")),
        name="stick_breaking",
    )(proj, proj_k, v_t)


def _forget_cum_kernel(f_ref, b_ref, col_ref, *, seq):
    x = f_ref[...] + b_ref[...]
    log_f = jnp.minimum(x, 0.0) - jnp.log(1.0 + jnp.exp(-jnp.abs(x)))
    xt = log_f.T
    pos = lax.broadcasted_iota(jnp.int32, xt.shape, 1)
    shift = 1
    while shift < seq:
        xt = xt + jnp.where(pos >= shift, pltpu.roll(xt, shift, 1), 0.0)
        shift *= 2
    col_ref[...] = xt.T


def _forget_cum(f_logit, b_forget, batch, seq):
    t = f_logit.shape[0]
    return pl.pallas_call(
        functools.partial(_forget_cum_kernel, seq=seq),
        grid=(batch,),
        in_specs=[pl.BlockSpec((seq, LANES), lambda b: (b, 0)),
                  pl.BlockSpec((1, LANES), lambda b: (0, 0))],
        out_specs=pl.BlockSpec((seq, LANES), lambda b: (b, 0)),
        out_shape=jax.ShapeDtypeStruct((t, LANES), F32),
        compiler_params=_cparams(("parallel",)),
        name="forget_cum",
    )(f_logit, b_forget)


def _bias_lanes(hd):
    return HEAD_DIM * (1 - hd)


def _with_bias_lanes(x, bias, lane, hmask, hd, is_query):
    pieces = []
    rest = bias
    for _ in range(N_BIAS_PIECES):
        p = rest.astype(BF16).astype(F32)
        pieces.append(p)
        rest = rest - p
    base = _bias_lanes(hd)
    out = jnp.where(hmask, x, 0.0)
    for i, p in enumerate(pieces):
        piece_lane = base + i if is_query else base + N_BIAS_PIECES + i
        one_lane = base + N_BIAS_PIECES + i if is_query else base + i
        out = jnp.where(lane == piece_lane, p if is_query else -p, out)
        out = jnp.where(lane == one_lane, 1.0, out)
    return out.astype(BF16)


def _fox_kernel(q_ref, k_ref, vt_ref, cq_ref, ck_ref, o_ref, kp_ref, vp_ref, ksq_ref, s0_ref, s1_ref,
                max0_ref, max1_ref, *, tq, sub, seq):
    hp = pl.program_id(1)
    qi = pl.program_id(2)
    s_refs = (s0_ref, s1_ref)
    max_refs = (max0_ref, max1_ref)
    nk = seq // sub
    lane, hmasks = _head_masks()
    heads = range(ATT_COLS * HEADS_PER_BLOCK)

    def head_bias(cum, hd):
        sel = lane == ATT_COLS * HEADS_PER_BLOCK * hp + hd
        return jnp.sum(jnp.where(sel, cum, 0.0), axis=1, keepdims=True) * LOG2E

    @pl.when(qi == 0)
    def _():
        kf = k_ref[...].astype(F32)
        ck = ck_ref[...]
        extra = lax.broadcasted_iota(jnp.int32, (FOX_VALUE_ROWS - HEAD_DIM, sub), 0)
        ones = jnp.where(extra == 0, 1.0, 0.0).astype(BF16)
        for hd in heads:
            side = hd % HEADS_PER_BLOCK
            k_hd = _col_block(kf, hd)
            kp_ref[hd] = _with_bias_lanes(k_hd, head_bias(ck, hd), lane, hmasks[side], side, False)
            k_sq = jnp.sum(jnp.where(hmasks[side], k_hd * k_hd, 0.0), axis=1, keepdims=True)
            ksq_ref[hd] = jnp.broadcast_to(jnp.max(k_sq, axis=0, keepdims=True), (1, LANES))
            for jb in range(nk):
                vp_ref[hd, jb, 0:HEAD_DIM] = _head_rows(vt_ref[jb], hd)
                vp_ref[hd, jb, HEAD_DIM:FOX_VALUE_ROWS] = ones

    qf = q_ref[...].astype(F32) * (HEAD_DIM ** -0.5 * LOG2E)
    cq = cq_ref[...]
    qh = [_with_bias_lanes(_col_block(qf, hd), head_bias(cq, hd), lane, hmasks[hd % HEADS_PER_BLOCK],
                           hd % HEADS_PER_BLOCK, True) for hd in heads]
    key = lax.broadcasted_iota(jnp.int32, (sub, tq), 0)
    qry = lax.broadcasted_iota(jnp.int32, (sub, tq), 1)

    pick = lax.broadcasted_iota(jnp.int32, (8, LANES), 0)
    pick_lane = lax.broadcasted_iota(jnp.int32, (8, LANES), 1)
    head_rows = jnp.where(pick == pick_lane // HEAD_DIM, 1.0, 0.0).astype(BF16)
    reach = []
    for hd in heads:
        side = hd % HEADS_PER_BLOCK
        q_hd = _col_block(qf, hd)
        q_sq = _dot_nt(head_rows, (q_hd * q_hd).astype(BF16))[side:side + 1]
        spare = _bias_lanes(side)
        bias_row = jnp.where((pick_lane >= spare) & (pick_lane < spare + N_BIAS_PIECES), 1.0, 0.0).astype(BF16)
        cq_row = _dot_nt(bias_row, qh[hd])[0:1]
        reach.append(NORM_SLACK * jnp.sqrt(q_sq * ksq_ref[hd][:, 0:1]) + cq_row)

    def scores_to(buf, jb, diag_off):
        ks = pl.multiple_of(jb * sub, sub)
        for hd in heads:
            s = _dot_nt(kp_ref[hd, pl.ds(ks, sub), :], qh[hd])
            if diag_off is not None:
                s = jnp.where((key + diag_off) <= qry, s, -jnp.inf)
            s_refs[buf][hd] = s
            max_refs[buf][hd] = jnp.max(s, axis=0, keepdims=True)

    def apply_from(buf, jb, carry):
        out = []
        for hd in heads:
            m, acc = carry[hd]
            m_new = jnp.maximum(m, max_refs[buf][hd])
            p = jnp.exp2(s_refs[buf][hd] - m_new).astype(BF16)
            out.append((m_new, jnp.exp2(m - m_new) * acc + _dot(vp_ref[hd, jb], p)))
        return out

    def rest_is_zero(carry, jb):
        ck_row = ck_ref[pl.ds(jnp.maximum((jb + 1) * sub - 1, 0), 1), :]
        worst = None
        for hd in heads:
            sel = lane == ATT_COLS * HEADS_PER_BLOCK * hp + hd
            ck_last = jnp.sum(jnp.where(sel, ck_row, 0.0), axis=1, keepdims=True) * LOG2E
            gap = reach[hd] - ck_last - carry[hd][0]
            worst = gap if worst is None else jnp.maximum(worst, gap)
        return jnp.max(worst) < -(UNDERFLOW_BITS + FOX_MARGIN_BITS)

    carry = [(jnp.full((1, tq), -jnp.inf, F32), jnp.zeros((FOX_VALUE_ROWS, tq), F32)) for _ in heads]
    carry = _two_stage_blocks(qi, sub, scores_to, apply_from, carry, upper_diag_first=False,
                              rest_is_zero=rest_is_zero)
    normed = []
    for hd in heads:
        acc = carry[hd][1]
        normed.append(acc[0:HEAD_DIM] * (1.0 / acc[HEAD_DIM:HEAD_DIM + 1, :]))
    _store_heads(o_ref, normed)


def _forgetting_attention(proj, v_t, cum_cols, batch, seq, n_heads):
    t = proj.shape[0]
    width = n_heads * HEAD_DIM
    nb = width // LANES
    tq = min(ATT_T, seq)
    sub = tq // 2
    nq = seq // tq
    nk = seq // sub
    assert v_t.shape[2] == sub
    n_step = ATT_COLS * HEADS_PER_BLOCK
    cols = ATT_COLS * LANES
    ng = nb // ATT_COLS
    score_buf = pltpu.VMEM((n_step, sub, tq), F32)
    max_buf = pltpu.VMEM((n_step, 1, tq), F32)
    q_spec = pl.BlockSpec((tq, cols), lambda b, hp, qi: (b * nq + qi, hp))
    k_spec = pl.BlockSpec((seq, cols), lambda b, hp, qi: (b, ng + hp))
    v_spec = pl.BlockSpec((nk, cols, sub), lambda b, hp, qi: (b, hp, 0))
    cq_spec = pl.BlockSpec((tq, LANES), lambda b, hp, qi: (b * nq + qi, 0))
    ck_spec = pl.BlockSpec((seq, LANES), lambda b, hp, qi: (b, 0))
    return pl.pallas_call(
        functools.partial(_fox_kernel, tq=tq, sub=sub, seq=seq),
        grid=(batch, ng, nq),
        in_specs=[q_spec, k_spec, v_spec, cq_spec, ck_spec],
        out_specs=pl.BlockSpec((tq, cols), lambda b, hp, qi: (b * nq + qi, hp)),
        out_shape=jax.ShapeDtypeStruct((t, width), BF16),
        scratch_shapes=[pltpu.VMEM((n_step, seq, LANES), BF16),
                        pltpu.VMEM((n_step, nk, FOX_VALUE_ROWS, sub), BF16),
                        pltpu.VMEM((n_step, 1, LANES), F32),
                        score_buf, score_buf, max_buf, max_buf],
        compiler_params=_cparams(("parallel", "parallel", "arbitrary")),
        name="forgetting_attention",
    )(proj, proj, v_t, cum_cols, cum_cols)


def _swiglu_partial(x, wg_ref, wu_ref, wd_ref):
    g = _dot(x, wg_ref[...])
    u = _dot(x, wu_ref[...])
    a = g * (1.0 / (1.0 + jnp.exp(-g))) * u
    return _dot(a.astype(BF16), wd_ref[...])


def _ffn_dense_kernel(x_ref, wg_ref, wu_ref, wd_ref, h_ref, gain_ref, ho_ref, xn_ref, acc_ref):
    f = pl.program_id(1)

    @pl.when(f == 0)
    def _():
        acc_ref[...] = h_ref[...]

    acc_ref[...] += _swiglu_partial(x_ref[...], wg_ref, wu_ref, wd_ref)

    @pl.when(f == pl.num_programs(1) - 1)
    def _():
        h_new = acc_ref[...]
        ho_ref[...] = h_new
        xn_ref[...] = _rms(h_new, gain_ref[...]).astype(xn_ref.dtype)


def _ffn_dense(xn, wg, wu, wd, h, next_gain):
    t, d = h.shape
    dff = wg.shape[1]
    tm = min(FFN_TM, t)
    tf = dff // FFN_NF
    row = pl.BlockSpec((tm, d), lambda i, f: (i, 0))
    return pl.pallas_call(
        _ffn_dense_kernel,
        grid=(t // tm, FFN_NF),
        in_specs=[row,
                  pl.BlockSpec((d, tf), lambda i, f: (0, f)),
                  pl.BlockSpec((d, tf), lambda i, f: (0, f)),
                  pl.BlockSpec((tf, d), lambda i, f: (f, 0)),
                  row,
                  pl.BlockSpec((1, d), lambda i, f: (0, 0))],
        out_specs=[row, row],
        out_shape=[jax.ShapeDtypeStruct((t, d), F32), jax.ShapeDtypeStruct((t, d), BF16)],
        scratch_shapes=[pltpu.VMEM((tm, d), F32)],
        compiler_params=_cparams(("parallel", "arbitrary")),
        name="ffn_dense",
    )(xn, wg, wu, wd, h, next_gain.reshape(1, d))


def _ffn_grouped_kernel(te_ref, nu_ref, x_ref, wg_ref, wu_ref, wd_ref, o_ref, acc_ref):
    i = pl.program_id(0)
    f = pl.program_id(1)
    used = i < nu_ref[0]

    @pl.when(jnp.logical_and(used, f == 0))
    def _():
        acc_ref[...] = jnp.zeros_like(acc_ref)

    @pl.when(used)
    def _():
        acc_ref[...] += _swiglu_partial(x_ref[...].astype(BF16), wg_ref, wu_ref, wd_ref)

    @pl.when(f == pl.num_programs(1) - 1)
    def _():
        @pl.when(used)
        def _():
            o_ref[...] = acc_ref[...]

        @pl.when(jnp.logical_not(used))
        def _():
            o_ref[...] = jnp.zeros_like(o_ref)


def _ffn_grouped(x_sorted, wg, wu, wd, tile_expert, n_used):
    r, d = x_sorted.shape
    dff = wg.shape[2]
    tm = FFN_TM
    tf = dff // FFN_NF
    grid_spec = pltpu.PrefetchScalarGridSpec(
        num_scalar_prefetch=2,
        grid=(r // tm, FFN_NF),
        in_specs=[pl.BlockSpec((tm, d), lambda i, f, te, nu: (i, 0)),
                  pl.BlockSpec((None, d, tf), lambda i, f, te, nu: (te[i], 0, f)),
                  pl.BlockSpec((None, d, tf), lambda i, f, te, nu: (te[i], 0, f)),
                  pl.BlockSpec((None, tf, d), lambda i, f, te, nu: (te[i], f, 0))],
        out_specs=pl.BlockSpec((tm, d), lambda i, f, te, nu: (i, 0)),
        scratch_shapes=[pltpu.VMEM((tm, d), F32)],
    )
    return pl.pallas_call(
        _ffn_grouped_kernel,
        grid_spec=grid_spec,
        out_shape=jax.ShapeDtypeStruct((r, d), F32),
        compiler_params=_cparams(("arbitrary", "arbitrary")),
        name="ffn_grouped",
    )(tile_expert, n_used, x_sorted, wg, wu, wd)


def _router_kernel(h_ref, gain_ref, wr_ref, xn_ref, route_ref):
    xn = _rms(h_ref[...], gain_ref[...])
    xn_ref[...] = xn
    logits = jnp.dot(xn, wr_ref[...], precision=lax.Precision.HIGHEST, preferred_element_type=F32)
    lane = lax.broadcasted_iota(jnp.int32, logits.shape, 1)
    lane_f = lane.astype(F32)
    lg = jnp.where(lane < N_EXPERTS, logits, -jnp.inf)
    v1 = jnp.max(lg, axis=1, keepdims=True)
    i1 = jnp.min(jnp.where(lg == v1, lane_f, float(LANES)), axis=1, keepdims=True)
    lg2 = jnp.where(lane_f == i1, -jnp.inf, lg)
    v2 = jnp.max(lg2, axis=1, keepdims=True)
    i2 = jnp.min(jnp.where(lg2 == v2, lane_f, float(LANES)), axis=1, keepdims=True)
    e2 = jnp.exp(v2 - v1)
    g1 = 1.0 / (1.0 + e2)
    g2 = e2 / (1.0 + e2)
    route_ref[...] = jnp.where(lane == 0, i1, jnp.where(lane == 1, i2,
                               jnp.where(lane == 2, g1, jnp.where(lane == 3, g2, 0.0))))


def _router(h, gain, w_router):
    t, d = h.shape
    tm = min(ROW_TM, t)
    wr = jnp.zeros((d, LANES), F32).at[:, :N_EXPERTS].set(w_router)
    row = pl.BlockSpec((tm, d), lambda i: (i, 0))
    return pl.pallas_call(
        _router_kernel,
        grid=(t // tm,),
        in_specs=[row, pl.BlockSpec((1, d), lambda i: (0, 0)),
                  pl.BlockSpec((d, LANES), lambda i: (0, 0))],
        out_specs=[row, pl.BlockSpec((tm, LANES), lambda i: (i, 0))],
        out_shape=[jax.ShapeDtypeStruct((t, d), F32), jax.ShapeDtypeStruct((t, LANES), F32)],
        compiler_params=_cparams(("parallel",)),
        name="router",
    )(h, gain.reshape(1, d), wr)


def _scatter_kernel(dest_ref, x_ref, init_hbm, o_hbm, sem, *, tm):
    del init_hbm

    def row_copy(r, kk):
        return pltpu.make_async_copy(x_ref.at[pl.ds(r, 1), :],
                                     o_hbm.at[pl.ds(dest_ref[0, 0, TOP_K * r + kk], 1), :], sem)

    def issue(r, carry):
        for kk in range(TOP_K):
            row_copy(r, kk).start(priority=kk)
        return carry

    lax.fori_loop(0, tm, issue, 0, unroll=DMA_UNROLL)

    for _ in range(TOP_K):
        pltpu.make_async_copy(x_ref, o_hbm.at[pl.ds(0, tm), :], sem).wait()


def _scatter_rows(xn, dest, n_rows):
    t, d = xn.shape
    tm = min(GATHER_TM, t)
    nt = t // tm
    dest3 = dest.reshape(nt, 1, TOP_K * tm)
    init = jnp.zeros((n_rows, d), F32)
    return pl.pallas_call(
        functools.partial(_scatter_kernel, tm=tm),
        grid=(nt,),
        in_specs=[pl.BlockSpec((1, 1, TOP_K * tm), lambda i: (i, 0, 0), memory_space=pltpu.SMEM),
                  pl.BlockSpec((tm, d), lambda i: (i, 0)),
                  pl.BlockSpec(memory_space=pl.ANY)],
        out_specs=pl.BlockSpec(memory_space=pl.ANY),
        out_shape=jax.ShapeDtypeStruct((n_rows, d), F32),
        scratch_shapes=[pltpu.SemaphoreType.DMA(())],
        input_output_aliases={2: 0},
        compiler_params=_cparams(("arbitrary",)),
        name="moe_scatter",
    )(dest3, xn, init)


def _combine_kernel(dcur_ref, dnext_ref, y_hbm, h_ref, route_ref, gain_ref, o_ref, buf, sems, *, tm):
    i = pl.program_id(0)
    n = pl.num_programs(0)
    slot = i % 2

    def row_copy(d_ref, s, r, kk):
        return pltpu.make_async_copy(y_hbm.at[pl.ds(d_ref[0, 0, TOP_K * r + kk], 1), :],
                                     buf.at[s, kk, pl.ds(r, 1), :], sems.at[s])

    def issue(d_ref, s):
        def body(r, carry):
            for kk in range(TOP_K):
                row_copy(d_ref, s, r, kk).start(priority=kk)
            return carry
        lax.fori_loop(0, tm, body, 0, unroll=DMA_UNROLL)

    @pl.when(i == 0)
    def _():
        issue(dcur_ref, slot)

    @pl.when(i + 1 < n)
    def _():
        issue(dnext_ref, 1 - slot)

    for kk in range(TOP_K):
        pltpu.make_async_copy(y_hbm.at[pl.ds(0, tm), :], buf.at[slot, kk], sems.at[slot]).wait()

    route = route_ref[...]
    lane = lax.broadcasted_iota(jnp.int32, route.shape, 1)
    g1 = jnp.sum(jnp.where(lane == 2, route, 0.0), axis=1, keepdims=True)
    g2 = jnp.sum(jnp.where(lane == 3, route, 0.0), axis=1, keepdims=True)
    out = h_ref[...] + (g1 * buf[slot, 0] + g2 * buf[slot, 1])
    o_ref[...] = _rms(out, gain_ref[...])


def _combine(y_sorted, dest, h, route, final_gain):
    t, d = h.shape
    tm = min(GATHER_TM, t)
    nt = t // tm
    dest3 = dest.reshape(nt, 1, TOP_K * tm)
    row = pl.BlockSpec((tm, d), lambda i: (i, 0))
    smem = functools.partial(pl.BlockSpec, (1, 1, TOP_K * tm), memory_space=pltpu.SMEM)
    return pl.pallas_call(
        functools.partial(_combine_kernel, tm=tm),
        grid=(nt,),
        in_specs=[smem(index_map=lambda i: (i, 0, 0)),
                  smem(index_map=lambda i: (jnp.minimum(i + 1, nt - 1), 0, 0)),
                  pl.BlockSpec(memory_space=pl.ANY),
                  row,
                  pl.BlockSpec((tm, LANES), lambda i: (i, 0)),
                  pl.BlockSpec((1, d), lambda i: (0, 0))],
        out_specs=row,
        out_shape=jax.ShapeDtypeStruct((t, d), F32),
        scratch_shapes=[pltpu.VMEM((2, TOP_K, tm, d), F32), pltpu.SemaphoreType.DMA((2,))],
        compiler_params=_cparams(("arbitrary",)),
        name="moe_combine",
    )(dest3, dest3, y_sorted, h, route, final_gain.reshape(1, d))


def _routing_plan(expert_idx, tm):
    t = expert_idx.shape[0]
    flat = expert_idx.reshape(-1)
    onehot = (flat[:, None] == jnp.arange(N_EXPERTS, dtype=jnp.int32)[None, :]).astype(jnp.int32)
    running = jnp.cumsum(onehot, axis=0)
    rank = jnp.sum((running - onehot) * onehot, axis=1)
    counts = running[-1]
    tiles = (counts + tm - 1) // tm
    tile_end = jnp.cumsum(tiles)
    group_start = (tile_end - tiles) * tm
    dest = jnp.sum(onehot * group_start[None, :], axis=1) + rank
    n_tiles = (t * TOP_K) // tm + N_EXPERTS
    n_used = tile_end[-1]
    tile_id = jnp.minimum(jnp.arange(n_tiles, dtype=jnp.int32), n_used - 1)
    tile_expert = jnp.sum((tile_id[:, None] >= tile_end[None, :]).astype(jnp.int32), axis=1)
    return dest.reshape(t, TOP_K).astype(jnp.int32), tile_expert.astype(jnp.int32), \
        n_used.reshape(1).astype(jnp.int32), n_tiles * tm


def kernel(x, attn_norm_even, w_in_even, ret_norm_even, w_out_even, ffn_norm_even, w_gate_even, w_up_even, w_down_even, attn_norm_odd, w_in_odd, b_forget_odd, w_out_odd, ffn_norm_odd, w_router_odd, w_gate_moe_odd, w_up_moe_odd, w_down_moe_odd, final_norm):
    batch, seq, d = x.shape
    t = batch * seq
    n_ret = d // (2 * HEAD_DIM)
    n_sb = d // (2 * HEAD_DIM)
    n_fox = d // HEAD_DIM
    ret_width = n_ret * HEAD_DIM
    sb_width = n_sb * HEAD_DIM
    fox_width = n_fox * HEAD_DIM
    sub = min(ATT_T, seq) // 2
    h0 = x.reshape(t, d)

    w_in = w_in_even[0]
    n_direct = 4 * ret_width + 2 * sb_width
    proj, xn0 = _norm_mm(h0, attn_norm_even[0], w_in[:, :n_direct].astype(BF16), BF16)
    v_sb_t = _mm_nt(xn0, w_in[:, n_direct:].T.astype(BF16), sub)
    y_ret = _retention(proj, ret_norm_even[0], batch, seq, n_ret)
    y_sb = _stick_breaking(proj, v_sb_t, batch, seq, n_sb, first_group=4)
    w_out = w_out_even[0].astype(BF16)
    h1, xn1 = _outproj([y_ret, y_sb], [w_out[:ret_width], w_out[ret_width:]], h0, ffn_norm_even[0])
    h2, xn2 = _ffn_dense(xn1, w_gate_even[0].astype(BF16), w_up_even[0].astype(BF16),
                         w_down_even[0].astype(BF16), h1, attn_norm_odd[0])

    w_in = w_in_odd[0]
    proj = _mm(xn2, w_in[:, :2 * fox_width].astype(BF16), BF16)
    v_fox_t = _mm_nt(xn2, w_in[:, 2 * fox_width:3 * fox_width].T.astype(BF16), sub)
    w_f = jnp.zeros((d, LANES), BF16).at[:, :n_fox].set(w_in[:, 3 * fox_width:].astype(BF16))
    f_logit = _mm(xn2, w_f, F32)
    b_f = jnp.zeros((1, LANES), F32).at[0, :n_fox].set(b_forget_odd[0])
    cum_cols = _forget_cum(f_logit, b_f, batch, seq)
    y_fox = _forgetting_attention(proj, v_fox_t, cum_cols, batch, seq, n_fox)
    h3 = _outproj([y_fox], [w_out_odd[0].astype(BF16)], h2)

    xn3, route = _router(h3, ffn_norm_odd[0], w_router_odd[0])
    expert_idx = route[:, :TOP_K].astype(jnp.int32)
    dest, tile_expert, n_used, n_rows = _routing_plan(expert_idx, FFN_TM)
    x_sorted = _scatter_rows(xn3, dest, n_rows)
    y_sorted = _ffn_grouped(x_sorted, w_gate_moe_odd[0].astype(BF16), w_up_moe_odd[0].astype(BF16),
                            w_down_moe_odd[0].astype(BF16), tile_expert, n_used)
    out = _combine(y_sorted, dest, h3, route, final_norm)
    return out.reshape(batch, seq, d)
```

```python
import functools

import jax
import jax.numpy as jnp
import numpy as np
from jax import lax
from jax.experimental import pallas as pl
from jax.experimental.pallas import tpu as pltpu

F32 = jnp.float32
BF16 = jnp.bfloat16

LANES = 128
HEAD_DIM = 64
HEADS_PER_BLOCK = LANES // HEAD_DIM
N_EXPERTS = 8
TOP_K = 2
ROPE_BASE = 10000.0
NORM_EPS = 1e-6
GROUP_NORM_EPS = 1e-5
LOG2E = 1.4426950408889634
UNDERFLOW_BITS = 160.0
FOX_MARGIN_BITS = 8.0
NORM_SLACK = 1.02
RET_CHUNK = 256
VMEM_LIMIT = 56 * 1024 * 1024

MM_TM = 1024
MM_TN = 1024
ROW_TM = 512
ATT_T = 512
ATT_COLS = 2
RET_TS = 512
FFN_TM = 256
FFN_NF = 1
GATHER_TM = 256
DMA_UNROLL = 8
N_BIAS_PIECES = 3
FOX_VALUE_ROWS = HEAD_DIM + 16


def _cparams(sem, vmem=VMEM_LIMIT):
    return pltpu.CompilerParams(dimension_semantics=sem, vmem_limit_bytes=vmem)


def _rms(xf, gain_row):
    ms = jnp.mean(xf * xf, axis=-1, keepdims=True)
    return xf * lax.rsqrt(ms + NORM_EPS) * gain_row


def _dot(a, b):
    return jnp.dot(a, b, preferred_element_type=F32)


def _dot_nt(a, b):
    return lax.dot_general(a, b, (((1,), (1,)), ((), ())), preferred_element_type=F32)


def _dot_tn(a, b):
    return lax.dot_general(a, b, (((0,), (0,)), ((), ())), preferred_element_type=F32)


def _norm_mm_kernel(x_ref, g_ref, w_ref, o_ref, xn_ref):
    @pl.when(pl.program_id(1) == 0)
    def _():
        xn_ref[...] = _rms(x_ref[...], g_ref[...]).astype(BF16)

    o_ref[...] = _dot(xn_ref[...], w_ref[...]).astype(o_ref.dtype)


def _norm_mm(x, gain, w, out_dtype):
    t, k = x.shape
    n = w.shape[1]
    tm, tn = min(MM_TM, t), min(MM_TN, n)
    return pl.pallas_call(
        _norm_mm_kernel,
        grid=(t // tm, n // tn),
        in_specs=[pl.BlockSpec((tm, k), lambda i, j: (i, 0)),
                  pl.BlockSpec((1, k), lambda i, j: (0, 0)),
                  pl.BlockSpec((k, tn), lambda i, j: (0, j))],
        out_specs=[pl.BlockSpec((tm, tn), lambda i, j: (i, j)),
                   pl.BlockSpec((tm, k), lambda i, j: (i, 0))],
        out_shape=[jax.ShapeDtypeStruct((t, n), out_dtype), jax.ShapeDtypeStruct((t, k), BF16)],
        compiler_params=_cparams(("parallel", "arbitrary")),
        name="norm_mm",
    )(x, gain.reshape(1, k), w)


def _mm_kernel(x_ref, w_ref, o_ref):
    o_ref[...] = _dot(x_ref[...], w_ref[...]).astype(o_ref.dtype)


def _mm(x, w, out_dtype):
    t, k = x.shape
    n = w.shape[1]
    tm, tn = min(MM_TM, t), min(MM_TN, n)
    return pl.pallas_call(
        _mm_kernel,
        grid=(t // tm, n // tn),
        in_specs=[pl.BlockSpec((tm, k), lambda i, j: (i, 0)),
                  pl.BlockSpec((k, tn), lambda i, j: (0, j))],
        out_specs=pl.BlockSpec((tm, tn), lambda i, j: (i, j)),
        out_shape=jax.ShapeDtypeStruct((t, n), out_dtype),
        compiler_params=_cparams(("parallel", "parallel")),
        name="mm",
    )(x, w)


def _mm_nt_kernel(x_ref, wt_ref, o_ref, *, sub):
    res = _dot_nt(wt_ref[...], x_ref[...]).astype(o_ref.dtype)
    for s in range(o_ref.shape[0]):
        o_ref[s] = res[:, s * sub:(s + 1) * sub]


def _mm_nt(x, wt, sub):
    t, k = x.shape
    n = wt.shape[0]
    tm, tn = min(MM_TM, t), min(MM_TN, n)
    return pl.pallas_call(
        functools.partial(_mm_nt_kernel, sub=sub),
        grid=(t // tm, n // tn),
        in_specs=[pl.BlockSpec((tm, k), lambda i, j: (i, 0)),
                  pl.BlockSpec((tn, k), lambda i, j: (j, 0))],
        out_specs=pl.BlockSpec((tm // sub, tn, sub), lambda i, j: (i, j, 0)),
        out_shape=jax.ShapeDtypeStruct((t // sub, n, sub), BF16),
        compiler_params=_cparams(("parallel", "parallel")),
        name="mm_nt",
    )(x, wt)


def _outproj_kernel(*refs, n_in, with_norm):
    ys = refs[:n_in]
    ws = refs[n_in:2 * n_in]
    h_ref = refs[2 * n_in]
    pos = 2 * n_in + 1
    acc = h_ref[...]
    for y_ref, w_ref in zip(ys, ws):
        acc = acc + _dot(y_ref[...], w_ref[...])
    if with_norm:
        g_ref, ho_ref, xn_ref = refs[pos], refs[pos + 1], refs[pos + 2]
        ho_ref[...] = acc
        xn_ref[...] = _rms(acc, g_ref[...]).astype(xn_ref.dtype)
    else:
        refs[pos][...] = acc


def _outproj(ys, ws, h, gain=None):
    t, d = h.shape
    tm = min(ROW_TM, t)
    n_in = len(ys)
    with_norm = gain is not None
    in_specs = [pl.BlockSpec((tm, y.shape[1]), lambda i: (i, 0)) for y in ys]
    in_specs += [pl.BlockSpec(w.shape, lambda i: (0, 0)) for w in ws]
    in_specs += [pl.BlockSpec((tm, d), lambda i: (i, 0))]
    args = list(ys) + list(ws) + [h]
    row_spec = pl.BlockSpec((tm, d), lambda i: (i, 0))
    if with_norm:
        in_specs += [pl.BlockSpec((1, d), lambda i: (0, 0))]
        args += [gain.reshape(1, d)]
        out_specs = [row_spec, row_spec]
        out_shape = [jax.ShapeDtypeStruct((t, d), F32), jax.ShapeDtypeStruct((t, d), BF16)]
    else:
        out_specs = row_spec
        out_shape = jax.ShapeDtypeStruct((t, d), F32)
    return pl.pallas_call(
        functools.partial(_outproj_kernel, n_in=n_in, with_norm=with_norm),
        grid=(t // tm,),
        in_specs=in_specs,
        out_specs=out_specs,
        out_shape=out_shape,
        compiler_params=_cparams(("parallel",)),
        name="outproj",
    )(*args)


def _retention_tables(seq, n_heads):
    half = HEAD_DIM // 2
    lane = np.arange(LANES)
    inv_freq = ROPE_BASE ** (-jnp.arange(half, dtype=F32) / half)
    ang = jnp.arange(seq, dtype=F32)[:, None] * inv_freq[None, :]
    cos, sin = jnp.cos(ang), jnp.sin(ang)
    cos_t = jnp.tile(cos, (1, LANES // half))
    sign = np.where((lane % HEAD_DIM) < half, -1.0, 1.0).astype(np.float32)
    sin_t = jnp.tile(sin, (1, LANES // half)) * sign[None, :]
    c = RET_CHUNK
    log_gamma = jnp.log(1.0 - 2.0 ** (-5.0 - jnp.arange(n_heads, dtype=F32)))
    pos = jnp.arange(c, dtype=F32)
    diff = pos[:, None] - pos[None, :]
    intra = jnp.where(diff >= 0.0,
                      jnp.exp(log_gamma[:, None, None] * jnp.maximum(diff, 0.0)), 0.0)
    intra = intra.reshape(n_heads // 2, 2, c, c)
    q_decay = jnp.exp(log_gamma[:, None] * (pos + 1.0))
    k_decay = jnp.exp(log_gamma[:, None] * (c - 1.0 - pos))
    chunk_decay = jnp.exp(log_gamma * c)

    def per_lane(tab):
        tab = tab.reshape(n_heads // 2, 2, c)
        return jnp.repeat(tab.transpose(0, 2, 1), HEAD_DIM, axis=2)

    head_of = lane // HEAD_DIM
    same = (head_of[:, None] == head_of[None, :]).astype(np.float32)
    cd = chunk_decay.reshape(n_heads // 2, 2)
    cd_rows = jnp.repeat(cd, HEAD_DIM, axis=1)
    state_decay = cd_rows[:, :, None] * same[None]
    return cos_t, sin_t, intra, per_lane(q_decay), per_lane(k_decay), state_decay, jnp.asarray(same)


def _retention_kernel(q_ref, k_ref, v_ref, g_ref, cos_ref, sin_ref, intra_ref, qd_ref, kd_ref,
                      sd_ref, same_ref, rn_ref, o_ref, state_ref, *, ts):
    @pl.when(pl.program_id(2) == 0)
    def _():
        state_ref[...] = jnp.zeros_like(state_ref)

    c = RET_CHUNK
    lane = lax.broadcasted_iota(jnp.int32, (1, LANES), 1)
    first_half = (lane % HEAD_DIM) < (HEAD_DIM // 2)
    head0 = lane < HEAD_DIM

    def rot(t, cos, sin):
        swapped = jnp.where(first_half, pltpu.roll(t, LANES - HEAD_DIM // 2, 1),
                            pltpu.roll(t, HEAD_DIM // 2, 1))
        return t * cos + swapped * sin

    for ci in range(ts // c):
        rows = slice(ci * c, (ci + 1) * c)
        cos, sin = cos_ref[rows, :], sin_ref[rows, :]
        q = rot(q_ref[rows, :].astype(F32), cos, sin)
        k = rot(k_ref[rows, :].astype(F32), cos, sin) * (HEAD_DIM ** -0.5)
        v = v_ref[rows, :]
        kb = k.astype(BF16)
        inner = []
        for hd in range(HEADS_PER_BLOCK):
            hmask = head0 if hd == 0 else jnp.logical_not(head0)
            qh = jnp.where(hmask, q, 0.0).astype(BF16)
            scores = _dot_nt(qh, kb) * intra_ref[hd]
            inner.append(_dot(scores.astype(BF16), v))
        state = state_ref[...]
        cross = _dot((q * qd_ref[...]).astype(BF16), state.astype(BF16))
        y = jnp.where(head0, inner[0], inner[1]) + cross
        kv = _dot_tn((k * kd_ref[...]).astype(BF16), v)
        state_ref[...] = state * sd_ref[...] + kv * same_ref[...]

        s0 = jnp.sum(jnp.where(head0, y, 0.0), axis=1, keepdims=True)
        s1 = jnp.sum(jnp.where(head0, 0.0, y), axis=1, keepdims=True)
        d = y - jnp.where(head0, s0, s1) * (1.0 / HEAD_DIM)
        dd = d * d
        v0 = jnp.sum(jnp.where(head0, dd, 0.0), axis=1, keepdims=True)
        v1 = jnp.sum(jnp.where(head0, 0.0, dd), axis=1, keepdims=True)
        var = jnp.where(head0, v0, v1) * (1.0 / HEAD_DIM)
        g = g_ref[rows, :].astype(F32)
        silu = g * (1.0 / (1.0 + jnp.exp(-g)))
        o_ref[rows, :] = (d * lax.rsqrt(var + GROUP_NORM_EPS) * rn_ref[...] * silu).astype(o_ref.dtype)


def _retention(proj, ret_norm, batch, seq, n_heads):
    t = proj.shape[0]
    width = n_heads * HEAD_DIM
    nb = width // LANES
    ts = min(RET_TS, seq)
    ns = seq // ts
    tabs = _retention_tables(seq, n_heads)
    cos_t, sin_t, intra, qd, kd, sd, same = tabs
    c = RET_CHUNK

    def col(group):
        return pl.BlockSpec((ts, LANES), lambda b, hp, si: (b * ns + si, group * nb + hp))

    in_specs = [col(0), col(1), col(2), col(3),
                pl.BlockSpec((ts, LANES), lambda b, hp, si: (si, 0)),
                pl.BlockSpec((ts, LANES), lambda b, hp, si: (si, 0)),
                pl.BlockSpec((None, 2, c, c), lambda b, hp, si: (hp, 0, 0, 0)),
                pl.BlockSpec((None, c, LANES), lambda b, hp, si: (hp, 0, 0)),
                pl.BlockSpec((None, c, LANES), lambda b, hp, si: (hp, 0, 0)),
                pl.BlockSpec((None, LANES, LANES), lambda b, hp, si: (hp, 0, 0)),
                pl.BlockSpec((LANES, LANES), lambda b, hp, si: (0, 0)),
                pl.BlockSpec((1, LANES), lambda b, hp, si: (0, hp))]
    return pl.pallas_call(
        functools.partial(_retention_kernel, ts=ts),
        grid=(batch, nb, ns),
        in_specs=in_specs,
        out_specs=pl.BlockSpec((ts, LANES), lambda b, hp, si: (b * ns + si, hp)),
        out_shape=jax.ShapeDtypeStruct((t, width), BF16),
        scratch_shapes=[pltpu.VMEM((LANES, LANES), F32)],
        compiler_params=_cparams(("parallel", "parallel", "arbitrary")),
        name="retention",
    )(proj, proj, proj, proj, cos_t, sin_t, intra, qd, kd, sd, same, ret_norm.reshape(1, width))


def _head_masks():
    lane = lax.broadcasted_iota(jnp.int32, (1, LANES), 1)
    head0 = lane < HEAD_DIM
    return lane, [head0, jnp.logical_not(head0)]


def _col_block(x, hd):
    cb = hd // HEADS_PER_BLOCK
    return x[:, cb * LANES:(cb + 1) * LANES]


def _head_rows(x, hd):
    return x[hd * HEAD_DIM:(hd + 1) * HEAD_DIM]


def _store_heads(o_ref, acc_t):
    for cb in range(ATT_COLS):
        pair = jnp.concatenate(acc_t[HEADS_PER_BLOCK * cb:HEADS_PER_BLOCK * (cb + 1)], axis=0)
        o_ref[:, cb * LANES:(cb + 1) * LANES] = pair.T.astype(o_ref.dtype)


def _two_stage_blocks(qi, sub, scores_to, apply_from, carry, upper_diag_first, rest_is_zero=None):
    top = 2 * qi + 1
    first, second = ((top, sub), (top - 1, 0)) if upper_diag_first else ((top - 1, 0), (top, sub))
    scores_to(0, *first)
    scores_to(1, *second)
    carry = apply_from(0, first[0], carry)

    def more(state):
        i, carry = state
        if rest_is_zero is None:
            return i < qi
        return jnp.logical_and(i < qi, jnp.logical_not(rest_is_zero(carry, top - 2 - 2 * i)))

    def pair(state):
        i, carry = state
        jb = top - 2 - 2 * i
        scores_to(0, jb, None)
        carry = apply_from(1, jnp.where(i == 0, second[0], jb + 1), carry)
        scores_to(1, jb - 1, None)
        return i + 1, apply_from(0, jb, carry)

    n_pairs, carry = lax.while_loop(more, pair, (jnp.int32(0), carry))
    last = jnp.where(n_pairs == 0, second[0], top - 1 - 2 * n_pairs)
    if rest_is_zero is None:
        return apply_from(1, last, carry)
    skip = jnp.logical_and(n_pairs > 0, rest_is_zero(carry, last))
    return lax.cond(skip, lambda c: c, lambda c: apply_from(1, last, c), carry)


def _sb_kernel(q_ref, k_ref, vt_ref, o_ref, d0_ref, d1_ref, tot0_ref, tot1_ref, *, tq, sub):
    qi = pl.program_id(2)
    d_refs = (d0_ref, d1_ref)
    tot_refs = (tot0_ref, tot1_ref)
    _, hmasks = _head_masks()
    key = lax.broadcasted_iota(jnp.int32, (sub, tq), 0)
    qry = lax.broadcasted_iota(jnp.int32, (sub, tq), 1)
    r = lax.broadcasted_iota(jnp.int32, (sub, sub), 0)
    c = lax.broadcasted_iota(jnp.int32, (sub, sub), 1)
    suffix = jnp.where(c >= r, 1.0, 0.0).astype(BF16)
    qf = q_ref[...].astype(F32) * (HEAD_DIM ** -0.5 * LOG2E)
    heads = range(ATT_COLS * HEADS_PER_BLOCK)
    qh = [jnp.where(hmasks[hd % HEADS_PER_BLOCK], _col_block(qf, hd), 0.0).astype(BF16) for hd in heads]

    sign_bit = jnp.uint32(0x80000000)

    def scores_to(buf, jb, diag_off):
        k = k_ref[jb]
        for hd in heads:
            z = _dot_nt(_col_block(k, hd), qh[hd])
            if diag_off is not None:
                z = jnp.where((key + diag_off) < qry, z, -jnp.inf)
            neg_abs = lax.bitcast_convert_type(lax.bitcast_convert_type(z, jnp.uint32) | sign_bit, F32)
            fail = jnp.maximum(z, 0.0) + jnp.log2(1.0 + jnp.exp2(neg_abs))
            tail = _dot(suffix, fail.astype(BF16))
            d_refs[buf][hd] = jnp.minimum(z - tail, 0.0)
            tot_refs[buf][hd] = tail[0:1, :]

    def apply_from(buf, jb, carry):
        vt = vt_ref[jb]
        out = []
        for hd in heads:
            later, acc = carry[hd]
            w = jnp.exp2(d_refs[buf][hd] - later)
            out.append((later + tot_refs[buf][hd], acc + _dot(_head_rows(vt, hd), w.astype(BF16))))
        return out

    def rest_is_zero(carry, _):
        least = carry[0][0]
        for hd in heads[1:]:
            least = jnp.minimum(least, carry[hd][0])
        return jnp.min(least) > UNDERFLOW_BITS

    carry = [(jnp.zeros((1, tq), F32), jnp.zeros((HEAD_DIM, tq), F32)) for _ in heads]
    carry = _two_stage_blocks(qi, sub, scores_to, apply_from, carry, upper_diag_first=True,
                              rest_is_zero=rest_is_zero)
    _store_heads(o_ref, [c[1] for c in carry])


def _stick_breaking(proj, v_t, batch, seq, n_heads, first_group):
    t = proj.shape[0]
    width = n_heads * HEAD_DIM
    nb = width // LANES
    tq = min(ATT_T, seq)
    sub = tq // 2
    nq = seq // tq
    nk = seq // sub
    assert v_t.shape[2] == sub
    proj_k = proj.reshape(t // sub, sub, proj.shape[1])
    n_step = ATT_COLS * HEADS_PER_BLOCK
    cols = ATT_COLS * LANES
    ng = nb // ATT_COLS
    score_buf = pltpu.VMEM((n_step, sub, tq), F32)
    total_buf = pltpu.VMEM((n_step, 1, tq), F32)
    q_spec = pl.BlockSpec((tq, cols), lambda b, hp, qi: (b * nq + qi, first_group * ng + hp))
    k_spec = pl.BlockSpec((nk, sub, cols), lambda b, hp, qi: (b, 0, (first_group + 1) * ng + hp))
    v_spec = pl.BlockSpec((nk, cols, sub), lambda b, hp, qi: (b, hp, 0))
    return pl.pallas_call(
        functools.partial(_sb_kernel, tq=tq, sub=sub),
        grid=(batch, ng, nq),
        in_specs=[q_spec, k_spec, v_spec],
        out_specs=pl.BlockSpec((tq, cols), lambda b, hp, qi: (b * nq + qi, hp)),
        out_shape=jax.ShapeDtypeStruct((t, width), BF16),
        scratch_shapes=[score_buf, score_buf, total_buf, total_buf],
        compiler_params=_cparams(("parallel", "parallel", "arbitrary")),
        name="stick_breaking",
    )(proj, proj_k, v_t)


def _forget_cum_kernel(f_ref, b_ref, col_ref, *, seq):
    x = f_ref[...] + b_ref[...]
    log_f = jnp.minimum(x, 0.0) - jnp.log(1.0 + jnp.exp(-jnp.abs(x)))
    xt = log_f.T
    pos = lax.broadcasted_iota(jnp.int32, xt.shape, 1)
    shift = 1
    while shift < seq:
        xt = xt + jnp.where(pos >= shift, pltpu.roll(xt, shift, 1), 0.0)
        shift *= 2
    col_ref[...] = xt.T


def _forget_cum(f_logit, b_forget, batch, seq):
    t = f_logit.shape[0]
    return pl.pallas_call(
        functools.partial(_forget_cum_kernel, seq=seq),
        grid=(batch,),
        in_specs=[pl.BlockSpec((seq, LANES), lambda b: (b, 0)),
                  pl.BlockSpec((1, LANES), lambda b: (0, 0))],
        out_specs=pl.BlockSpec((seq, LANES), lambda b: (b, 0)),
        out_shape=jax.ShapeDtypeStruct((t, LANES), F32),
        compiler_params=_cparams(("parallel",)),
        name="forget_cum",
    )(f_logit, b_forget)


def _bias_lanes(hd):
    return HEAD_DIM * (1 - hd)


def _bias_tiles(cum, hp, is_query):
    n_step = ATT_COLS * HEADS_PER_BLOCK
    pieces = []
    rest = cum * LOG2E
    for _ in range(N_BIAS_PIECES):
        p = rest.astype(BF16)
        pieces.append(p)
        rest = rest - p.astype(F32)
    stacked = jnp.concatenate(pieces, axis=1)
    row = lax.broadcasted_iota(jnp.int32, (N_BIAS_PIECES * LANES, 1), 0)
    piece = row >> 7
    head = (row & (LANES - 1)) - n_step * hp
    first = 0 if is_query else N_BIAS_PIECES
    target = (head >> 1) * LANES + (1 - (head & 1)) * HEAD_DIM + first + piece
    target = jnp.where(jnp.logical_and(head >= 0, head < n_step), target, -1)
    col = lax.broadcasted_iota(jnp.int32, (1, ATT_COLS * LANES), 1)
    selector = jnp.where(col == target, 1.0 if is_query else -1.0, 0.0).astype(BF16)
    tile = _dot(stacked, selector)
    ones_first = N_BIAS_PIECES if is_query else 0
    in_half = col & (HEAD_DIM - 1)
    ones = jnp.logical_and(in_half >= ones_first, in_half < ones_first + N_BIAS_PIECES)
    return jnp.where(ones, 1.0, tile)


def _fox_kernel(q_ref, k_ref, vt_ref, cq_ref, ck_ref, o_ref, kp_ref, vp_ref, ksq_ref, s0_ref, s1_ref,
                max0_ref, max1_ref, *, tq, sub, seq):
    hp = pl.program_id(1)
    qi = pl.program_id(2)
    s_refs = (s0_ref, s1_ref)
    max_refs = (max0_ref, max1_ref)
    nk = seq // sub
    lane, hmasks = _head_masks()
    heads = range(ATT_COLS * HEADS_PER_BLOCK)

    def with_bias_lanes(x, bias_tile, hd):
        return jnp.where(hmasks[hd % HEADS_PER_BLOCK], _col_block(x, hd), _col_block(bias_tile, hd)).astype(BF16)

    @pl.when(qi == 0)
    def _():
        kf = k_ref[...].astype(F32)
        k_bias = _bias_tiles(ck_ref[...], hp, False)
        extra = lax.broadcasted_iota(jnp.int32, (FOX_VALUE_ROWS - HEAD_DIM, sub), 0)
        ones = jnp.where(extra == 0, 1.0, 0.0).astype(BF16)
        for hd in heads:
            side = hd % HEADS_PER_BLOCK
            k_hd = _col_block(kf, hd)
            kp_ref[hd] = with_bias_lanes(kf, k_bias, hd)
            k_sq = jnp.sum(jnp.where(hmasks[side], k_hd * k_hd, 0.0), axis=1, keepdims=True)
            ksq_ref[hd] = jnp.broadcast_to(jnp.max(k_sq, axis=0, keepdims=True), (1, LANES))
            for jb in range(nk):
                vp_ref[hd, jb, 0:HEAD_DIM] = _head_rows(vt_ref[jb], hd)
                vp_ref[hd, jb, HEAD_DIM:FOX_VALUE_ROWS] = ones

    qf = q_ref[...].astype(F32) * (HEAD_DIM ** -0.5 * LOG2E)
    q_bias = _bias_tiles(cq_ref[...], hp, True)
    qh = [with_bias_lanes(qf, q_bias, hd) for hd in heads]
    key = lax.broadcasted_iota(jnp.int32, (sub, tq), 0)
    qry = lax.broadcasted_iota(jnp.int32, (sub, tq), 1)

    pick = lax.broadcasted_iota(jnp.int32, (8, LANES), 0)
    pick_lane = lax.broadcasted_iota(jnp.int32, (8, LANES), 1)
    head_rows = jnp.where(pick == pick_lane // HEAD_DIM, 1.0, 0.0).astype(BF16)
    reach = []
    for hd in heads:
        side = hd % HEADS_PER_BLOCK
        q_hd = _col_block(qf, hd)
        q_sq = _dot_nt(head_rows, (q_hd * q_hd).astype(BF16))[side:side + 1]
        spare = _bias_lanes(side)
        bias_row = jnp.where((pick_lane >= spare) & (pick_lane < spare + N_BIAS_PIECES), 1.0, 0.0).astype(BF16)
        cq_row = _dot_nt(bias_row, qh[hd])[0:1]
        reach.append(NORM_SLACK * jnp.sqrt(q_sq * ksq_ref[hd][:, 0:1]) + cq_row)

    def scores_to(buf, jb, diag_off):
        ks = pl.multiple_of(jb * sub, sub)
        for hd in heads:
            s = _dot_nt(kp_ref[hd, pl.ds(ks, sub), :], qh[hd])
            if diag_off is not None:
                s = jnp.where((key + diag_off) <= qry, s, -jnp.inf)
            s_refs[buf][hd] = s
            max_refs[buf][hd] = jnp.max(s, axis=0, keepdims=True)

    def apply_from(buf, jb, carry):
        out = []
        for hd in heads:
            m, acc = carry[hd]
            m_new = jnp.maximum(m, max_refs[buf][hd])
            p = jnp.exp2(s_refs[buf][hd] - m_new).astype(BF16)
            out.append((m_new, jnp.exp2(m - m_new) * acc + _dot(vp_ref[hd, jb], p)))
        return out

    def rest_is_zero(carry, jb):
        ck_row = ck_ref[pl.ds(jnp.maximum((jb + 1) * sub - 1, 0), 1), :]
        worst = None
        for hd in heads:
            sel = lane == ATT_COLS * HEADS_PER_BLOCK * hp + hd
            ck_last = jnp.sum(jnp.where(sel, ck_row, 0.0), axis=1, keepdims=True) * LOG2E
            gap = reach[hd] - ck_last - carry[hd][0]
            worst = gap if worst is None else jnp.maximum(worst, gap)
        return jnp.max(worst) < -(UNDERFLOW_BITS + FOX_MARGIN_BITS)

    carry = [(jnp.full((1, tq), -jnp.inf, F32), jnp.zeros((FOX_VALUE_ROWS, tq), F32)) for _ in heads]
    carry = _two_stage_blocks(qi, sub, scores_to, apply_from, carry, upper_diag_first=False,
                              rest_is_zero=rest_is_zero)
    normed = []
    for hd in heads:
        acc = carry[hd][1]
        normed.append(acc[0:HEAD_DIM] * (1.0 / acc[HEAD_DIM:HEAD_DIM + 1, :]))
    _store_heads(o_ref, normed)


def _forgetting_attention(proj, v_t, cum_cols, batch, seq, n_heads):
    t = proj.shape[0]
    width = n_heads * HEAD_DIM
    nb = width // LANES
    tq = min(ATT_T, seq)
    sub = tq // 2
    nq = seq // tq
    nk = seq // sub
    assert v_t.shape[2] == sub
    n_step = ATT_COLS * HEADS_PER_BLOCK
    cols = ATT_COLS * LANES
    ng = nb // ATT_COLS
    score_buf = pltpu.VMEM((n_step, sub, tq), F32)
    max_buf = pltpu.VMEM((n_step, 1, tq), F32)
    q_spec = pl.BlockSpec((tq, cols), lambda b, hp, qi: (b * nq + qi, hp))
    k_spec = pl.BlockSpec((seq, cols), lambda b, hp, qi: (b, ng + hp))
    v_spec = pl.BlockSpec((nk, cols, sub), lambda b, hp, qi: (b, hp, 0))
    cq_spec = pl.BlockSpec((tq, LANES), lambda b, hp, qi: (b * nq + qi, 0))
    ck_spec = pl.BlockSpec((seq, LANES), lambda b, hp, qi: (b, 0))
    return pl.pallas_call(
        functools.partial(_fox_kernel, tq=tq, sub=sub, seq=seq),
        grid=(batch, ng, nq),
        in_specs=[q_spec, k_spec, v_spec, cq_spec, ck_spec],
        out_specs=pl.BlockSpec((tq, cols), lambda b, hp, qi: (b * nq + qi, hp)),
        out_shape=jax.ShapeDtypeStruct((t, width), BF16),
        scratch_shapes=[pltpu.VMEM((n_step, seq, LANES), BF16),
                        pltpu.VMEM((n_step, nk, FOX_VALUE_ROWS, sub), BF16),
                        pltpu.VMEM((n_step, 1, LANES), F32),
                        score_buf, score_buf, max_buf, max_buf],
        compiler_params=_cparams(("parallel", "parallel", "arbitrary")),
        name="forgetting_attention",
    )(proj, proj, v_t, cum_cols, cum_cols)


def _swiglu_partial(x, wg_ref, wu_ref, wd_ref):
    g = _dot(x, wg_ref[...])
    u = _dot(x, wu_ref[...])
    a = g * (1.0 / (1.0 + jnp.exp(-g))) * u
    return _dot(a.astype(BF16), wd_ref[...])


def _ffn_dense_kernel(x_ref, wg_ref, wu_ref, wd_ref, h_ref, gain_ref, ho_ref, xn_ref, acc_ref):
    f = pl.program_id(1)

    @pl.when(f == 0)
    def _():
        acc_ref[...] = h_ref[...]

    acc_ref[...] += _swiglu_partial(x_ref[...], wg_ref, wu_ref, wd_ref)

    @pl.when(f == pl.num_programs(1) - 1)
    def _():
        h_new = acc_ref[...]
        ho_ref[...] = h_new
        xn_ref[...] = _rms(h_new, gain_ref[...]).astype(xn_ref.dtype)


def _ffn_dense(xn, wg, wu, wd, h, next_gain):
    t, d = h.shape
    dff = wg.shape[1]
    tm = min(FFN_TM, t)
    tf = dff // FFN_NF
    row = pl.BlockSpec((tm, d), lambda i, f: (i, 0))
    return pl.pallas_call(
        _ffn_dense_kernel,
        grid=(t // tm, FFN_NF),
        in_specs=[row,
                  pl.BlockSpec((d, tf), lambda i, f: (0, f)),
                  pl.BlockSpec((d, tf), lambda i, f: (0, f)),
                  pl.BlockSpec((tf, d), lambda i, f: (f, 0)),
                  row,
                  pl.BlockSpec((1, d), lambda i, f: (0, 0))],
        out_specs=[row, row],
        out_shape=[jax.ShapeDtypeStruct((t, d), F32), jax.ShapeDtypeStruct((t, d), BF16)],
        scratch_shapes=[pltpu.VMEM((tm, d), F32)],
        compiler_params=_cparams(("parallel", "arbitrary")),
        name="ffn_dense",
    )(xn, wg, wu, wd, h, next_gain.reshape(1, d))


def _ffn_grouped_kernel(te_ref, nu_ref, x_ref, wg_ref, wu_ref, wd_ref, o_ref, acc_ref):
    i = pl.program_id(0)
    f = pl.program_id(1)
    used = i < nu_ref[0]

    @pl.when(jnp.logical_and(used, f == 0))
    def _():
        acc_ref[...] = jnp.zeros_like(acc_ref)

    @pl.when(used)
    def _():
        acc_ref[...] += _swiglu_partial(x_ref[...].astype(BF16), wg_ref, wu_ref, wd_ref)

    @pl.when(f == pl.num_programs(1) - 1)
    def _():
        @pl.when(used)
        def _():
            o_ref[...] = acc_ref[...]

        @pl.when(jnp.logical_not(used))
        def _():
            o_ref[...] = jnp.zeros_like(o_ref)


def _ffn_grouped(x_sorted, wg, wu, wd, tile_expert, n_used):
    r, d = x_sorted.shape
    dff = wg.shape[2]
    tm = FFN_TM
    tf = dff // FFN_NF
    grid_spec = pltpu.PrefetchScalarGridSpec(
        num_scalar_prefetch=2,
        grid=(r // tm, FFN_NF),
        in_specs=[pl.BlockSpec((tm, d), lambda i, f, te, nu: (i, 0)),
                  pl.BlockSpec((None, d, tf), lambda i, f, te, nu: (te[i], 0, f)),
                  pl.BlockSpec((None, d, tf), lambda i, f, te, nu: (te[i], 0, f)),
                  pl.BlockSpec((None, tf, d), lambda i, f, te, nu: (te[i], f, 0))],
        out_specs=pl.BlockSpec((tm, d), lambda i, f, te, nu: (i, 0)),
        scratch_shapes=[pltpu.VMEM((tm, d), F32)],
    )
    return pl.pallas_call(
        _ffn_grouped_kernel,
        grid_spec=grid_spec,
        out_shape=jax.ShapeDtypeStruct((r, d), F32),
        compiler_params=_cparams(("arbitrary", "arbitrary")),
        name="ffn_grouped",
    )(tile_expert, n_used, x_sorted, wg, wu, wd)


def _router_kernel(h_ref, gain_ref, wr_ref, xn_ref, route_ref):
    xn = _rms(h_ref[...], gain_ref[...])
    xn_ref[...] = xn
    logits = jnp.dot(xn, wr_ref[...], precision=lax.Precision.HIGHEST, preferred_element_type=F32)
    lane = lax.broadcasted_iota(jnp.int32, logits.shape, 1)
    lane_f = lane.astype(F32)
    lg = jnp.where(lane < N_EXPERTS, logits, -jnp.inf)
    v1 = jnp.max(lg, axis=1, keepdims=True)
    i1 = jnp.min(jnp.where(lg == v1, lane_f, float(LANES)), axis=1, keepdims=True)
    lg2 = jnp.where(lane_f == i1, -jnp.inf, lg)
    v2 = jnp.max(lg2, axis=1, keepdims=True)
    i2 = jnp.min(jnp.where(lg2 == v2, lane_f, float(LANES)), axis=1, keepdims=True)
    e2 = jnp.exp(v2 - v1)
    g1 = 1.0 / (1.0 + e2)
    g2 = e2 / (1.0 + e2)
    route_ref[...] = jnp.where(lane == 0, i1, jnp.where(lane == 1, i2,
                               jnp.where(lane == 2, g1, jnp.where(lane == 3, g2, 0.0))))


def _router(h, gain, w_router):
    t, d = h.shape
    tm = min(ROW_TM, t)
    wr = jnp.zeros((d, LANES), F32).at[:, :N_EXPERTS].set(w_router)
    row = pl.BlockSpec((tm, d), lambda i: (i, 0))
    return pl.pallas_call(
        _router_kernel,
        grid=(t // tm,),
        in_specs=[row, pl.BlockSpec((1, d), lambda i: (0, 0)),
                  pl.BlockSpec((d, LANES), lambda i: (0, 0))],
        out_specs=[row, pl.BlockSpec((tm, LANES), lambda i: (i, 0))],
        out_shape=[jax.ShapeDtypeStruct((t, d), F32), jax.ShapeDtypeStruct((t, LANES), F32)],
        compiler_params=_cparams(("parallel",)),
        name="router",
    )(h, gain.reshape(1, d), wr)


def _scatter_kernel(dest_ref, x_ref, init_hbm, o_hbm, sem, *, tm):
    del init_hbm

    def row_copy(r, kk):
        return pltpu.make_async_copy(x_ref.at[pl.ds(r, 1), :],
                                     o_hbm.at[pl.ds(dest_ref[0, 0, TOP_K * r + kk], 1), :], sem)

    def issue(r, carry):
        for kk in range(TOP_K):
            row_copy(r, kk).start(priority=kk)
        return carry

    lax.fori_loop(0, tm, issue, 0, unroll=DMA_UNROLL)

    for _ in range(TOP_K):
        pltpu.make_async_copy(x_ref, o_hbm.at[pl.ds(0, tm), :], sem).wait()


def _scatter_rows(xn, dest, n_rows):
    t, d = xn.shape
    tm = min(GATHER_TM, t)
    nt = t // tm
    dest3 = dest.reshape(nt, 1, TOP_K * tm)
    init = jnp.zeros((n_rows, d), F32)
    return pl.pallas_call(
        functools.partial(_scatter_kernel, tm=tm),
        grid=(nt,),
        in_specs=[pl.BlockSpec((1, 1, TOP_K * tm), lambda i: (i, 0, 0), memory_space=pltpu.SMEM),
                  pl.BlockSpec((tm, d), lambda i: (i, 0)),
                  pl.BlockSpec(memory_space=pl.ANY)],
        out_specs=pl.BlockSpec(memory_space=pl.ANY),
        out_shape=jax.ShapeDtypeStruct((n_rows, d), F32),
        scratch_shapes=[pltpu.SemaphoreType.DMA(())],
        input_output_aliases={2: 0},
        compiler_params=_cparams(("arbitrary",)),
        name="moe_scatter",
    )(dest3, xn, init)


def _combine_kernel(dcur_ref, dnext_ref, y_hbm, h_ref, route_ref, gain_ref, o_ref, buf, sems, *, tm):
    i = pl.program_id(0)
    n = pl.num_programs(0)
    slot = i % 2

    def row_copy(d_ref, s, r, kk):
        return pltpu.make_async_copy(y_hbm.at[pl.ds(d_ref[0, 0, TOP_K * r + kk], 1), :],
                                     buf.at[s, kk, pl.ds(r, 1), :], sems.at[s])

    def issue(d_ref, s):
        def body(r, carry):
            for kk in range(TOP_K):
                row_copy(d_ref, s, r, kk).start(priority=kk)
            return carry
        lax.fori_loop(0, tm, body, 0, unroll=DMA_UNROLL)

    @pl.when(i == 0)
    def _():
        issue(dcur_ref, slot)

    @pl.when(i + 1 < n)
    def _():
        issue(dnext_ref, 1 - slot)

    for kk in range(TOP_K):
        pltpu.make_async_copy(y_hbm.at[pl.ds(0, tm), :], buf.at[slot, kk], sems.at[slot]).wait()

    route = route_ref[...]
    lane = lax.broadcasted_iota(jnp.int32, route.shape, 1)
    g1 = jnp.sum(jnp.where(lane == 2, route, 0.0), axis=1, keepdims=True)
    g2 = jnp.sum(jnp.where(lane == 3, route, 0.0), axis=1, keepdims=True)
    out = h_ref[...] + (g1 * buf[slot, 0] + g2 * buf[slot, 1])
    o_ref[...] = _rms(out, gain_ref[...])


def _combine(y_sorted, dest, h, route, final_gain):
    t, d = h.shape
    tm = min(GATHER_TM, t)
    nt = t // tm
    dest3 = dest.reshape(nt, 1, TOP_K * tm)
    row = pl.BlockSpec((tm, d), lambda i: (i, 0))
    smem = functools.partial(pl.BlockSpec, (1, 1, TOP_K * tm), memory_space=pltpu.SMEM)
    return pl.pallas_call(
        functools.partial(_combine_kernel, tm=tm),
        grid=(nt,),
        in_specs=[smem(index_map=lambda i: (i, 0, 0)),
                  smem(index_map=lambda i: (jnp.minimum(i + 1, nt - 1), 0, 0)),
                  pl.BlockSpec(memory_space=pl.ANY),
                  row,
                  pl.BlockSpec((tm, LANES), lambda i: (i, 0)),
                  pl.BlockSpec((1, d), lambda i: (0, 0))],
        out_specs=row,
        out_shape=jax.ShapeDtypeStruct((t, d), F32),
        scratch_shapes=[pltpu.VMEM((2, TOP_K, tm, d), F32), pltpu.SemaphoreType.DMA((2,))],
        compiler_params=_cparams(("arbitrary",)),
        name="moe_combine",
    )(dest3, dest3, y_sorted, h, route, final_gain.reshape(1, d))


def _routing_plan(expert_idx, tm):
    t = expert_idx.shape[0]
    flat = expert_idx.reshape(-1)
    onehot = (flat[:, None] == jnp.arange(N_EXPERTS, dtype=jnp.int32)[None, :]).astype(jnp.int32)
    running = jnp.cumsum(onehot, axis=0)
    rank = jnp.sum((running - onehot) * onehot, axis=1)
    counts = running[-1]
    tiles = (counts + tm - 1) // tm
    tile_end = jnp.cumsum(tiles)
    group_start = (tile_end - tiles) * tm
    dest = jnp.sum(onehot * group_start[None, :], axis=1) + rank
    n_tiles = (t * TOP_K) // tm + N_EXPERTS
    n_used = tile_end[-1]
    tile_id = jnp.minimum(jnp.arange(n_tiles, dtype=jnp.int32), n_used - 1)
    tile_expert = jnp.sum((tile_id[:, None] >= tile_end[None, :]).astype(jnp.int32), axis=1)
    return dest.reshape(t, TOP_K).astype(jnp.int32), tile_expert.astype(jnp.int32), \
        n_used.reshape(1).astype(jnp.int32), n_tiles * tm


def kernel(x, attn_norm_even, w_in_even, ret_norm_even, w_out_even, ffn_norm_even, w_gate_even, w_up_even, w_down_even, attn_norm_odd, w_in_odd, b_forget_odd, w_out_odd, ffn_norm_odd, w_router_odd, w_gate_moe_odd, w_up_moe_odd, w_down_moe_odd, final_norm):
    batch, seq, d = x.shape
    t = batch * seq
    n_ret = d // (2 * HEAD_DIM)
    n_sb = d // (2 * HEAD_DIM)
    n_fox = d // HEAD_DIM
    ret_width = n_ret * HEAD_DIM
    sb_width = n_sb * HEAD_DIM
    fox_width = n_fox * HEAD_DIM
    sub = min(ATT_T, seq) // 2
    h0 = x.reshape(t, d)

    w_in = w_in_even[0]
    n_direct = 4 * ret_width + 2 * sb_width
    proj, xn0 = _norm_mm(h0, attn_norm_even[0], w_in[:, :n_direct].astype(BF16), BF16)
    v_sb_t = _mm_nt(xn0, w_in[:, n_direct:].T.astype(BF16), sub)
    y_ret = _retention(proj, ret_norm_even[0], batch, seq, n_ret)
    y_sb = _stick_breaking(proj, v_sb_t, batch, seq, n_sb, first_group=4)
    w_out = w_out_even[0].astype(BF16)
    h1, xn1 = _outproj([y_ret, y_sb], [w_out[:ret_width], w_out[ret_width:]], h0, ffn_norm_even[0])
    h2, xn2 = _ffn_dense(xn1, w_gate_even[0].astype(BF16), w_up_even[0].astype(BF16),
                         w_down_even[0].astype(BF16), h1, attn_norm_odd[0])

    w_in = w_in_odd[0]
    proj = _mm(xn2, w_in[:, :2 * fox_width].astype(BF16), BF16)
    v_fox_t = _mm_nt(xn2, w_in[:, 2 * fox_width:3 * fox_width].T.astype(BF16), sub)
    w_f = jnp.zeros((d, LANES), BF16).at[:, :n_fox].set(w_in[:, 3 * fox_width:].astype(BF16))
    f_logit = _mm(xn2, w_f, F32)
    b_f = jnp.zeros((1, LANES), F32).at[0, :n_fox].set(b_forget_odd[0])
    cum_cols = _forget_cum(f_logit, b_f, batch, seq)
    y_fox = _forgetting_attention(proj, v_fox_t, cum_cols, batch, seq, n_fox)
    h3 = _outproj([y_fox], [w_out_odd[0].astype(BF16)], h2)

    xn3, route = _router(h3, ffn_norm_odd[0], w_router_odd[0])
    expert_idx = route[:, :TOP_K].astype(jnp.int32)
    dest, tile_expert, n_used, n_rows = _routing_plan(expert_idx, FFN_TM)
    x_sorted = _scatter_rows(xn3, dest, n_rows)
    y_sorted = _ffn_grouped(x_sorted, w_gate_moe_odd[0].astype(BF16), w_up_moe_odd[0].astype(BF16),
                            w_down_moe_odd[0].astype(BF16), tile_expert, n_used)
    out = _combine(y_sorted, dest, h3, route, final_norm)
    return out.reshape(batch, seq, d)
```

```python
import functools

import jax
import jax.numpy as jnp
import numpy as np
from jax import lax
from jax.experimental import pallas as pl
from jax.experimental.pallas import tpu as pltpu

F32 = jnp.float32
BF16 = jnp.bfloat16

LANES = 128
HEAD_DIM = 64
HEADS_PER_BLOCK = LANES // HEAD_DIM
N_EXPERTS = 8
TOP_K = 2
ROPE_BASE = 10000.0
NORM_EPS = 1e-6
GROUP_NORM_EPS = 1e-5
LOG2E = 1.4426950408889634
UNDERFLOW_BITS = 160.0
FOX_MARGIN_BITS = 8.0
NORM_SLACK = 1.02
RET_CHUNK = 256
VMEM_LIMIT = 56 * 1024 * 1024

MM_TM = 1024
MM_TN = 1024
ROW_TM = 512
ATT_T = 512
ATT_COLS = 2
RET_TS = 512
FFN_TM = 256
FFN_NF = 1
GATHER_TM = 512
DMA_UNROLL = 8
N_BIAS_PIECES = 3
FOX_VALUE_ROWS = HEAD_DIM + 16


def _cparams(sem, vmem=VMEM_LIMIT):
    return pltpu.CompilerParams(dimension_semantics=sem, vmem_limit_bytes=vmem)


def _rms(xf, gain_row):
    ms = jnp.mean(xf * xf, axis=-1, keepdims=True)
    return xf * lax.rsqrt(ms + NORM_EPS) * gain_row


def _dot(a, b):
    return jnp.dot(a, b, preferred_element_type=F32)


def _dot_nt(a, b):
    return lax.dot_general(a, b, (((1,), (1,)), ((), ())), preferred_element_type=F32)


def _dot_tn(a, b):
    return lax.dot_general(a, b, (((0,), (0,)), ((), ())), preferred_element_type=F32)


def _norm_mm_kernel(x_ref, g_ref, w_ref, o_ref, xn_ref):
    @pl.when(pl.program_id(1) == 0)
    def _():
        xn_ref[...] = _rms(x_ref[...], g_ref[...]).astype(BF16)

    o_ref[...] = _dot(xn_ref[...], w_ref[...]).astype(o_ref.dtype)


def _norm_mm(x, gain, w, out_dtype):
    t, k = x.shape
    n = w.shape[1]
    tm, tn = min(MM_TM, t), min(MM_TN, n)
    return pl.pallas_call(
        _norm_mm_kernel,
        grid=(t // tm, n // tn),
        in_specs=[pl.BlockSpec((tm, k), lambda i, j: (i, 0)),
                  pl.BlockSpec((1, k), lambda i, j: (0, 0)),
                  pl.BlockSpec((k, tn), lambda i, j: (0, j))],
        out_specs=[pl.BlockSpec((tm, tn), lambda i, j: (i, j)),
                   pl.BlockSpec((tm, k), lambda i, j: (i, 0))],
        out_shape=[jax.ShapeDtypeStruct((t, n), out_dtype), jax.ShapeDtypeStruct((t, k), BF16)],
        compiler_params=_cparams(("parallel", "arbitrary")),
        name="norm_mm",
    )(x, gain.reshape(1, k), w)


def _mm_kernel(x_ref, w_ref, o_ref):
    o_ref[...] = _dot(x_ref[...], w_ref[...]).astype(o_ref.dtype)


def _mm(x, w, out_dtype):
    t, k = x.shape
    n = w.shape[1]
    tm, tn = min(MM_TM, t), min(MM_TN, n)
    return pl.pallas_call(
        _mm_kernel,
        grid=(t // tm, n // tn),
        in_specs=[pl.BlockSpec((tm, k), lambda i, j: (i, 0)),
                  pl.BlockSpec((k, tn), lambda i, j: (0, j))],
        out_specs=pl.BlockSpec((tm, tn), lambda i, j: (i, j)),
        out_shape=jax.ShapeDtypeStruct((t, n), out_dtype),
        compiler_params=_cparams(("parallel", "parallel")),
        name="mm",
    )(x, w)


def _mm_nt_kernel(x_ref, wt_ref, o_ref, *, sub):
    res = _dot_nt(wt_ref[...], x_ref[...]).astype(o_ref.dtype)
    for s in range(o_ref.shape[0]):
        o_ref[s] = res[:, s * sub:(s + 1) * sub]


def _mm_nt(x, wt, sub):
    t, k = x.shape
    n = wt.shape[0]
    tm, tn = min(MM_TM, t), min(MM_TN, n)
    return pl.pallas_call(
        functools.partial(_mm_nt_kernel, sub=sub),
        grid=(t // tm, n // tn),
        in_specs=[pl.BlockSpec((tm, k), lambda i, j: (i, 0)),
                  pl.BlockSpec((tn, k), lambda i, j: (j, 0))],
        out_specs=pl.BlockSpec((tm // sub, tn, sub), lambda i, j: (i, j, 0)),
        out_shape=jax.ShapeDtypeStruct((t // sub, n, sub), BF16),
        compiler_params=_cparams(("parallel", "parallel")),
        name="mm_nt",
    )(x, wt)


def _outproj_kernel(*refs, n_in, with_norm):
    ys = refs[:n_in]
    ws = refs[n_in:2 * n_in]
    h_ref = refs[2 * n_in]
    pos = 2 * n_in + 1
    acc = h_ref[...]
    for y_ref, w_ref in zip(ys, ws):
        acc = acc + _dot(y_ref[...], w_ref[...])
    if with_norm:
        g_ref, ho_ref, xn_ref = refs[pos], refs[pos + 1], refs[pos + 2]
        ho_ref[...] = acc
        xn_ref[...] = _rms(acc, g_ref[...]).astype(xn_ref.dtype)
    else:
        refs[pos][...] = acc


def _outproj(ys, ws, h, gain=None):
    t, d = h.shape
    tm = min(ROW_TM, t)
    n_in = len(ys)
    with_norm = gain is not None
    in_specs = [pl.BlockSpec((tm, y.shape[1]), lambda i: (i, 0)) for y in ys]
    in_specs += [pl.BlockSpec(w.shape, lambda i: (0, 0)) for w in ws]
    in_specs += [pl.BlockSpec((tm, d), lambda i: (i, 0))]
    args = list(ys) + list(ws) + [h]
    row_spec = pl.BlockSpec((tm, d), lambda i: (i, 0))
    if with_norm:
        in_specs += [pl.BlockSpec((1, d), lambda i: (0, 0))]
        args += [gain.reshape(1, d)]
        out_specs = [row_spec, row_spec]
        out_shape = [jax.ShapeDtypeStruct((t, d), F32), jax.ShapeDtypeStruct((t, d), BF16)]
    else:
        out_specs = row_spec
        out_shape = jax.ShapeDtypeStruct((t, d), F32)
    return pl.pallas_call(
        functools.partial(_outproj_kernel, n_in=n_in, with_norm=with_norm),
        grid=(t // tm,),
        in_specs=in_specs,
        out_specs=out_specs,
        out_shape=out_shape,
        compiler_params=_cparams(("parallel",)),
        name="outproj",
    )(*args)


def _retention_tables(seq, n_heads):
    half = HEAD_DIM // 2
    lane = np.arange(LANES)
    inv_freq = ROPE_BASE ** (-jnp.arange(half, dtype=F32) / half)
    ang = jnp.arange(seq, dtype=F32)[:, None] * inv_freq[None, :]
    cos, sin = jnp.cos(ang), jnp.sin(ang)
    cos_t = jnp.tile(cos, (1, LANES // half))
    sign = np.where((lane % HEAD_DIM) < half, -1.0, 1.0).astype(np.float32)
    sin_t = jnp.tile(sin, (1, LANES // half)) * sign[None, :]
    c = RET_CHUNK
    log_gamma = jnp.log(1.0 - 2.0 ** (-5.0 - jnp.arange(n_heads, dtype=F32)))
    pos = jnp.arange(c, dtype=F32)
    diff = pos[:, None] - pos[None, :]
    intra = jnp.where(diff >= 0.0,
                      jnp.exp(log_gamma[:, None, None] * jnp.maximum(diff, 0.0)), 0.0)
    intra = intra.reshape(n_heads // 2, 2, c, c)
    q_decay = jnp.exp(log_gamma[:, None] * (pos + 1.0))
    k_decay = jnp.exp(log_gamma[:, None] * (c - 1.0 - pos))
    chunk_decay = jnp.exp(log_gamma * c)

    def per_lane(tab):
        tab = tab.reshape(n_heads // 2, 2, c)
        return jnp.repeat(tab.transpose(0, 2, 1), HEAD_DIM, axis=2)

    head_of = lane // HEAD_DIM
    same = (head_of[:, None] == head_of[None, :]).astype(np.float32)
    cd = chunk_decay.reshape(n_heads // 2, 2)
    cd_rows = jnp.repeat(cd, HEAD_DIM, axis=1)
    state_decay = cd_rows[:, :, None] * same[None]
    return cos_t, sin_t, intra, per_lane(q_decay), per_lane(k_decay), state_decay, jnp.asarray(same)


def _retention_kernel(q_ref, k_ref, v_ref, g_ref, cos_ref, sin_ref, intra_ref, qd_ref, kd_ref,
                      sd_ref, same_ref, rn_ref, o_ref, state_ref, *, ts):
    @pl.when(pl.program_id(2) == 0)
    def _():
        state_ref[...] = jnp.zeros_like(state_ref)

    c = RET_CHUNK
    lane = lax.broadcasted_iota(jnp.int32, (1, LANES), 1)
    first_half = (lane % HEAD_DIM) < (HEAD_DIM // 2)
    head0 = lane < HEAD_DIM

    def rot(t, cos, sin):
        swapped = jnp.where(first_half, pltpu.roll(t, LANES - HEAD_DIM // 2, 1),
                            pltpu.roll(t, HEAD_DIM // 2, 1))
        return t * cos + swapped * sin

    for ci in range(ts // c):
        rows = slice(ci * c, (ci + 1) * c)
        cos, sin = cos_ref[rows, :], sin_ref[rows, :]
        q = rot(q_ref[rows, :].astype(F32), cos, sin)
        k = rot(k_ref[rows, :].astype(F32), cos, sin) * (HEAD_DIM ** -0.5)
        v = v_ref[rows, :]
        kb = k.astype(BF16)
        inner = []
        for hd in range(HEADS_PER_BLOCK):
            hmask = head0 if hd == 0 else jnp.logical_not(head0)
            qh = jnp.where(hmask, q, 0.0).astype(BF16)
            scores = _dot_nt(qh, kb) * intra_ref[hd]
            inner.append(_dot(scores.astype(BF16), v))
        state = state_ref[...]
        cross = _dot((q * qd_ref[...]).astype(BF16), state.astype(BF16))
        y = jnp.where(head0, inner[0], inner[1]) + cross
        kv = _dot_tn((k * kd_ref[...]).astype(BF16), v)
        state_ref[...] = state * sd_ref[...] + kv * same_ref[...]

        s0 = jnp.sum(jnp.where(head0, y, 0.0), axis=1, keepdims=True)
        s1 = jnp.sum(jnp.where(head0, 0.0, y), axis=1, keepdims=True)
        d = y - jnp.where(head0, s0, s1) * (1.0 / HEAD_DIM)
        dd = d * d
        v0 = jnp.sum(jnp.where(head0, dd, 0.0), axis=1, keepdims=True)
        v1 = jnp.sum(jnp.where(head0, 0.0, dd), axis=1, keepdims=True)
        var = jnp.where(head0, v0, v1) * (1.0 / HEAD_DIM)
        g = g_ref[rows, :].astype(F32)
        silu = g * (1.0 / (1.0 + jnp.exp(-g)))
        o_ref[rows, :] = (d * lax.rsqrt(var + GROUP_NORM_EPS) * rn_ref[...] * silu).astype(o_ref.dtype)


def _retention(proj, ret_norm, batch, seq, n_heads):
    t = proj.shape[0]
    width = n_heads * HEAD_DIM
    nb = width // LANES
    ts = min(RET_TS, seq)
    ns = seq // ts
    tabs = _retention_tables(seq, n_heads)
    cos_t, sin_t, intra, qd, kd, sd, same = tabs
    c = RET_CHUNK

    def col(group):
        return pl.BlockSpec((ts, LANES), lambda b, hp, si: (b * ns + si, group * nb + hp))

    in_specs = [col(0), col(1), col(2), col(3),
                pl.BlockSpec((ts, LANES), lambda b, hp, si: (si, 0)),
                pl.BlockSpec((ts, LANES), lambda b, hp, si: (si, 0)),
                pl.BlockSpec((None, 2, c, c), lambda b, hp, si: (hp, 0, 0, 0)),
                pl.BlockSpec((None, c, LANES), lambda b, hp, si: (hp, 0, 0)),
                pl.BlockSpec((None, c, LANES), lambda b, hp, si: (hp, 0, 0)),
                pl.BlockSpec((None, LANES, LANES), lambda b, hp, si: (hp, 0, 0)),
                pl.BlockSpec((LANES, LANES), lambda b, hp, si: (0, 0)),
                pl.BlockSpec((1, LANES), lambda b, hp, si: (0, hp))]
    return pl.pallas_call(
        functools.partial(_retention_kernel, ts=ts),
        grid=(batch, nb, ns),
        in_specs=in_specs,
        out_specs=pl.BlockSpec((ts, LANES), lambda b, hp, si: (b * ns + si, hp)),
        out_shape=jax.ShapeDtypeStruct((t, width), BF16),
        scratch_shapes=[pltpu.VMEM((LANES, LANES), F32)],
        compiler_params=_cparams(("parallel", "parallel", "arbitrary")),
        name="retention",
    )(proj, proj, proj, proj, cos_t, sin_t, intra, qd, kd, sd, same, ret_norm.reshape(1, width))


def _head_masks():
    lane = lax.broadcasted_iota(jnp.int32, (1, LANES), 1)
    head0 = lane < HEAD_DIM
    return lane, [head0, jnp.logical_not(head0)]


def _col_block(x, hd):
    cb = hd // HEADS_PER_BLOCK
    return x[:, cb * LANES:(cb + 1) * LANES]


def _head_rows(x, hd):
    return x[hd * HEAD_DIM:(hd + 1) * HEAD_DIM]


def _store_heads(o_ref, acc_t):
    for cb in range(ATT_COLS):
        pair = jnp.concatenate(acc_t[HEADS_PER_BLOCK * cb:HEADS_PER_BLOCK * (cb + 1)], axis=0)
        o_ref[:, cb * LANES:(cb + 1) * LANES] = pair.T.astype(o_ref.dtype)


def _two_stage_blocks(qi, sub, scores_to, apply_from, carry, upper_diag_first, rest_is_zero=None,
                      stop_per_block=False):
    top = 2 * qi + 1
    first, second = ((top, sub), (top - 1, 0)) if upper_diag_first else ((top - 1, 0), (top, sub))
    scores_to(0, *first)
    scores_to(1, *second)
    carry = apply_from(0, first[0], carry)

    if stop_per_block:
        def more_blocks(state):
            n, carry = state
            return jnp.logical_and(n < 2 * qi, jnp.logical_not(rest_is_zero(carry, top - 2 - n)))

        def one_block(state):
            n, carry = state
            jb = top - 2 - n
            prev = jnp.where(n == 0, second[0], jb + 1)

            def fill(buf):
                def run(c):
                    scores_to(buf, jb, None)
                    return apply_from(1 - buf, prev, c)
                return run

            return n + 1, lax.cond((n & 1) == 0, fill(0), fill(1), carry)

        n, carry = lax.while_loop(more_blocks, one_block, (jnp.int32(0), carry))
        last = jnp.where(n == 0, second[0], top - 1 - n)
        skip = jnp.logical_and(n > 0, rest_is_zero(carry, last))

        def finish(c):
            return lax.cond((n & 1) == 0, lambda c: apply_from(1, last, c), lambda c: apply_from(0, last, c), c)

        return lax.cond(skip, lambda c: c, finish, carry)

    def more(state):
        i, carry = state
        if rest_is_zero is None:
            return i < qi
        return jnp.logical_and(i < qi, jnp.logical_not(rest_is_zero(carry, top - 2 - 2 * i)))

    def pair(state):
        i, carry = state
        jb = top - 2 - 2 * i
        scores_to(0, jb, None)
        carry = apply_from(1, jnp.where(i == 0, second[0], jb + 1), carry)
        scores_to(1, jb - 1, None)
        return i + 1, apply_from(0, jb, carry)

    n_pairs, carry = lax.while_loop(more, pair, (jnp.int32(0), carry))
    last = jnp.where(n_pairs == 0, second[0], top - 1 - 2 * n_pairs)
    if rest_is_zero is None:
        return apply_from(1, last, carry)
    skip = jnp.logical_and(n_pairs > 0, rest_is_zero(carry, last))
    return lax.cond(skip, lambda c: c, lambda c: apply_from(1, last, c), carry)


def _sb_kernel(q_ref, k_ref, vt_ref, o_ref, d0_ref, d1_ref, tot0_ref, tot1_ref, *, tq, sub):
    qi = pl.program_id(2)
    d_refs = (d0_ref, d1_ref)
    tot_refs = (tot0_ref, tot1_ref)
    _, hmasks = _head_masks()
    key = lax.broadcasted_iota(jnp.int32, (sub, tq), 0)
    qry = lax.broadcasted_iota(jnp.int32, (sub, tq), 1)
    r = lax.broadcasted_iota(jnp.int32, (sub, sub), 0)
    c = lax.broadcasted_iota(jnp.int32, (sub, sub), 1)
    suffix = jnp.where(c >= r, 1.0, 0.0).astype(BF16)
    qf = q_ref[...].astype(F32) * (HEAD_DIM ** -0.5 * LOG2E)
    heads = range(ATT_COLS * HEADS_PER_BLOCK)
    qh = [jnp.where(hmasks[hd % HEADS_PER_BLOCK], _col_block(qf, hd), 0.0).astype(BF16) for hd in heads]

    sign_bit = jnp.uint32(0x80000000)

    def scores_to(buf, jb, diag_off):
        k = k_ref[jb]
        for hd in heads:
            z = _dot_nt(_col_block(k, hd), qh[hd])
            if diag_off is not None:
                z = jnp.where((key + diag_off) < qry, z, -jnp.inf)
            neg_abs = lax.bitcast_convert_type(lax.bitcast_convert_type(z, jnp.uint32) | sign_bit, F32)
            fail = jnp.maximum(z, 0.0) + jnp.log2(1.0 + jnp.exp2(neg_abs))
            tail = _dot(suffix, fail.astype(BF16))
            d_refs[buf][hd] = jnp.minimum(z - tail, 0.0)
            tot_refs[buf][hd] = tail[0:1, :]

    def apply_from(buf, jb, carry):
        vt = vt_ref[jb]
        out = []
        for hd in heads:
            later, acc = carry[hd]
            w = jnp.exp2(d_refs[buf][hd] - later)
            out.append((later + tot_refs[buf][hd], acc + _dot(_head_rows(vt, hd), w.astype(BF16))))
        return out

    def rest_is_zero(carry, _):
        least = carry[0][0]
        for hd in heads[1:]:
            least = jnp.minimum(least, carry[hd][0])
        return jnp.min(least) > UNDERFLOW_BITS

    carry = [(jnp.zeros((1, tq), F32), jnp.zeros((HEAD_DIM, tq), F32)) for _ in heads]
    carry = _two_stage_blocks(qi, sub, scores_to, apply_from, carry, upper_diag_first=True,
                              rest_is_zero=rest_is_zero, stop_per_block=True)
    _store_heads(o_ref, [c[1] for c in carry])


def _stick_breaking(proj, v_t, batch, seq, n_heads, first_group):
    t = proj.shape[0]
    width = n_heads * HEAD_DIM
    nb = width // LANES
    tq = min(ATT_T, seq)
    sub = tq // 2
    nq = seq // tq
    nk = seq // sub
    assert v_t.shape[2] == sub
    proj_k = proj.reshape(t // sub, sub, proj.shape[1])
    n_step = ATT_COLS * HEADS_PER_BLOCK
    cols = ATT_COLS * LANES
    ng = nb // ATT_COLS
    score_buf = pltpu.VMEM((n_step, sub, tq), F32)
    total_buf = pltpu.VMEM((n_step, 1, tq), F32)
    q_spec = pl.BlockSpec((tq, cols), lambda b, hp, qi: (b * nq + qi, first_group * ng + hp))
    k_spec = pl.BlockSpec((nk, sub, cols), lambda b, hp, qi: (b, 0, (first_group + 1) * ng + hp))
    v_spec = pl.BlockSpec((nk, cols, sub), lambda b, hp, qi: (b, hp, 0))
    return pl.pallas_call(
        functools.partial(_sb_kernel, tq=tq, sub=sub),
        grid=(batch, ng, nq),
        in_specs=[q_spec, k_spec, v_spec],
        out_specs=pl.BlockSpec((tq, cols), lambda b, hp, qi: (b * nq + qi, hp)),
        out_shape=jax.ShapeDtypeStruct((t, width), BF16),
        scratch_shapes=[score_buf, score_buf, total_buf, total_buf],
        compiler_params=_cparams(("parallel", "parallel", "arbitrary")),
        name="stick_breaking",
    )(proj, proj_k, v_t)


def _forget_cum_kernel(f_ref, b_ref, col_ref, *, seq):
    x = f_ref[...] + b_ref[...]
    log_f = jnp.minimum(x, 0.0) - jnp.log(1.0 + jnp.exp(-jnp.abs(x)))
    xt = log_f.T
    pos = lax.broadcasted_iota(jnp.int32, xt.shape, 1)
    shift = 1
    while shift < seq:
        xt = xt + jnp.where(pos >= shift, pltpu.roll(xt, shift, 1), 0.0)
        shift *= 2
    col_ref[...] = xt.T


def _forget_cum(f_logit, b_forget, batch, seq):
    t = f_logit.shape[0]
    return pl.pallas_call(
        functools.partial(_forget_cum_kernel, seq=seq),
        grid=(batch,),
        in_specs=[pl.BlockSpec((seq, LANES), lambda b: (b, 0)),
                  pl.BlockSpec((1, LANES), lambda b: (0, 0))],
        out_specs=pl.BlockSpec((seq, LANES), lambda b: (b, 0)),
        out_shape=jax.ShapeDtypeStruct((t, LANES), F32),
        compiler_params=_cparams(("parallel",)),
        name="forget_cum",
    )(f_logit, b_forget)


def _bias_lanes(hd):
    return HEAD_DIM * (1 - hd)


def _bias_tiles(cum, hp, is_query):
    n_step = ATT_COLS * HEADS_PER_BLOCK
    pieces = []
    rest = cum * LOG2E
    for _ in range(N_BIAS_PIECES):
        p = rest.astype(BF16)
        pieces.append(p)
        rest = rest - p.astype(F32)
    stacked = jnp.concatenate(pieces, axis=1)
    row = lax.broadcasted_iota(jnp.int32, (N_BIAS_PIECES * LANES, 1), 0)
    piece = row >> 7
    head = (row & (LANES - 1)) - n_step * hp
    first = 0 if is_query else N_BIAS_PIECES
    target = (head >> 1) * LANES + (1 - (head & 1)) * HEAD_DIM + first + piece
    target = jnp.where(jnp.logical_and(head >= 0, head < n_step), target, -1)
    col = lax.broadcasted_iota(jnp.int32, (1, ATT_COLS * LANES), 1)
    selector = jnp.where(col == target, 1.0 if is_query else -1.0, 0.0).astype(BF16)
    tile = _dot(stacked, selector)
    ones_first = N_BIAS_PIECES if is_query else 0
    in_half = col & (HEAD_DIM - 1)
    ones = jnp.logical_and(in_half >= ones_first, in_half < ones_first + N_BIAS_PIECES)
    return jnp.where(ones, 1.0, tile)


def _fox_kernel(q_ref, k_ref, vt_ref, cq_ref, ck_ref, o_ref, kp_ref, vp_ref, ksq_ref, s0_ref, s1_ref,
                max0_ref, max1_ref, *, tq, sub, seq):
    hp = pl.program_id(1)
    qi = pl.program_id(2)
    s_refs = (s0_ref, s1_ref)
    max_refs = (max0_ref, max1_ref)
    nk = seq // sub
    lane, hmasks = _head_masks()
    heads = range(ATT_COLS * HEADS_PER_BLOCK)

    def with_bias_lanes(x, bias_tile, hd):
        return jnp.where(hmasks[hd % HEADS_PER_BLOCK], _col_block(x, hd), _col_block(bias_tile, hd)).astype(BF16)

    @pl.when(qi == 0)
    def _():
        kf = k_ref[...].astype(F32)
        k_bias = _bias_tiles(ck_ref[...], hp, False)
        extra = lax.broadcasted_iota(jnp.int32, (FOX_VALUE_ROWS - HEAD_DIM, sub), 0)
        ones = jnp.where(extra == 0, 1.0, 0.0).astype(BF16)
        for hd in heads:
            side = hd % HEADS_PER_BLOCK
            k_hd = _col_block(kf, hd)
            kp_ref[hd] = with_bias_lanes(kf, k_bias, hd)
            k_sq = jnp.sum(jnp.where(hmasks[side], k_hd * k_hd, 0.0), axis=1, keepdims=True)
            ksq_ref[hd] = jnp.broadcast_to(jnp.max(k_sq, axis=0, keepdims=True), (1, LANES))
            for jb in range(nk):
                vp_ref[hd, jb, 0:HEAD_DIM] = _head_rows(vt_ref[jb], hd)
                vp_ref[hd, jb, HEAD_DIM:FOX_VALUE_ROWS] = ones

    qf = q_ref[...].astype(F32) * (HEAD_DIM ** -0.5 * LOG2E)
    q_bias = _bias_tiles(cq_ref[...], hp, True)
    qh = [with_bias_lanes(qf, q_bias, hd) for hd in heads]
    key = lax.broadcasted_iota(jnp.int32, (sub, tq), 0)
    qry = lax.broadcasted_iota(jnp.int32, (sub, tq), 1)

    pick = lax.broadcasted_iota(jnp.int32, (8, LANES), 0)
    pick_lane = lax.broadcasted_iota(jnp.int32, (8, LANES), 1)
    head_rows = jnp.where(pick == pick_lane // HEAD_DIM, 1.0, 0.0).astype(BF16)
    reach = []
    for hd in heads:
        side = hd % HEADS_PER_BLOCK
        q_hd = _col_block(qf, hd)
        q_sq = _dot_nt(head_rows, (q_hd * q_hd).astype(BF16))[side:side + 1]
        spare = _bias_lanes(side)
        bias_row = jnp.where((pick_lane >= spare) & (pick_lane < spare + N_BIAS_PIECES), 1.0, 0.0).astype(BF16)
        cq_row = _dot_nt(bias_row, qh[hd])[0:1]
        reach.append(NORM_SLACK * jnp.sqrt(q_sq * ksq_ref[hd][:, 0:1]) + cq_row)

    def scores_to(buf, jb, diag_off):
        ks = pl.multiple_of(jb * sub, sub)
        for hd in heads:
            s = _dot_nt(kp_ref[hd, pl.ds(ks, sub), :], qh[hd])
            if diag_off is not None:
                s = jnp.where((key + diag_off) <= qry, s, -jnp.inf)
            s_refs[buf][hd] = s
            max_refs[buf][hd] = jnp.max(s, axis=0, keepdims=True)

    def apply_from(buf, jb, carry):
        out = []
        for hd in heads:
            m, acc = carry[hd]
            m_new = jnp.maximum(m, max_refs[buf][hd])
            p = jnp.exp2(s_refs[buf][hd] - m_new).astype(BF16)
            out.append((m_new, jnp.exp2(m - m_new) * acc + _dot(vp_ref[hd, jb], p)))
        return out

    def rest_is_zero(carry, jb):
        ck_row = ck_ref[pl.ds(jnp.maximum((jb + 1) * sub - 1, 0), 1), :]
        worst = None
        for hd in heads:
            sel = lane == ATT_COLS * HEADS_PER_BLOCK * hp + hd
            ck_last = jnp.sum(jnp.where(sel, ck_row, 0.0), axis=1, keepdims=True) * LOG2E
            gap = reach[hd] - ck_last - carry[hd][0]
            worst = gap if worst is None else jnp.maximum(worst, gap)
        return jnp.max(worst) < -(UNDERFLOW_BITS + FOX_MARGIN_BITS)

    carry = [(jnp.full((1, tq), -jnp.inf, F32), jnp.zeros((FOX_VALUE_ROWS, tq), F32)) for _ in heads]
    carry = _two_stage_blocks(qi, sub, scores_to, apply_from, carry, upper_diag_first=False,
                              rest_is_zero=rest_is_zero)
    normed = []
    for hd in heads:
        acc = carry[hd][1]
        normed.append(acc[0:HEAD_DIM] * (1.0 / acc[HEAD_DIM:HEAD_DIM + 1, :]))
    _store_heads(o_ref, normed)


def _forgetting_attention(proj, v_t, cum_cols, batch, seq, n_heads):
    t = proj.shape[0]
    width = n_heads * HEAD_DIM
    nb = width // LANES
    tq = min(ATT_T, seq)
    sub = tq // 2
    nq = seq // tq
    nk = seq // sub
    assert v_t.shape[2] == sub
    n_step = ATT_COLS * HEADS_PER_BLOCK
    cols = ATT_COLS * LANES
    ng = nb // ATT_COLS
    score_buf = pltpu.VMEM((n_step, sub, tq), F32)
    max_buf = pltpu.VMEM((n_step, 1, tq), F32)
    q_spec = pl.BlockSpec((tq, cols), lambda b, hp, qi: (b * nq + qi, hp))
    k_spec = pl.BlockSpec((seq, cols), lambda b, hp, qi: (b, ng + hp))
    v_spec = pl.BlockSpec((nk, cols, sub), lambda b, hp, qi: (b, hp, 0))
    cq_spec = pl.BlockSpec((tq, LANES), lambda b, hp, qi: (b * nq + qi, 0))
    ck_spec = pl.BlockSpec((seq, LANES), lambda b, hp, qi: (b, 0))
    return pl.pallas_call(
        functools.partial(_fox_kernel, tq=tq, sub=sub, seq=seq),
        grid=(batch, ng, nq),
        in_specs=[q_spec, k_spec, v_spec, cq_spec, ck_spec],
        out_specs=pl.BlockSpec((tq, cols), lambda b, hp, qi: (b * nq + qi, hp)),
        out_shape=jax.ShapeDtypeStruct((t, width), BF16),
        scratch_shapes=[pltpu.VMEM((n_step, seq, LANES), BF16),
                        pltpu.VMEM((n_step, nk, FOX_VALUE_ROWS, sub), BF16),
                        pltpu.VMEM((n_step, 1, LANES), F32),
                        score_buf, score_buf, max_buf, max_buf],
        compiler_params=_cparams(("parallel", "parallel", "arbitrary")),
        name="forgetting_attention",
    )(proj, proj, v_t, cum_cols, cum_cols)


def _swiglu_partial(x, wg_ref, wu_ref, wd_ref):
    g = _dot(x, wg_ref[...])
    u = _dot(x, wu_ref[...])
    a = g * (1.0 / (1.0 + jnp.exp(-g))) * u
    return _dot(a.astype(BF16), wd_ref[...])


def _ffn_dense_kernel(x_ref, wg_ref, wu_ref, wd_ref, h_ref, gain_ref, ho_ref, xn_ref, acc_ref):
    f = pl.program_id(1)

    @pl.when(f == 0)
    def _():
        acc_ref[...] = h_ref[...]

    acc_ref[...] += _swiglu_partial(x_ref[...], wg_ref, wu_ref, wd_ref)

    @pl.when(f == pl.num_programs(1) - 1)
    def _():
        h_new = acc_ref[...]
        ho_ref[...] = h_new
        xn_ref[...] = _rms(h_new, gain_ref[...]).astype(xn_ref.dtype)


def _ffn_dense(xn, wg, wu, wd, h, next_gain):
    t, d = h.shape
    dff = wg.shape[1]
    tm = min(FFN_TM, t)
    tf = dff // FFN_NF
    row = pl.BlockSpec((tm, d), lambda i, f: (i, 0))
    return pl.pallas_call(
        _ffn_dense_kernel,
        grid=(t // tm, FFN_NF),
        in_specs=[row,
                  pl.BlockSpec((d, tf), lambda i, f: (0, f)),
                  pl.BlockSpec((d, tf), lambda i, f: (0, f)),
                  pl.BlockSpec((tf, d), lambda i, f: (f, 0)),
                  row,
                  pl.BlockSpec((1, d), lambda i, f: (0, 0))],
        out_specs=[row, row],
        out_shape=[jax.ShapeDtypeStruct((t, d), F32), jax.ShapeDtypeStruct((t, d), BF16)],
        scratch_shapes=[pltpu.VMEM((tm, d), F32)],
        compiler_params=_cparams(("parallel", "arbitrary")),
        name="ffn_dense",
    )(xn, wg, wu, wd, h, next_gain.reshape(1, d))


def _ffn_grouped_kernel(te_ref, nu_ref, x_ref, wg_ref, wu_ref, wd_ref, o_ref, acc_ref):
    i = pl.program_id(0)
    f = pl.program_id(1)
    used = i < nu_ref[0]

    @pl.when(jnp.logical_and(used, f == 0))
    def _():
        acc_ref[...] = jnp.zeros_like(acc_ref)

    @pl.when(used)
    def _():
        acc_ref[...] += _swiglu_partial(x_ref[...].astype(BF16), wg_ref, wu_ref, wd_ref)

    @pl.when(f == pl.num_programs(1) - 1)
    def _():
        @pl.when(used)
        def _():
            o_ref[...] = acc_ref[...]

        @pl.when(jnp.logical_not(used))
        def _():
            o_ref[...] = jnp.zeros_like(o_ref)


def _ffn_grouped(x_sorted, wg, wu, wd, tile_expert, n_used):
    r, d = x_sorted.shape
    dff = wg.shape[2]
    tm = FFN_TM
    tf = dff // FFN_NF
    grid_spec = pltpu.PrefetchScalarGridSpec(
        num_scalar_prefetch=2,
        grid=(r // tm, FFN_NF),
        in_specs=[pl.BlockSpec((tm, d), lambda i, f, te, nu: (i, 0)),
                  pl.BlockSpec((None, d, tf), lambda i, f, te, nu: (te[i], 0, f)),
                  pl.BlockSpec((None, d, tf), lambda i, f, te, nu: (te[i], 0, f)),
                  pl.BlockSpec((None, tf, d), lambda i, f, te, nu: (te[i], f, 0))],
        out_specs=pl.BlockSpec((tm, d), lambda i, f, te, nu: (i, 0)),
        scratch_shapes=[pltpu.VMEM((tm, d), F32)],
    )
    return pl.pallas_call(
        _ffn_grouped_kernel,
        grid_spec=grid_spec,
        out_shape=jax.ShapeDtypeStruct((r, d), F32),
        compiler_params=_cparams(("arbitrary", "arbitrary")),
        name="ffn_grouped",
    )(tile_expert, n_used, x_sorted, wg, wu, wd)


def _router_kernel(h_ref, gain_ref, wr_ref, xn_ref, route_ref):
    xn = _rms(h_ref[...], gain_ref[...])
    xn_ref[...] = xn
    logits = jnp.dot(xn, wr_ref[...], precision=lax.Precision.HIGHEST, preferred_element_type=F32)
    lane = lax.broadcasted_iota(jnp.int32, logits.shape, 1)
    lane_f = lane.astype(F32)
    lg = jnp.where(lane < N_EXPERTS, logits, -jnp.inf)
    v1 = jnp.max(lg, axis=1, keepdims=True)
    i1 = jnp.min(jnp.where(lg == v1, lane_f, float(LANES)), axis=1, keepdims=True)
    lg2 = jnp.where(lane_f == i1, -jnp.inf, lg)
    v2 = jnp.max(lg2, axis=1, keepdims=True)
    i2 = jnp.min(jnp.where(lg2 == v2, lane_f, float(LANES)), axis=1, keepdims=True)
    e2 = jnp.exp(v2 - v1)
    g1 = 1.0 / (1.0 + e2)
    g2 = e2 / (1.0 + e2)
    route_ref[...] = jnp.where(lane == 0, i1, jnp.where(lane == 1, i2,
                               jnp.where(lane == 2, g1, jnp.where(lane == 3, g2, 0.0))))


def _router(h, gain, w_router):
    t, d = h.shape
    tm = min(ROW_TM, t)
    wr = jnp.zeros((d, LANES), F32).at[:, :N_EXPERTS].set(w_router)
    row = pl.BlockSpec((tm, d), lambda i: (i, 0))
    return pl.pallas_call(
        _router_kernel,
        grid=(t // tm,),
        in_specs=[row, pl.BlockSpec((1, d), lambda i: (0, 0)),
                  pl.BlockSpec((d, LANES), lambda i: (0, 0))],
        out_specs=[row, pl.BlockSpec((tm, LANES), lambda i: (i, 0))],
        out_shape=[jax.ShapeDtypeStruct((t, d), F32), jax.ShapeDtypeStruct((t, LANES), F32)],
        compiler_params=_cparams(("parallel",)),
        name="router",
    )(h, gain.reshape(1, d), wr)


def _scatter_kernel(dest_ref, x_ref, init_hbm, o_hbm, sem, *, tm):
    del init_hbm

    def row_copy(r, kk):
        return pltpu.make_async_copy(x_ref.at[pl.ds(r, 1), :],
                                     o_hbm.at[pl.ds(dest_ref[0, 0, TOP_K * r + kk], 1), :], sem)

    def issue(r, carry):
        for kk in range(TOP_K):
            row_copy(r, kk).start(priority=kk)
        return carry

    lax.fori_loop(0, tm, issue, 0, unroll=DMA_UNROLL)

    for _ in range(TOP_K):
        pltpu.make_async_copy(x_ref, o_hbm.at[pl.ds(0, tm), :], sem).wait()


def _scatter_rows(xn, dest, n_rows):
    t, d = xn.shape
    tm = min(GATHER_TM, t)
    nt = t // tm
    dest3 = dest.reshape(nt, 1, TOP_K * tm)
    init = jnp.zeros((n_rows, d), F32)
    return pl.pallas_call(
        functools.partial(_scatter_kernel, tm=tm),
        grid=(nt,),
        in_specs=[pl.BlockSpec((1, 1, TOP_K * tm), lambda i: (i, 0, 0), memory_space=pltpu.SMEM),
                  pl.BlockSpec((tm, d), lambda i: (i, 0)),
                  pl.BlockSpec(memory_space=pl.ANY)],
        out_specs=pl.BlockSpec(memory_space=pl.ANY),
        out_shape=jax.ShapeDtypeStruct((n_rows, d), F32),
        scratch_shapes=[pltpu.SemaphoreType.DMA(())],
        input_output_aliases={2: 0},
        compiler_params=_cparams(("arbitrary",)),
        name="moe_scatter",
    )(dest3, xn, init)


def _combine_kernel(dcur_ref, dnext_ref, y_hbm, h_ref, route_ref, gain_ref, o_ref, buf, sems, *, tm):
    i = pl.program_id(0)
    n = pl.num_programs(0)
    slot = i % 2

    def row_copy(d_ref, s, r, kk):
        return pltpu.make_async_copy(y_hbm.at[pl.ds(d_ref[0, 0, TOP_K * r + kk], 1), :],
                                     buf.at[s, kk, pl.ds(r, 1), :], sems.at[s])

    def issue(d_ref, s):
        def body(r, carry):
            for kk in range(TOP_K):
                row_copy(d_ref, s, r, kk).start(priority=kk)
            return carry
        lax.fori_loop(0, tm, body, 0, unroll=DMA_UNROLL)

    @pl.when(i == 0)
    def _():
        issue(dcur_ref, slot)

    @pl.when(i + 1 < n)
    def _():
        issue(dnext_ref, 1 - slot)

    for kk in range(TOP_K):
        pltpu.make_async_copy(y_hbm.at[pl.ds(0, tm), :], buf.at[slot, kk], sems.at[slot]).wait()

    route = route_ref[...]
    lane = lax.broadcasted_iota(jnp.int32, route.shape, 1)
    g1 = jnp.sum(jnp.where(lane == 2, route, 0.0), axis=1, keepdims=True)
    g2 = jnp.sum(jnp.where(lane == 3, route, 0.0), axis=1, keepdims=True)
    out = h_ref[...] + (g1 * buf[slot, 0] + g2 * buf[slot, 1])
    o_ref[...] = _rms(out, gain_ref[...])


def _combine(y_sorted, dest, h, route, final_gain):
    t, d = h.shape
    tm = min(GATHER_TM, t)
    nt = t // tm
    dest3 = dest.reshape(nt, 1, TOP_K * tm)
    row = pl.BlockSpec((tm, d), lambda i: (i, 0))
    smem = functools.partial(pl.BlockSpec, (1, 1, TOP_K * tm), memory_space=pltpu.SMEM)
    return pl.pallas_call(
        functools.partial(_combine_kernel, tm=tm),
        grid=(nt,),
        in_specs=[smem(index_map=lambda i: (i, 0, 0)),
                  smem(index_map=lambda i: (jnp.minimum(i + 1, nt - 1), 0, 0)),
                  pl.BlockSpec(memory_space=pl.ANY),
                  row,
                  pl.BlockSpec((tm, LANES), lambda i: (i, 0)),
                  pl.BlockSpec((1, d), lambda i: (0, 0))],
        out_specs=row,
        out_shape=jax.ShapeDtypeStruct((t, d), F32),
        scratch_shapes=[pltpu.VMEM((2, TOP_K, tm, d), F32), pltpu.SemaphoreType.DMA((2,))],
        compiler_params=_cparams(("arbitrary",)),
        name="moe_combine",
    )(dest3, dest3, y_sorted, h, route, final_gain.reshape(1, d))


def _routing_plan(expert_idx, tm):
    t = expert_idx.shape[0]
    flat = expert_idx.reshape(-1)
    onehot = (flat[:, None] == jnp.arange(N_EXPERTS, dtype=jnp.int32)[None, :]).astype(jnp.int32)
    running = jnp.cumsum(onehot, axis=0)
    rank = jnp.sum((running - onehot) * onehot, axis=1)
    counts = running[-1]
    tiles = (counts + tm - 1) // tm
    tile_end = jnp.cumsum(tiles)
    group_start = (tile_end - tiles) * tm
    dest = jnp.sum(onehot * group_start[None, :], axis=1) + rank
    n_tiles = (t * TOP_K) // tm + N_EXPERTS
    n_used = tile_end[-1]
    tile_id = jnp.minimum(jnp.arange(n_tiles, dtype=jnp.int32), n_used - 1)
    tile_expert = jnp.sum((tile_id[:, None] >= tile_end[None, :]).astype(jnp.int32), axis=1)
    return dest.reshape(t, TOP_K).astype(jnp.int32), tile_expert.astype(jnp.int32), \
        n_used.reshape(1).astype(jnp.int32), n_tiles * tm


def kernel(x, attn_norm_even, w_in_even, ret_norm_even, w_out_even, ffn_norm_even, w_gate_even, w_up_even, w_down_even, attn_norm_odd, w_in_odd, b_forget_odd, w_out_odd, ffn_norm_odd, w_router_odd, w_gate_moe_odd, w_up_moe_odd, w_down_moe_odd, final_norm):
    batch, seq, d = x.shape
    t = batch * seq
    n_ret = d // (2 * HEAD_DIM)
    n_sb = d // (2 * HEAD_DIM)
    n_fox = d // HEAD_DIM
    ret_width = n_ret * HEAD_DIM
    sb_width = n_sb * HEAD_DIM
    fox_width = n_fox * HEAD_DIM
    sub = min(ATT_T, seq) // 2
    h0 = x.reshape(t, d)

    w_in = w_in_even[0]
    n_direct = 4 * ret_width + 2 * sb_width
    proj, xn0 = _norm_mm(h0, attn_norm_even[0], w_in[:, :n_direct].astype(BF16), BF16)
    v_sb_t = _mm_nt(xn0, w_in[:, n_direct:].T.astype(BF16), sub)
    y_ret = _retention(proj, ret_norm_even[0], batch, seq, n_ret)
    y_sb = _stick_breaking(proj, v_sb_t, batch, seq, n_sb, first_group=4)
    w_out = w_out_even[0].astype(BF16)
    h1, xn1 = _outproj([y_ret, y_sb], [w_out[:ret_width], w_out[ret_width:]], h0, ffn_norm_even[0])
    h2, xn2 = _ffn_dense(xn1, w_gate_even[0].astype(BF16), w_up_even[0].astype(BF16),
                         w_down_even[0].astype(BF16), h1, attn_norm_odd[0])

    w_in = w_in_odd[0]
    proj = _mm(xn2, w_in[:, :2 * fox_width].astype(BF16), BF16)
    v_fox_t = _mm_nt(xn2, w_in[:, 2 * fox_width:3 * fox_width].T.astype(BF16), sub)
    w_f = jnp.zeros((d, LANES), BF16).at[:, :n_fox].set(w_in[:, 3 * fox_width:].astype(BF16))
    f_logit = _mm(xn2, w_f, F32)
    b_f = jnp.zeros((1, LANES), F32).at[0, :n_fox].set(b_forget_odd[0])
    cum_cols = _forget_cum(f_logit, b_f, batch, seq)
    y_fox = _forgetting_attention(proj, v_fox_t, cum_cols, batch, seq, n_fox)
    h3 = _outproj([y_fox], [w_out_odd[0].astype(BF16)], h2)

    xn3, route = _router(h3, ffn_norm_odd[0], w_router_odd[0])
    expert_idx = route[:, :TOP_K].astype(jnp.int32)
    dest, tile_expert, n_used, n_rows = _routing_plan(expert_idx, FFN_TM)
    x_sorted = _scatter_rows(xn3, dest, n_rows)
    y_sorted = _ffn_grouped(x_sorted, w_gate_moe_odd[0].astype(BF16), w_up_moe_odd[0].astype(BF16),
                            w_down_moe_odd[0].astype(BF16), tile_expert, n_used)
    out = _combine(y_sorted, dest, h3, route, final_norm)
    return out.reshape(batch, seq, d)
```

```python
import functools

import jax
import jax.numpy as jnp
import numpy as np
from jax import lax
from jax.experimental import pallas as pl
from jax.experimental.pallas import tpu as pltpu

F32 = jnp.float32
BF16 = jnp.bfloat16

LANES = 128
HEAD_DIM = 64
HEADS_PER_BLOCK = LANES // HEAD_DIM
N_EXPERTS = 8
TOP_K = 2
ROPE_BASE = 10000.0
NORM_EPS = 1e-6
GROUP_NORM_EPS = 1e-5
LOG2E = 1.4426950408889634
UNDERFLOW_BITS = 160.0
FOX_MARGIN_BITS = 8.0
NORM_SLACK = 1.02
RET_CHUNK = 256
VMEM_LIMIT = 56 * 1024 * 1024

MM_TM = 1024
MM_TN = 1024
ROW_TM = 512
ATT_T = 512
ATT_COLS = 2
RET_TS = 1024
FFN_TM = 256
FFN_NF = 1
GATHER_TM = 512
DMA_UNROLL = 8
N_BIAS_PIECES = 3
FOX_VALUE_ROWS = HEAD_DIM + 16


def _cparams(sem, vmem=VMEM_LIMIT):
    return pltpu.CompilerParams(dimension_semantics=sem, vmem_limit_bytes=vmem)


def _rms(xf, gain_row):
    ms = jnp.mean(xf * xf, axis=-1, keepdims=True)
    return xf * lax.rsqrt(ms + NORM_EPS) * gain_row


def _dot(a, b):
    return jnp.dot(a, b, preferred_element_type=F32)


def _dot_nt(a, b):
    return lax.dot_general(a, b, (((1,), (1,)), ((), ())), preferred_element_type=F32)


def _dot_tn(a, b):
    return lax.dot_general(a, b, (((0,), (0,)), ((), ())), preferred_element_type=F32)


def _norm_mm_kernel(x_ref, g_ref, w_ref, o_ref, xn_ref):
    @pl.when(pl.program_id(1) == 0)
    def _():
        xn_ref[...] = _rms(x_ref[...], g_ref[...]).astype(BF16)

    o_ref[...] = _dot(xn_ref[...], w_ref[...]).astype(o_ref.dtype)


def _norm_mm(x, gain, w, out_dtype):
    t, k = x.shape
    n = w.shape[1]
    tm, tn = min(MM_TM, t), min(MM_TN, n)
    return pl.pallas_call(
        _norm_mm_kernel,
        grid=(t // tm, n // tn),
        in_specs=[pl.BlockSpec((tm, k), lambda i, j: (i, 0)),
                  pl.BlockSpec((1, k), lambda i, j: (0, 0)),
                  pl.BlockSpec((k, tn), lambda i, j: (0, j))],
        out_specs=[pl.BlockSpec((tm, tn), lambda i, j: (i, j)),
                   pl.BlockSpec((tm, k), lambda i, j: (i, 0))],
        out_shape=[jax.ShapeDtypeStruct((t, n), out_dtype), jax.ShapeDtypeStruct((t, k), BF16)],
        compiler_params=_cparams(("parallel", "arbitrary")),
        name="norm_mm",
    )(x, gain.reshape(1, k), w)


def _mm_kernel(x_ref, w_ref, o_ref):
    o_ref[...] = _dot(x_ref[...], w_ref[...]).astype(o_ref.dtype)


def _mm(x, w, out_dtype):
    t, k = x.shape
    n = w.shape[1]
    tm, tn = min(MM_TM, t), min(MM_TN, n)
    return pl.pallas_call(
        _mm_kernel,
        grid=(t // tm, n // tn),
        in_specs=[pl.BlockSpec((tm, k), lambda i, j: (i, 0)),
                  pl.BlockSpec((k, tn), lambda i, j: (0, j))],
        out_specs=pl.BlockSpec((tm, tn), lambda i, j: (i, j)),
        out_shape=jax.ShapeDtypeStruct((t, n), out_dtype),
        compiler_params=_cparams(("parallel", "parallel")),
        name="mm",
    )(x, w)


def _mm_nt_kernel(x_ref, wt_ref, o_ref, *, sub):
    res = _dot_nt(wt_ref[...], x_ref[...]).astype(o_ref.dtype)
    for s in range(o_ref.shape[0]):
        o_ref[s] = res[:, s * sub:(s + 1) * sub]


def _mm_nt(x, wt, sub):
    t, k = x.shape
    n = wt.shape[0]
    tm, tn = min(MM_TM, t), min(MM_TN, n)
    return pl.pallas_call(
        functools.partial(_mm_nt_kernel, sub=sub),
        grid=(t // tm, n // tn),
        in_specs=[pl.BlockSpec((tm, k), lambda i, j: (i, 0)),
                  pl.BlockSpec((tn, k), lambda i, j: (j, 0))],
        out_specs=pl.BlockSpec((tm // sub, tn, sub), lambda i, j: (i, j, 0)),
        out_shape=jax.ShapeDtypeStruct((t // sub, n, sub), BF16),
        compiler_params=_cparams(("parallel", "parallel")),
        name="mm_nt",
    )(x, wt)


def _outproj_kernel(*refs, n_in, with_norm):
    ys = refs[:n_in]
    ws = refs[n_in:2 * n_in]
    h_ref = refs[2 * n_in]
    pos = 2 * n_in + 1
    acc = h_ref[...]
    for y_ref, w_ref in zip(ys, ws):
        acc = acc + _dot(y_ref[...], w_ref[...])
    if with_norm:
        g_ref, ho_ref, xn_ref = refs[pos], refs[pos + 1], refs[pos + 2]
        ho_ref[...] = acc
        xn_ref[...] = _rms(acc, g_ref[...]).astype(xn_ref.dtype)
    else:
        refs[pos][...] = acc


def _outproj(ys, ws, h, gain=None):
    t, d = h.shape
    tm = min(ROW_TM, t)
    n_in = len(ys)
    with_norm = gain is not None
    in_specs = [pl.BlockSpec((tm, y.shape[1]), lambda i: (i, 0)) for y in ys]
    in_specs += [pl.BlockSpec(w.shape, lambda i: (0, 0)) for w in ws]
    in_specs += [pl.BlockSpec((tm, d), lambda i: (i, 0))]
    args = list(ys) + list(ws) + [h]
    row_spec = pl.BlockSpec((tm, d), lambda i: (i, 0))
    if with_norm:
        in_specs += [pl.BlockSpec((1, d), lambda i: (0, 0))]
        args += [gain.reshape(1, d)]
        out_specs = [row_spec, row_spec]
        out_shape = [jax.ShapeDtypeStruct((t, d), F32), jax.ShapeDtypeStruct((t, d), BF16)]
    else:
        out_specs = row_spec
        out_shape = jax.ShapeDtypeStruct((t, d), F32)
    return pl.pallas_call(
        functools.partial(_outproj_kernel, n_in=n_in, with_norm=with_norm),
        grid=(t // tm,),
        in_specs=in_specs,
        out_specs=out_specs,
        out_shape=out_shape,
        compiler_params=_cparams(("parallel",)),
        name="outproj",
    )(*args)


def _retention_tables(seq, n_heads):
    half = HEAD_DIM // 2
    lane = np.arange(LANES)
    inv_freq = ROPE_BASE ** (-jnp.arange(half, dtype=F32) / half)
    ang = jnp.arange(seq, dtype=F32)[:, None] * inv_freq[None, :]
    cos, sin = jnp.cos(ang), jnp.sin(ang)
    cos_t = jnp.tile(cos, (1, LANES // half))
    sign = np.where((lane % HEAD_DIM) < half, -1.0, 1.0).astype(np.float32)
    sin_t = jnp.tile(sin, (1, LANES // half)) * sign[None, :]
    c = RET_CHUNK
    log_gamma = jnp.log(1.0 - 2.0 ** (-5.0 - jnp.arange(n_heads, dtype=F32)))
    pos = jnp.arange(c, dtype=F32)
    diff = pos[:, None] - pos[None, :]
    intra = jnp.where(diff >= 0.0,
                      jnp.exp(log_gamma[:, None, None] * jnp.maximum(diff, 0.0)), 0.0)
    intra = intra.reshape(n_heads // 2, 2, c, c)
    q_decay = jnp.exp(log_gamma[:, None] * (pos + 1.0))
    k_decay = jnp.exp(log_gamma[:, None] * (c - 1.0 - pos))
    chunk_decay = jnp.exp(log_gamma * c)

    def per_lane(tab):
        tab = tab.reshape(n_heads // 2, 2, c)
        return jnp.repeat(tab.transpose(0, 2, 1), HEAD_DIM, axis=2)

    head_of = lane // HEAD_DIM
    same = (head_of[:, None] == head_of[None, :]).astype(np.float32)
    cd = chunk_decay.reshape(n_heads // 2, 2)
    cd_rows = jnp.repeat(cd, HEAD_DIM, axis=1)
    state_decay = cd_rows[:, :, None] * same[None]
    return cos_t, sin_t, intra, per_lane(q_decay), per_lane(k_decay), state_decay, jnp.asarray(same)


def _retention_kernel(q_ref, k_ref, v_ref, g_ref, cos_ref, sin_ref, intra_ref, qd_ref, kd_ref,
                      sd_ref, same_ref, rn_ref, o_ref, state_ref, *, ts):
    @pl.when(pl.program_id(2) == 0)
    def _():
        state_ref[...] = jnp.zeros_like(state_ref)

    c = RET_CHUNK
    lane = lax.broadcasted_iota(jnp.int32, (1, LANES), 1)
    first_half = (lane % HEAD_DIM) < (HEAD_DIM // 2)
    head0 = lane < HEAD_DIM

    def rot(t, cos, sin):
        swapped = jnp.where(first_half, pltpu.roll(t, LANES - HEAD_DIM // 2, 1),
                            pltpu.roll(t, HEAD_DIM // 2, 1))
        return t * cos + swapped * sin

    for ci in range(ts // c):
        rows = slice(ci * c, (ci + 1) * c)
        cos, sin = cos_ref[rows, :], sin_ref[rows, :]
        q = rot(q_ref[rows, :].astype(F32), cos, sin)
        k = rot(k_ref[rows, :].astype(F32), cos, sin) * (HEAD_DIM ** -0.5)
        v = v_ref[rows, :]
        kb = k.astype(BF16)
        inner = []
        for hd in range(HEADS_PER_BLOCK):
            hmask = head0 if hd == 0 else jnp.logical_not(head0)
            qh = jnp.where(hmask, q, 0.0).astype(BF16)
            scores = _dot_nt(qh, kb) * intra_ref[hd]
            inner.append(_dot(scores.astype(BF16), v))
        state = state_ref[...]
        cross = _dot((q * qd_ref[...]).astype(BF16), state.astype(BF16))
        y = jnp.where(head0, inner[0], inner[1]) + cross
        kv = _dot_tn((k * kd_ref[...]).astype(BF16), v)
        state_ref[...] = state * sd_ref[...] + kv * same_ref[...]

        s0 = jnp.sum(jnp.where(head0, y, 0.0), axis=1, keepdims=True)
        s1 = jnp.sum(jnp.where(head0, 0.0, y), axis=1, keepdims=True)
        d = y - jnp.where(head0, s0, s1) * (1.0 / HEAD_DIM)
        dd = d * d
        v0 = jnp.sum(jnp.where(head0, dd, 0.0), axis=1, keepdims=True)
        v1 = jnp.sum(jnp.where(head0, 0.0, dd), axis=1, keepdims=True)
        var = jnp.where(head0, v0, v1) * (1.0 / HEAD_DIM)
        g = g_ref[rows, :].astype(F32)
        silu = g * (1.0 / (1.0 + jnp.exp(-g)))
        o_ref[rows, :] = (d * lax.rsqrt(var + GROUP_NORM_EPS) * rn_ref[...] * silu).astype(o_ref.dtype)


def _retention(proj, ret_norm, batch, seq, n_heads):
    t = proj.shape[0]
    width = n_heads * HEAD_DIM
    nb = width // LANES
    ts = min(RET_TS, seq)
    ns = seq // ts
    tabs = _retention_tables(seq, n_heads)
    cos_t, sin_t, intra, qd, kd, sd, same = tabs
    c = RET_CHUNK

    def col(group):
        return pl.BlockSpec((ts, LANES), lambda b, hp, si: (b * ns + si, group * nb + hp))

    in_specs = [col(0), col(1), col(2), col(3),
                pl.BlockSpec((ts, LANES), lambda b, hp, si: (si, 0)),
                pl.BlockSpec((ts, LANES), lambda b, hp, si: (si, 0)),
                pl.BlockSpec((None, 2, c, c), lambda b, hp, si: (hp, 0, 0, 0)),
                pl.BlockSpec((None, c, LANES), lambda b, hp, si: (hp, 0, 0)),
                pl.BlockSpec((None, c, LANES), lambda b, hp, si: (hp, 0, 0)),
                pl.BlockSpec((None, LANES, LANES), lambda b, hp, si: (hp, 0, 0)),
                pl.BlockSpec((LANES, LANES), lambda b, hp, si: (0, 0)),
                pl.BlockSpec((1, LANES), lambda b, hp, si: (0, hp))]
    return pl.pallas_call(
        functools.partial(_retention_kernel, ts=ts),
        grid=(batch, nb, ns),
        in_specs=in_specs,
        out_specs=pl.BlockSpec((ts, LANES), lambda b, hp, si: (b * ns + si, hp)),
        out_shape=jax.ShapeDtypeStruct((t, width), BF16),
        scratch_shapes=[pltpu.VMEM((LANES, LANES), F32)],
        compiler_params=_cparams(("parallel", "parallel", "arbitrary")),
        name="retention",
    )(proj, proj, proj, proj, cos_t, sin_t, intra, qd, kd, sd, same, ret_norm.reshape(1, width))


def _head_masks():
    lane = lax.broadcasted_iota(jnp.int32, (1, LANES), 1)
    head0 = lane < HEAD_DIM
    return lane, [head0, jnp.logical_not(head0)]


def _col_block(x, hd):
    cb = hd // HEADS_PER_BLOCK
    return x[:, cb * LANES:(cb + 1) * LANES]


def _head_rows(x, hd):
    return x[hd * HEAD_DIM:(hd + 1) * HEAD_DIM]


def _store_heads(o_ref, acc_t):
    for cb in range(ATT_COLS):
        pair = jnp.concatenate(acc_t[HEADS_PER_BLOCK * cb:HEADS_PER_BLOCK * (cb + 1)], axis=0)
        o_ref[:, cb * LANES:(cb + 1) * LANES] = pair.T.astype(o_ref.dtype)


def _two_stage_blocks(qi, sub, scores_to, apply_from, carry, upper_diag_first, rest_is_zero=None):
    top = 2 * qi + 1
    first, second = ((top, sub), (top - 1, 0)) if upper_diag_first else ((top - 1, 0), (top, sub))
    scores_to(0, *first)
    scores_to(1, *second)
    carry = apply_from(0, first[0], carry)

    def more(state):
        i, carry = state
        if rest_is_zero is None:
            return i < qi
        return jnp.logical_and(i < qi, jnp.logical_not(rest_is_zero(carry, top - 2 - 2 * i)))

    def pair(state):
        i, carry = state
        jb = top - 2 - 2 * i
        scores_to(0, jb, None)
        carry = apply_from(1, jnp.where(i == 0, second[0], jb + 1), carry)
        scores_to(1, jb - 1, None)
        return i + 1, apply_from(0, jb, carry)

    n_pairs, carry = lax.while_loop(more, pair, (jnp.int32(0), carry))
    last = jnp.where(n_pairs == 0, second[0], top - 1 - 2 * n_pairs)
    if rest_is_zero is None:
        return apply_from(1, last, carry)
    skip = jnp.logical_and(n_pairs > 0, rest_is_zero(carry, last))
    return lax.cond(skip, lambda c: c, lambda c: apply_from(1, last, c), carry)


def _sb_kernel(q_ref, k_ref, vt_ref, o_ref, d0_ref, d1_ref, tot0_ref, tot1_ref, *, tq, sub):
    qi = pl.program_id(2)
    d_refs = (d0_ref, d1_ref)
    tot_refs = (tot0_ref, tot1_ref)
    _, hmasks = _head_masks()
    key = lax.broadcasted_iota(jnp.int32, (sub, tq), 0)
    qry = lax.broadcasted_iota(jnp.int32, (sub, tq), 1)
    r = lax.broadcasted_iota(jnp.int32, (sub, sub), 0)
    c = lax.broadcasted_iota(jnp.int32, (sub, sub), 1)
    suffix = jnp.where(c >= r, 1.0, 0.0).astype(BF16)
    qf = q_ref[...].astype(F32) * (HEAD_DIM ** -0.5 * LOG2E)
    heads = range(ATT_COLS * HEADS_PER_BLOCK)
    qh = [jnp.where(hmasks[hd % HEADS_PER_BLOCK], _col_block(qf, hd), 0.0).astype(BF16) for hd in heads]

    sign_bit = jnp.uint32(0x80000000)

    def scores_to(buf, jb, diag_off):
        k = k_ref[jb]
        for hd in heads:
            z = _dot_nt(_col_block(k, hd), qh[hd])
            if diag_off is not None:
                z = jnp.where((key + diag_off) < qry, z, -jnp.inf)
            neg_abs = lax.bitcast_convert_type(lax.bitcast_convert_type(z, jnp.uint32) | sign_bit, F32)
            fail = jnp.maximum(z, 0.0) + jnp.log2(1.0 + jnp.exp2(neg_abs))
            tail = _dot(suffix, fail.astype(BF16))
            d_refs[buf][hd] = jnp.minimum(z - tail, 0.0)
            tot_refs[buf][hd] = tail[0:1, :]

    def apply_from(buf, jb, carry):
        vt = vt_ref[jb]
        out = []
        for hd in heads:
            later, acc = carry[hd]
            w = jnp.exp2(d_refs[buf][hd] - later)
            out.append((later + tot_refs[buf][hd], acc + _dot(_head_rows(vt, hd), w.astype(BF16))))
        return out

    def rest_is_zero(carry, _):
        least = carry[0][0]
        for hd in heads[1:]:
            least = jnp.minimum(least, carry[hd][0])
        return jnp.min(least) > UNDERFLOW_BITS

    carry = [(jnp.zeros((1, tq), F32), jnp.zeros((HEAD_DIM, tq), F32)) for _ in heads]
    carry = _two_stage_blocks(qi, sub, scores_to, apply_from, carry, upper_diag_first=True,
                              rest_is_zero=rest_is_zero)
    _store_heads(o_ref, [c[1] for c in carry])


def _stick_breaking(proj, v_t, batch, seq, n_heads, first_group):
    t = proj.shape[0]
    width = n_heads * HEAD_DIM
    nb = width // LANES
    tq = min(ATT_T, seq)
    sub = tq // 2
    nq = seq // tq
    nk = seq // sub
    assert v_t.shape[2] == sub
    proj_k = proj.reshape(t // sub, sub, proj.shape[1])
    n_step = ATT_COLS * HEADS_PER_BLOCK
    cols = ATT_COLS * LANES
    ng = nb // ATT_COLS
    score_buf = pltpu.VMEM((n_step, sub, tq), F32)
    total_buf = pltpu.VMEM((n_step, 1, tq), F32)
    q_spec = pl.BlockSpec((tq, cols), lambda b, hp, qi: (b * nq + qi, first_group * ng + hp))
    k_spec = pl.BlockSpec((nk, sub, cols), lambda b, hp, qi: (b, 0, (first_group + 1) * ng + hp))
    v_spec = pl.BlockSpec((nk, cols, sub), lambda b, hp, qi: (b, hp, 0))
    return pl.pallas_call(
        functools.partial(_sb_kernel, tq=tq, sub=sub),
        grid=(batch, ng, nq),
        in_specs=[q_spec, k_spec, v_spec],
        out_specs=pl.BlockSpec((tq, cols), lambda b, hp, qi: (b * nq + qi, hp)),
        out_shape=jax.ShapeDtypeStruct((t, width), BF16),
        scratch_shapes=[score_buf, score_buf, total_buf, total_buf],
        compiler_params=_cparams(("parallel", "parallel", "arbitrary")),
        name="stick_breaking",
    )(proj, proj_k, v_t)


def _forget_cum_kernel(f_ref, b_ref, col_ref, *, seq):
    x = f_ref[...] + b_ref[...]
    log_f = jnp.minimum(x, 0.0) - jnp.log(1.0 + jnp.exp(-jnp.abs(x)))
    xt = log_f.T
    pos = lax.broadcasted_iota(jnp.int32, xt.shape, 1)
    shift = 1
    while shift < seq:
        xt = xt + jnp.where(pos >= shift, pltpu.roll(xt, shift, 1), 0.0)
        shift *= 2
    col_ref[...] = xt.T


def _forget_cum(f_logit, b_forget, batch, seq):
    t = f_logit.shape[0]
    return pl.pallas_call(
        functools.partial(_forget_cum_kernel, seq=seq),
        grid=(batch,),
        in_specs=[pl.BlockSpec((seq, LANES), lambda b: (b, 0)),
                  pl.BlockSpec((1, LANES), lambda b: (0, 0))],
        out_specs=pl.BlockSpec((seq, LANES), lambda b: (b, 0)),
        out_shape=jax.ShapeDtypeStruct((t, LANES), F32),
        compiler_params=_cparams(("parallel",)),
        name="forget_cum",
    )(f_logit, b_forget)


def _bias_lanes(hd):
    return HEAD_DIM * (1 - hd)


def _bias_tiles(cum, hp, is_query):
    n_step = ATT_COLS * HEADS_PER_BLOCK
    pieces = []
    rest = cum * LOG2E
    for _ in range(N_BIAS_PIECES):
        p = rest.astype(BF16)
        pieces.append(p)
        rest = rest - p.astype(F32)
    stacked = jnp.concatenate(pieces, axis=1)
    row = lax.broadcasted_iota(jnp.int32, (N_BIAS_PIECES * LANES, 1), 0)
    piece = row >> 7
    head = (row & (LANES - 1)) - n_step * hp
    first = 0 if is_query else N_BIAS_PIECES
    target = (head >> 1) * LANES + (1 - (head & 1)) * HEAD_DIM + first + piece
    target = jnp.where(jnp.logical_and(head >= 0, head < n_step), target, -1)
    col = lax.broadcasted_iota(jnp.int32, (1, ATT_COLS * LANES), 1)
    selector = jnp.where(col == target, 1.0 if is_query else -1.0, 0.0).astype(BF16)
    tile = _dot(stacked, selector)
    ones_first = N_BIAS_PIECES if is_query else 0
    in_half = col & (HEAD_DIM - 1)
    ones = jnp.logical_and(in_half >= ones_first, in_half < ones_first + N_BIAS_PIECES)
    return jnp.where(ones, 1.0, tile)


def _fox_kernel(q_ref, k_ref, vt_ref, cq_ref, ck_ref, o_ref, kp_ref, vp_ref, ksq_ref, s0_ref, s1_ref,
                max0_ref, max1_ref, *, tq, sub, seq):
    hp = pl.program_id(1)
    qi = pl.program_id(2)
    s_refs = (s0_ref, s1_ref)
    max_refs = (max0_ref, max1_ref)
    nk = seq // sub
    lane, hmasks = _head_masks()
    heads = range(ATT_COLS * HEADS_PER_BLOCK)

    def with_bias_lanes(x, bias_tile, hd):
        return jnp.where(hmasks[hd % HEADS_PER_BLOCK], _col_block(x, hd), _col_block(bias_tile, hd)).astype(BF16)

    @pl.when(qi == 0)
    def _():
        kf = k_ref[...].astype(F32)
        k_bias = _bias_tiles(ck_ref[...], hp, False)
        extra = lax.broadcasted_iota(jnp.int32, (FOX_VALUE_ROWS - HEAD_DIM, sub), 0)
        ones = jnp.where(extra == 0, 1.0, 0.0).astype(BF16)
        for hd in heads:
            side = hd % HEADS_PER_BLOCK
            k_hd = _col_block(kf, hd)
            kp_ref[hd] = with_bias_lanes(kf, k_bias, hd)
            k_sq = jnp.sum(jnp.where(hmasks[side], k_hd * k_hd, 0.0), axis=1, keepdims=True)
            ksq_ref[hd] = jnp.broadcast_to(jnp.max(k_sq, axis=0, keepdims=True), (1, LANES))
            for jb in range(nk):
                vp_ref[hd, jb, 0:HEAD_DIM] = _head_rows(vt_ref[jb], hd)
                vp_ref[hd, jb, HEAD_DIM:FOX_VALUE_ROWS] = ones

    qf = q_ref[...].astype(F32) * (HEAD_DIM ** -0.5 * LOG2E)
    q_bias = _bias_tiles(cq_ref[...], hp, True)
    qh = [with_bias_lanes(qf, q_bias, hd) for hd in heads]
    key = lax.broadcasted_iota(jnp.int32, (sub, tq), 0)
    qry = lax.broadcasted_iota(jnp.int32, (sub, tq), 1)

    pick = lax.broadcasted_iota(jnp.int32, (8, LANES), 0)
    pick_lane = lax.broadcasted_iota(jnp.int32, (8, LANES), 1)
    head_rows = jnp.where(pick == pick_lane // HEAD_DIM, 1.0, 0.0).astype(BF16)
    reach = []
    for hd in heads:
        side = hd % HEADS_PER_BLOCK
        q_hd = _col_block(qf, hd)
        q_sq = _dot_nt(head_rows, (q_hd * q_hd).astype(BF16))[side:side + 1]
        spare = _bias_lanes(side)
        bias_row = jnp.where((pick_lane >= spare) & (pick_lane < spare + N_BIAS_PIECES), 1.0, 0.0).astype(BF16)
        cq_row = _dot_nt(bias_row, qh[hd])[0:1]
        reach.append(NORM_SLACK * jnp.sqrt(q_sq * ksq_ref[hd][:, 0:1]) + cq_row)

    def scores_to(buf, jb, diag_off):
        ks = pl.multiple_of(jb * sub, sub)
        for hd in heads:
            s = _dot_nt(kp_ref[hd, pl.ds(ks, sub), :], qh[hd])
            if diag_off is not None:
                s = jnp.where((key + diag_off) <= qry, s, -jnp.inf)
            s_refs[buf][hd] = s
            max_refs[buf][hd] = jnp.max(s, axis=0, keepdims=True)

    def apply_from(buf, jb, carry):
        out = []
        for hd in heads:
            m, acc = carry[hd]
            m_new = jnp.maximum(m, max_refs[buf][hd])
            p = jnp.exp2(s_refs[buf][hd] - m_new).astype(BF16)
            out.append((m_new, jnp.exp2(m - m_new) * acc + _dot(vp_ref[hd, jb], p)))
        return out

    def rest_is_zero(carry, jb):
        ck_row = ck_ref[pl.ds(jnp.maximum((jb + 1) * sub - 1, 0), 1), :]
        worst = None
        for hd in heads:
            sel = lane == ATT_COLS * HEADS_PER_BLOCK * hp + hd
            ck_last = jnp.sum(jnp.where(sel, ck_row, 0.0), axis=1, keepdims=True) * LOG2E
            gap = reach[hd] - ck_last - carry[hd][0]
            worst = gap if worst is None else jnp.maximum(worst, gap)
        return jnp.max(worst) < -(UNDERFLOW_BITS + FOX_MARGIN_BITS)

    carry = [(jnp.full((1, tq), -jnp.inf, F32), jnp.zeros((FOX_VALUE_ROWS, tq), F32)) for _ in heads]
    carry = _two_stage_blocks(qi, sub, scores_to, apply_from, carry, upper_diag_first=False,
                              rest_is_zero=rest_is_zero)
    normed = []
    for hd in heads:
        acc = carry[hd][1]
        normed.append(acc[0:HEAD_DIM] * (1.0 / acc[HEAD_DIM:HEAD_DIM + 1, :]))
    _store_heads(o_ref, normed)


def _forgetting_attention(proj, v_t, cum_cols, batch, seq, n_heads):
    t = proj.shape[0]
    width = n_heads * HEAD_DIM
    nb = width // LANES
    tq = min(ATT_T, seq)
    sub = tq // 2
    nq = seq // tq
    nk = seq // sub
    assert v_t.shape[2] == sub
    n_step = ATT_COLS * HEADS_PER_BLOCK
    cols = ATT_COLS * LANES
    ng = nb // ATT_COLS
    score_buf = pltpu.VMEM((n_step, sub, tq), F32)
    max_buf = pltpu.VMEM((n_step, 1, tq), F32)
    q_spec = pl.BlockSpec((tq, cols), lambda b, hp, qi: (b * nq + qi, hp))
    k_spec = pl.BlockSpec((seq, cols), lambda b, hp, qi: (b, ng + hp))
    v_spec = pl.BlockSpec((nk, cols, sub), lambda b, hp, qi: (b, hp, 0))
    cq_spec = pl.BlockSpec((tq, LANES), lambda b, hp, qi: (b * nq + qi, 0))
    ck_spec = pl.BlockSpec((seq, LANES), lambda b, hp, qi: (b, 0))
    return pl.pallas_call(
        functools.partial(_fox_kernel, tq=tq, sub=sub, seq=seq),
        grid=(batch, ng, nq),
        in_specs=[q_spec, k_spec, v_spec, cq_spec, ck_spec],
        out_specs=pl.BlockSpec((tq, cols), lambda b, hp, qi: (b * nq + qi, hp)),
        out_shape=jax.ShapeDtypeStruct((t, width), BF16),
        scratch_shapes=[pltpu.VMEM((n_step, seq, LANES), BF16),
                        pltpu.VMEM((n_step, nk, FOX_VALUE_ROWS, sub), BF16),
                        pltpu.VMEM((n_step, 1, LANES), F32),
                        score_buf, score_buf, max_buf, max_buf],
        compiler_params=_cparams(("parallel", "parallel", "arbitrary")),
        name="forgetting_attention",
    )(proj, proj, v_t, cum_cols, cum_cols)


def _swiglu_partial(x, wg_ref, wu_ref, wd_ref):
    g = _dot(x, wg_ref[...])
    u = _dot(x, wu_ref[...])
    a = g * (1.0 / (1.0 + jnp.exp(-g))) * u
    return _dot(a.astype(BF16), wd_ref[...])


def _ffn_dense_kernel(x_ref, wg_ref, wu_ref, wd_ref, h_ref, gain_ref, ho_ref, xn_ref, acc_ref):
    f = pl.program_id(1)

    @pl.when(f == 0)
    def _():
        acc_ref[...] = h_ref[...]

    acc_ref[...] += _swiglu_partial(x_ref[...], wg_ref, wu_ref, wd_ref)

    @pl.when(f == pl.num_programs(1) - 1)
    def _():
        h_new = acc_ref[...]
        ho_ref[...] = h_new
        xn_ref[...] = _rms(h_new, gain_ref[...]).astype(xn_ref.dtype)


def _ffn_dense(xn, wg, wu, wd, h, next_gain):
    t, d = h.shape
    dff = wg.shape[1]
    tm = min(FFN_TM, t)
    tf = dff // FFN_NF
    row = pl.BlockSpec((tm, d), lambda i, f: (i, 0))
    return pl.pallas_call(
        _ffn_dense_kernel,
        grid=(t // tm, FFN_NF),
        in_specs=[row,
                  pl.BlockSpec((d, tf), lambda i, f: (0, f)),
                  pl.BlockSpec((d, tf), lambda i, f: (0, f)),
                  pl.BlockSpec((tf, d), lambda i, f: (f, 0)),
                  row,
                  pl.BlockSpec((1, d), lambda i, f: (0, 0))],
        out_specs=[row, row],
        out_shape=[jax.ShapeDtypeStruct((t, d), F32), jax.ShapeDtypeStruct((t, d), BF16)],
        scratch_shapes=[pltpu.VMEM((tm, d), F32)],
        compiler_params=_cparams(("parallel", "arbitrary")),
        name="ffn_dense",
    )(xn, wg, wu, wd, h, next_gain.reshape(1, d))


def _ffn_grouped_kernel(te_ref, nu_ref, x_ref, wg_ref, wu_ref, wd_ref, o_ref, acc_ref):
    i = pl.program_id(0)
    f = pl.program_id(1)
    used = i < nu_ref[0]

    @pl.when(jnp.logical_and(used, f == 0))
    def _():
        acc_ref[...] = jnp.zeros_like(acc_ref)

    @pl.when(used)
    def _():
        acc_ref[...] += _swiglu_partial(x_ref[...].astype(BF16), wg_ref, wu_ref, wd_ref)

    @pl.when(f == pl.num_programs(1) - 1)
    def _():
        @pl.when(used)
        def _():
            o_ref[...] = acc_ref[...]

        @pl.when(jnp.logical_not(used))
        def _():
            o_ref[...] = jnp.zeros_like(o_ref)


def _ffn_grouped(x_sorted, wg, wu, wd, tile_expert, n_used):
    r, d = x_sorted.shape
    dff = wg.shape[2]
    tm = FFN_TM
    tf = dff // FFN_NF
    grid_spec = pltpu.PrefetchScalarGridSpec(
        num_scalar_prefetch=2,
        grid=(r // tm, FFN_NF),
        in_specs=[pl.BlockSpec((tm, d), lambda i, f, te, nu: (i, 0)),
                  pl.BlockSpec((None, d, tf), lambda i, f, te, nu: (te[i], 0, f)),
                  pl.BlockSpec((None, d, tf), lambda i, f, te, nu: (te[i], 0, f)),
                  pl.BlockSpec((None, tf, d), lambda i, f, te, nu: (te[i], f, 0))],
        out_specs=pl.BlockSpec((tm, d), lambda i, f, te, nu: (i, 0)),
        scratch_shapes=[pltpu.VMEM((tm, d), F32)],
    )
    return pl.pallas_call(
        _ffn_grouped_kernel,
        grid_spec=grid_spec,
        out_shape=jax.ShapeDtypeStruct((r, d), F32),
        compiler_params=_cparams(("arbitrary", "arbitrary")),
        name="ffn_grouped",
    )(tile_expert, n_used, x_sorted, wg, wu, wd)


def _router_kernel(h_ref, gain_ref, wr_ref, xn_ref, route_ref):
    xn = _rms(h_ref[...], gain_ref[...])
    xn_ref[...] = xn
    logits = jnp.dot(xn, wr_ref[...], precision=lax.Precision.HIGHEST, preferred_element_type=F32)
    lane = lax.broadcasted_iota(jnp.int32, logits.shape, 1)
    lane_f = lane.astype(F32)
    lg = jnp.where(lane < N_EXPERTS, logits, -jnp.inf)
    v1 = jnp.max(lg, axis=1, keepdims=True)
    i1 = jnp.min(jnp.where(lg == v1, lane_f, float(LANES)), axis=1, keepdims=True)
    lg2 = jnp.where(lane_f == i1, -jnp.inf, lg)
    v2 = jnp.max(lg2, axis=1, keepdims=True)
    i2 = jnp.min(jnp.where(lg2 == v2, lane_f, float(LANES)), axis=1, keepdims=True)
    e2 = jnp.exp(v2 - v1)
    g1 = 1.0 / (1.0 + e2)
    g2 = e2 / (1.0 + e2)
    route_ref[...] = jnp.where(lane == 0, i1, jnp.where(lane == 1, i2,
                               jnp.where(lane == 2, g1, jnp.where(lane == 3, g2, 0.0))))


def _router(h, gain, w_router):
    t, d = h.shape
    tm = min(ROW_TM, t)
    wr = jnp.zeros((d, LANES), F32).at[:, :N_EXPERTS].set(w_router)
    row = pl.BlockSpec((tm, d), lambda i: (i, 0))
    return pl.pallas_call(
        _router_kernel,
        grid=(t // tm,),
        in_specs=[row, pl.BlockSpec((1, d), lambda i: (0, 0)),
                  pl.BlockSpec((d, LANES), lambda i: (0, 0))],
        out_specs=[row, pl.BlockSpec((tm, LANES), lambda i: (i, 0))],
        out_shape=[jax.ShapeDtypeStruct((t, d), F32), jax.ShapeDtypeStruct((t, LANES), F32)],
        compiler_params=_cparams(("parallel",)),
        name="router",
    )(h, gain.reshape(1, d), wr)


def _scatter_kernel(dest_ref, x_ref, init_hbm, o_hbm, sem, *, tm):
    del init_hbm

    def row_copy(r, kk):
        return pltpu.make_async_copy(x_ref.at[pl.ds(r, 1), :],
                                     o_hbm.at[pl.ds(dest_ref[0, 0, TOP_K * r + kk], 1), :], sem)

    def issue(r, carry):
        for kk in range(TOP_K):
            row_copy(r, kk).start(priority=kk)
        return carry

    lax.fori_loop(0, tm, issue, 0, unroll=DMA_UNROLL)

    for _ in range(TOP_K):
        pltpu.make_async_copy(x_ref, o_hbm.at[pl.ds(0, tm), :], sem).wait()


def _scatter_rows(xn, dest, n_rows):
    t, d = xn.shape
    tm = min(GATHER_TM, t)
    nt = t // tm
    dest3 = dest.reshape(nt, 1, TOP_K * tm)
    init = jnp.zeros((n_rows, d), F32)
    return pl.pallas_call(
        functools.partial(_scatter_kernel, tm=tm),
        grid=(nt,),
        in_specs=[pl.BlockSpec((1, 1, TOP_K * tm), lambda i: (i, 0, 0), memory_space=pltpu.SMEM),
                  pl.BlockSpec((tm, d), lambda i: (i, 0)),
                  pl.BlockSpec(memory_space=pl.ANY)],
        out_specs=pl.BlockSpec(memory_space=pl.ANY),
        out_shape=jax.ShapeDtypeStruct((n_rows, d), F32),
        scratch_shapes=[pltpu.SemaphoreType.DMA(())],
        input_output_aliases={2: 0},
        compiler_params=_cparams(("arbitrary",)),
        name="moe_scatter",
    )(dest3, xn, init)


def _combine_kernel(dcur_ref, dnext_ref, y_hbm, h_ref, route_ref, gain_ref, o_ref, buf, sems, *, tm):
    i = pl.program_id(0)
    n = pl.num_programs(0)
    slot = i % 2

    def row_copy(d_ref, s, r, kk):
        return pltpu.make_async_copy(y_hbm.at[pl.ds(d_ref[0, 0, TOP_K * r + kk], 1), :],
                                     buf.at[s, kk, pl.ds(r, 1), :], sems.at[s])

    def issue(d_ref, s):
        def body(r, carry):
            for kk in range(TOP_K):
                row_copy(d_ref, s, r, kk).start(priority=kk)
            return carry
        lax.fori_loop(0, tm, body, 0, unroll=DMA_UNROLL)

    @pl.when(i == 0)
    def _():
        issue(dcur_ref, slot)

    @pl.when(i + 1 < n)
    def _():
        issue(dnext_ref, 1 - slot)

    for kk in range(TOP_K):
        pltpu.make_async_copy(y_hbm.at[pl.ds(0, tm), :], buf.at[slot, kk], sems.at[slot]).wait()

    route = route_ref[...]
    lane = lax.broadcasted_iota(jnp.int32, route.shape, 1)
    g1 = jnp.sum(jnp.where(lane == 2, route, 0.0), axis=1, keepdims=True)
    g2 = jnp.sum(jnp.where(lane == 3, route, 0.0), axis=1, keepdims=True)
    out = h_ref[...] + (g1 * buf[slot, 0] + g2 * buf[slot, 1])
    o_ref[...] = _rms(out, gain_ref[...])


def _combine(y_sorted, dest, h, route, final_gain):
    t, d = h.shape
    tm = min(GATHER_TM, t)
    nt = t // tm
    dest3 = dest.reshape(nt, 1, TOP_K * tm)
    row = pl.BlockSpec((tm, d), lambda i: (i, 0))
    smem = functools.partial(pl.BlockSpec, (1, 1, TOP_K * tm), memory_space=pltpu.SMEM)
    return pl.pallas_call(
        functools.partial(_combine_kernel, tm=tm),
        grid=(nt,),
        in_specs=[smem(index_map=lambda i: (i, 0, 0)),
                  smem(index_map=lambda i: (jnp.minimum(i + 1, nt - 1), 0, 0)),
                  pl.BlockSpec(memory_space=pl.ANY),
                  row,
                  pl.BlockSpec((tm, LANES), lambda i: (i, 0)),
                  pl.BlockSpec((1, d), lambda i: (0, 0))],
        out_specs=row,
        out_shape=jax.ShapeDtypeStruct((t, d), F32),
        scratch_shapes=[pltpu.VMEM((2, TOP_K, tm, d), F32), pltpu.SemaphoreType.DMA((2,))],
        compiler_params=_cparams(("arbitrary",)),
        name="moe_combine",
    )(dest3, dest3, y_sorted, h, route, final_gain.reshape(1, d))


def _routing_plan(expert_idx, tm):
    t = expert_idx.shape[0]
    flat = expert_idx.reshape(-1)
    onehot = (flat[:, None] == jnp.arange(N_EXPERTS, dtype=jnp.int32)[None, :]).astype(jnp.int32)
    running = jnp.cumsum(onehot, axis=0)
    rank = jnp.sum((running - onehot) * onehot, axis=1)
    counts = running[-1]
    tiles = (counts + tm - 1) // tm
    tile_end = jnp.cumsum(tiles)
    group_start = (tile_end - tiles) * tm
    dest = jnp.sum(onehot * group_start[None, :], axis=1) + rank
    n_tiles = (t * TOP_K) // tm + N_EXPERTS
    n_used = tile_end[-1]
    tile_id = jnp.minimum(jnp.arange(n_tiles, dtype=jnp.int32), n_used - 1)
    tile_expert = jnp.sum((tile_id[:, None] >= tile_end[None, :]).astype(jnp.int32), axis=1)
    return dest.reshape(t, TOP_K).astype(jnp.int32), tile_expert.astype(jnp.int32), \
        n_used.reshape(1).astype(jnp.int32), n_tiles * tm


def kernel(x, attn_norm_even, w_in_even, ret_norm_even, w_out_even, ffn_norm_even, w_gate_even, w_up_even, w_down_even, attn_norm_odd, w_in_odd, b_forget_odd, w_out_odd, ffn_norm_odd, w_router_odd, w_gate_moe_odd, w_up_moe_odd, w_down_moe_odd, final_norm):
    batch, seq, d = x.shape
    t = batch * seq
    n_ret = d // (2 * HEAD_DIM)
    n_sb = d // (2 * HEAD_DIM)
    n_fox = d // HEAD_DIM
    ret_width = n_ret * HEAD_DIM
    sb_width = n_sb * HEAD_DIM
    fox_width = n_fox * HEAD_DIM
    sub = min(ATT_T, seq) // 2
    h0 = x.reshape(t, d)

    w_in = w_in_even[0]
    n_direct = 4 * ret_width + 2 * sb_width
    proj, xn0 = _norm_mm(h0, attn_norm_even[0], w_in[:, :n_direct].astype(BF16), BF16)
    v_sb_t = _mm_nt(xn0, w_in[:, n_direct:].T.astype(BF16), sub)
    y_ret = _retention(proj, ret_norm_even[0], batch, seq, n_ret)
    y_sb = _stick_breaking(proj, v_sb_t, batch, seq, n_sb, first_group=4)
    w_out = w_out_even[0].astype(BF16)
    h1, xn1 = _outproj([y_ret, y_sb], [w_out[:ret_width], w_out[ret_width:]], h0, ffn_norm_even[0])
    h2, xn2 = _ffn_dense(xn1, w_gate_even[0].astype(BF16), w_up_even[0].astype(BF16),
                         w_down_even[0].astype(BF16), h1, attn_norm_odd[0])

    w_in = w_in_odd[0]
    proj = _mm(xn2, w_in[:, :2 * fox_width].astype(BF16), BF16)
    v_fox_t = _mm_nt(xn2, w_in[:, 2 * fox_width:3 * fox_width].T.astype(BF16), sub)
    w_f = jnp.zeros((d, LANES), BF16).at[:, :n_fox].set(w_in[:, 3 * fox_width:].astype(BF16))
    f_logit = _mm(xn2, w_f, F32)
    b_f = jnp.zeros((1, LANES), F32).at[0, :n_fox].set(b_forget_odd[0])
    cum_cols = _forget_cum(f_logit, b_f, batch, seq)
    y_fox = _forgetting_attention(proj, v_fox_t, cum_cols, batch, seq, n_fox)
    h3 = _outproj([y_fox], [w_out_odd[0].astype(BF16)], h2)

    xn3, route = _router(h3, ffn_norm_odd[0], w_router_odd[0])
    expert_idx = route[:, :TOP_K].astype(jnp.int32)
    dest, tile_expert, n_used, n_rows = _routing_plan(expert_idx, FFN_TM)
    x_sorted = _scatter_rows(xn3, dest, n_rows)
    y_sorted = _ffn_grouped(x_sorted, w_gate_moe_odd[0].astype(BF16), w_up_moe_odd[0].astype(BF16),
                            w_down_moe_odd[0].astype(BF16), tile_expert, n_used)
    out = _combine(y_sorted, dest, h3, route, final_norm)
    return out.reshape(batch, seq, d)
```

```python
import functools

import jax
import jax.numpy as jnp
import numpy as np
from jax import lax
from jax.experimental import pallas as pl
from jax.experimental.pallas import tpu as pltpu

F32 = jnp.float32
BF16 = jnp.bfloat16

LANES = 128
HEAD_DIM = 64
HEADS_PER_BLOCK = LANES // HEAD_DIM
N_EXPERTS = 8
TOP_K = 2
ROPE_BASE = 10000.0
NORM_EPS = 1e-6
GROUP_NORM_EPS = 1e-5
LOG2E = 1.4426950408889634
UNDERFLOW_BITS = 160.0
FOX_MARGIN_BITS = 8.0
NORM_SLACK = 1.02
RET_CHUNK = 256
VMEM_LIMIT = 56 * 1024 * 1024

MM_TM = 1024
MM_TN = 1024
ROW_TM = 1024
ATT_T = 512
ATT_COLS = 2
RET_TS = 2048
FFN_TM = 256
FFN_NF = 1
GATHER_TM = 512
DMA_UNROLL = 8
N_BIAS_PIECES = 3
FOX_VALUE_ROWS = HEAD_DIM + 16


def _cparams(sem, vmem=VMEM_LIMIT):
    return pltpu.CompilerParams(dimension_semantics=sem, vmem_limit_bytes=vmem)


def _rms(xf, gain_row):
    ms = jnp.mean(xf * xf, axis=-1, keepdims=True)
    return xf * lax.rsqrt(ms + NORM_EPS) * gain_row


def _dot(a, b):
    return jnp.dot(a, b, preferred_element_type=F32)


def _dot_nt(a, b):
    return lax.dot_general(a, b, (((1,), (1,)), ((), ())), preferred_element_type=F32)


def _dot_tn(a, b):
    return lax.dot_general(a, b, (((0,), (0,)), ((), ())), preferred_element_type=F32)


def _norm_mm_kernel(x_ref, g_ref, w_ref, o_ref, xn_ref):
    @pl.when(pl.program_id(1) == 0)
    def _():
        xn_ref[...] = _rms(x_ref[...], g_ref[...]).astype(BF16)

    o_ref[...] = _dot(xn_ref[...], w_ref[...]).astype(o_ref.dtype)


def _norm_mm(x, gain, w, out_dtype):
    t, k = x.shape
    n = w.shape[1]
    tm, tn = min(MM_TM, t), min(MM_TN, n)
    return pl.pallas_call(
        _norm_mm_kernel,
        grid=(t // tm, n // tn),
        in_specs=[pl.BlockSpec((tm, k), lambda i, j: (i, 0)),
                  pl.BlockSpec((1, k), lambda i, j: (0, 0)),
                  pl.BlockSpec((k, tn), lambda i, j: (0, j))],
        out_specs=[pl.BlockSpec((tm, tn), lambda i, j: (i, j)),
                   pl.BlockSpec((tm, k), lambda i, j: (i, 0))],
        out_shape=[jax.ShapeDtypeStruct((t, n), out_dtype), jax.ShapeDtypeStruct((t, k), BF16)],
        compiler_params=_cparams(("parallel", "arbitrary")),
        name="norm_mm",
    )(x, gain.reshape(1, k), w)


def _mm_kernel(x_ref, w_ref, o_ref):
    o_ref[...] = _dot(x_ref[...], w_ref[...]).astype(o_ref.dtype)


def _mm(x, w, out_dtype):
    t, k = x.shape
    n = w.shape[1]
    tm, tn = min(MM_TM, t), min(MM_TN, n)
    return pl.pallas_call(
        _mm_kernel,
        grid=(t // tm, n // tn),
        in_specs=[pl.BlockSpec((tm, k), lambda i, j: (i, 0)),
                  pl.BlockSpec((k, tn), lambda i, j: (0, j))],
        out_specs=pl.BlockSpec((tm, tn), lambda i, j: (i, j)),
        out_shape=jax.ShapeDtypeStruct((t, n), out_dtype),
        compiler_params=_cparams(("parallel", "parallel")),
        name="mm",
    )(x, w)


def _mm_nt_kernel(x_ref, wt_ref, o_ref, *, sub):
    res = _dot_nt(wt_ref[...], x_ref[...]).astype(o_ref.dtype)
    for s in range(o_ref.shape[0]):
        o_ref[s] = res[:, s * sub:(s + 1) * sub]


def _mm_nt(x, wt, sub):
    t, k = x.shape
    n = wt.shape[0]
    tm, tn = min(MM_TM, t), min(MM_TN, n)
    return pl.pallas_call(
        functools.partial(_mm_nt_kernel, sub=sub),
        grid=(t // tm, n // tn),
        in_specs=[pl.BlockSpec((tm, k), lambda i, j: (i, 0)),
                  pl.BlockSpec((tn, k), lambda i, j: (j, 0))],
        out_specs=pl.BlockSpec((tm // sub, tn, sub), lambda i, j: (i, j, 0)),
        out_shape=jax.ShapeDtypeStruct((t // sub, n, sub), BF16),
        compiler_params=_cparams(("parallel", "parallel")),
        name="mm_nt",
    )(x, wt)


def _outproj_kernel(*refs, n_in, with_norm):
    ys = refs[:n_in]
    ws = refs[n_in:2 * n_in]
    h_ref = refs[2 * n_in]
    pos = 2 * n_in + 1
    acc = h_ref[...]
    for y_ref, w_ref in zip(ys, ws):
        acc = acc + _dot(y_ref[...], w_ref[...])
    if with_norm:
        g_ref, ho_ref, xn_ref = refs[pos], refs[pos + 1], refs[pos + 2]
        ho_ref[...] = acc
        xn_ref[...] = _rms(acc, g_ref[...]).astype(xn_ref.dtype)
    else:
        refs[pos][...] = acc


def _outproj(ys, ws, h, gain=None):
    t, d = h.shape
    tm = min(ROW_TM, t)
    n_in = len(ys)
    with_norm = gain is not None
    in_specs = [pl.BlockSpec((tm, y.shape[1]), lambda i: (i, 0)) for y in ys]
    in_specs += [pl.BlockSpec(w.shape, lambda i: (0, 0)) for w in ws]
    in_specs += [pl.BlockSpec((tm, d), lambda i: (i, 0))]
    args = list(ys) + list(ws) + [h]
    row_spec = pl.BlockSpec((tm, d), lambda i: (i, 0))
    if with_norm:
        in_specs += [pl.BlockSpec((1, d), lambda i: (0, 0))]
        args += [gain.reshape(1, d)]
        out_specs = [row_spec, row_spec]
        out_shape = [jax.ShapeDtypeStruct((t, d), F32), jax.ShapeDtypeStruct((t, d), BF16)]
    else:
        out_specs = row_spec
        out_shape = jax.ShapeDtypeStruct((t, d), F32)
    return pl.pallas_call(
        functools.partial(_outproj_kernel, n_in=n_in, with_norm=with_norm),
        grid=(t // tm,),
        in_specs=in_specs,
        out_specs=out_specs,
        out_shape=out_shape,
        compiler_params=_cparams(("parallel",)),
        name="outproj",
    )(*args)


def _retention_tables(seq, n_heads):
    half = HEAD_DIM // 2
    lane = np.arange(LANES)
    inv_freq = ROPE_BASE ** (-jnp.arange(half, dtype=F32) / half)
    ang = jnp.arange(seq, dtype=F32)[:, None] * inv_freq[None, :]
    cos, sin = jnp.cos(ang), jnp.sin(ang)
    cos_t = jnp.tile(cos, (1, LANES // half))
    sign = np.where((lane % HEAD_DIM) < half, -1.0, 1.0).astype(np.float32)
    sin_t = jnp.tile(sin, (1, LANES // half)) * sign[None, :]
    c = RET_CHUNK
    log_gamma = jnp.log(1.0 - 2.0 ** (-5.0 - jnp.arange(n_heads, dtype=F32)))
    pos = jnp.arange(c, dtype=F32)
    diff = pos[:, None] - pos[None, :]
    intra = jnp.where(diff >= 0.0,
                      jnp.exp(log_gamma[:, None, None] * jnp.maximum(diff, 0.0)), 0.0)
    intra = intra.reshape(n_heads // 2, 2, c, c)
    q_decay = jnp.exp(log_gamma[:, None] * (pos + 1.0))
    k_decay = jnp.exp(log_gamma[:, None] * (c - 1.0 - pos))
    chunk_decay = jnp.exp(log_gamma * c)

    def per_lane(tab):
        tab = tab.reshape(n_heads // 2, 2, c)
        return jnp.repeat(tab.transpose(0, 2, 1), HEAD_DIM, axis=2)

    head_of = lane // HEAD_DIM
    same = (head_of[:, None] == head_of[None, :]).astype(np.float32)
    cd = chunk_decay.reshape(n_heads // 2, 2)
    cd_rows = jnp.repeat(cd, HEAD_DIM, axis=1)
    state_decay = cd_rows[:, :, None] * same[None]
    return cos_t, sin_t, intra, per_lane(q_decay), per_lane(k_decay), state_decay, jnp.asarray(same)


def _retention_kernel(q_ref, k_ref, v_ref, g_ref, cos_ref, sin_ref, intra_ref, qd_ref, kd_ref,
                      sd_ref, same_ref, rn_ref, o_ref, state_ref, *, ts):
    @pl.when(pl.program_id(2) == 0)
    def _():
        state_ref[...] = jnp.zeros_like(state_ref)

    c = RET_CHUNK
    lane = lax.broadcasted_iota(jnp.int32, (1, LANES), 1)
    first_half = (lane % HEAD_DIM) < (HEAD_DIM // 2)
    head0 = lane < HEAD_DIM

    def rot(t, cos, sin):
        swapped = jnp.where(first_half, pltpu.roll(t, LANES - HEAD_DIM // 2, 1),
                            pltpu.roll(t, HEAD_DIM // 2, 1))
        return t * cos + swapped * sin

    for ci in range(ts // c):
        rows = slice(ci * c, (ci + 1) * c)
        cos, sin = cos_ref[rows, :], sin_ref[rows, :]
        q = rot(q_ref[rows, :].astype(F32), cos, sin)
        k = rot(k_ref[rows, :].astype(F32), cos, sin) * (HEAD_DIM ** -0.5)
        v = v_ref[rows, :]
        kb = k.astype(BF16)
        inner = []
        for hd in range(HEADS_PER_BLOCK):
            hmask = head0 if hd == 0 else jnp.logical_not(head0)
            qh = jnp.where(hmask, q, 0.0).astype(BF16)
            scores = _dot_nt(qh, kb) * intra_ref[hd]
            inner.append(_dot(scores.astype(BF16), v))
        state = state_ref[...]
        cross = _dot((q * qd_ref[...]).astype(BF16), state.astype(BF16))
        y = jnp.where(head0, inner[0], inner[1]) + cross
        kv = _dot_tn((k * kd_ref[...]).astype(BF16), v)
        state_ref[...] = state * sd_ref[...] + kv * same_ref[...]

        s0 = jnp.sum(jnp.where(head0, y, 0.0), axis=1, keepdims=True)
        s1 = jnp.sum(jnp.where(head0, 0.0, y), axis=1, keepdims=True)
        d = y - jnp.where(head0, s0, s1) * (1.0 / HEAD_DIM)
        dd = d * d
        v0 = jnp.sum(jnp.where(head0, dd, 0.0), axis=1, keepdims=True)
        v1 = jnp.sum(jnp.where(head0, 0.0, dd), axis=1, keepdims=True)
        var = jnp.where(head0, v0, v1) * (1.0 / HEAD_DIM)
        g = g_ref[rows, :].astype(F32)
        silu = g * (1.0 / (1.0 + jnp.exp(-g)))
        o_ref[rows, :] = (d * lax.rsqrt(var + GROUP_NORM_EPS) * rn_ref[...] * silu).astype(o_ref.dtype)


def _retention(proj, ret_norm, batch, seq, n_heads):
    t = proj.shape[0]
    width = n_heads * HEAD_DIM
    nb = width // LANES
    ts = min(RET_TS, seq)
    ns = seq // ts
    tabs = _retention_tables(seq, n_heads)
    cos_t, sin_t, intra, qd, kd, sd, same = tabs
    c = RET_CHUNK

    def col(group):
        return pl.BlockSpec((ts, LANES), lambda b, hp, si: (b * ns + si, group * nb + hp))

    in_specs = [col(0), col(1), col(2), col(3),
                pl.BlockSpec((ts, LANES), lambda b, hp, si: (si, 0)),
                pl.BlockSpec((ts, LANES), lambda b, hp, si: (si, 0)),
                pl.BlockSpec((None, 2, c, c), lambda b, hp, si: (hp, 0, 0, 0)),
                pl.BlockSpec((None, c, LANES), lambda b, hp, si: (hp, 0, 0)),
                pl.BlockSpec((None, c, LANES), lambda b, hp, si: (hp, 0, 0)),
                pl.BlockSpec((None, LANES, LANES), lambda b, hp, si: (hp, 0, 0)),
                pl.BlockSpec((LANES, LANES), lambda b, hp, si: (0, 0)),
                pl.BlockSpec((1, LANES), lambda b, hp, si: (0, hp))]
    return pl.pallas_call(
        functools.partial(_retention_kernel, ts=ts),
        grid=(batch, nb, ns),
        in_specs=in_specs,
        out_specs=pl.BlockSpec((ts, LANES), lambda b, hp, si: (b * ns + si, hp)),
        out_shape=jax.ShapeDtypeStruct((t, width), BF16),
        scratch_shapes=[pltpu.VMEM((LANES, LANES), F32)],
        compiler_params=_cparams(("parallel", "parallel", "arbitrary")),
        name="retention",
    )(proj, proj, proj, proj, cos_t, sin_t, intra, qd, kd, sd, same, ret_norm.reshape(1, width))


def _head_masks():
    lane = lax.broadcasted_iota(jnp.int32, (1, LANES), 1)
    head0 = lane < HEAD_DIM
    return lane, [head0, jnp.logical_not(head0)]


def _col_block(x, hd):
    cb = hd // HEADS_PER_BLOCK
    return x[:, cb * LANES:(cb + 1) * LANES]


def _head_rows(x, hd):
    return x[hd * HEAD_DIM:(hd + 1) * HEAD_DIM]


def _store_heads(o_ref, acc_t):
    for cb in range(ATT_COLS):
        pair = jnp.concatenate(acc_t[HEADS_PER_BLOCK * cb:HEADS_PER_BLOCK * (cb + 1)], axis=0)
        o_ref[:, cb * LANES:(cb + 1) * LANES] = pair.T.astype(o_ref.dtype)


def _two_stage_blocks(qi, sub, scores_to, apply_from, carry, upper_diag_first, rest_is_zero=None):
    top = 2 * qi + 1
    first, second = ((top, sub), (top - 1, 0)) if upper_diag_first else ((top - 1, 0), (top, sub))
    scores_to(0, *first)
    scores_to(1, *second)
    carry = apply_from(0, first[0], carry)

    def more(state):
        i, carry = state
        if rest_is_zero is None:
            return i < qi
        return jnp.logical_and(i < qi, jnp.logical_not(rest_is_zero(carry, top - 2 - 2 * i)))

    def pair(state):
        i, carry = state
        jb = top - 2 - 2 * i
        scores_to(0, jb, None)
        carry = apply_from(1, jnp.where(i == 0, second[0], jb + 1), carry)
        scores_to(1, jb - 1, None)
        return i + 1, apply_from(0, jb, carry)

    n_pairs, carry = lax.while_loop(more, pair, (jnp.int32(0), carry))
    last = jnp.where(n_pairs == 0, second[0], top - 1 - 2 * n_pairs)
    if rest_is_zero is None:
        return apply_from(1, last, carry)
    skip = jnp.logical_and(n_pairs > 0, rest_is_zero(carry, last))
    return lax.cond(skip, lambda c: c, lambda c: apply_from(1, last, c), carry)


def _sb_kernel(q_ref, k_ref, vt_ref, o_ref, d0_ref, d1_ref, tot0_ref, tot1_ref, *, tq, sub):
    qi = pl.program_id(2)
    d_refs = (d0_ref, d1_ref)
    tot_refs = (tot0_ref, tot1_ref)
    _, hmasks = _head_masks()
    key = lax.broadcasted_iota(jnp.int32, (sub, tq), 0)
    qry = lax.broadcasted_iota(jnp.int32, (sub, tq), 1)
    r = lax.broadcasted_iota(jnp.int32, (sub, sub), 0)
    c = lax.broadcasted_iota(jnp.int32, (sub, sub), 1)
    suffix = jnp.where(c >= r, 1.0, 0.0).astype(BF16)
    qf = q_ref[...].astype(F32) * (HEAD_DIM ** -0.5 * LOG2E)
    heads = range(ATT_COLS * HEADS_PER_BLOCK)
    qh = [jnp.where(hmasks[hd % HEADS_PER_BLOCK], _col_block(qf, hd), 0.0).astype(BF16) for hd in heads]

    sign_bit = jnp.uint32(0x80000000)

    def scores_to(buf, jb, diag_off):
        k = k_ref[jb]
        for hd in heads:
            z = _dot_nt(_col_block(k, hd), qh[hd])
            if diag_off is not None:
                z = jnp.where((key + diag_off) < qry, z, -jnp.inf)
            neg_abs = lax.bitcast_convert_type(lax.bitcast_convert_type(z, jnp.uint32) | sign_bit, F32)
            fail = jnp.maximum(z, 0.0) + jnp.log2(1.0 + jnp.exp2(neg_abs))
            tail = _dot(suffix, fail.astype(BF16))
            d_refs[buf][hd] = jnp.minimum(z - tail, 0.0)
            tot_refs[buf][hd] = tail[0:1, :]

    def apply_from(buf, jb, carry):
        vt = vt_ref[jb]
        out = []
        for hd in heads:
            later, acc = carry[hd]
            w = jnp.exp2(d_refs[buf][hd] - later)
            out.append((later + tot_refs[buf][hd], acc + _dot(_head_rows(vt, hd), w.astype(BF16))))
        return out

    def rest_is_zero(carry, _):
        least = carry[0][0]
        for hd in heads[1:]:
            least = jnp.minimum(least, carry[hd][0])
        return jnp.min(least) > UNDERFLOW_BITS

    carry = [(jnp.zeros((1, tq), F32), jnp.zeros((HEAD_DIM, tq), F32)) for _ in heads]
    carry = _two_stage_blocks(qi, sub, scores_to, apply_from, carry, upper_diag_first=True,
                              rest_is_zero=rest_is_zero)
    _store_heads(o_ref, [c[1] for c in carry])


def _stick_breaking(proj, v_t, batch, seq, n_heads, first_group):
    t = proj.shape[0]
    width = n_heads * HEAD_DIM
    nb = width // LANES
    tq = min(ATT_T, seq)
    sub = tq // 2
    nq = seq // tq
    nk = seq // sub
    assert v_t.shape[2] == sub
    proj_k = proj.reshape(t // sub, sub, proj.shape[1])
    n_step = ATT_COLS * HEADS_PER_BLOCK
    cols = ATT_COLS * LANES
    ng = nb // ATT_COLS
    score_buf = pltpu.VMEM((n_step, sub, tq), F32)
    total_buf = pltpu.VMEM((n_step, 1, tq), F32)
    q_spec = pl.BlockSpec((tq, cols), lambda b, hp, qi: (b * nq + qi, first_group * ng + hp))
    k_spec = pl.BlockSpec((nk, sub, cols), lambda b, hp, qi: (b, 0, (first_group + 1) * ng + hp))
    v_spec = pl.BlockSpec((nk, cols, sub), lambda b, hp, qi: (b, hp, 0))
    return pl.pallas_call(
        functools.partial(_sb_kernel, tq=tq, sub=sub),
        grid=(batch, ng, nq),
        in_specs=[q_spec, k_spec, v_spec],
        out_specs=pl.BlockSpec((tq, cols), lambda b, hp, qi: (b * nq + qi, hp)),
        out_shape=jax.ShapeDtypeStruct((t, width), BF16),
        scratch_shapes=[score_buf, score_buf, total_buf, total_buf],
        compiler_params=_cparams(("parallel", "parallel", "arbitrary")),
        name="stick_breaking",
    )(proj, proj_k, v_t)


def _forget_cum_kernel(f_ref, b_ref, col_ref, *, seq):
    x = f_ref[...] + b_ref[...]
    log_f = jnp.minimum(x, 0.0) - jnp.log(1.0 + jnp.exp(-jnp.abs(x)))
    xt = log_f.T
    pos = lax.broadcasted_iota(jnp.int32, xt.shape, 1)
    shift = 1
    while shift < seq:
        xt = xt + jnp.where(pos >= shift, pltpu.roll(xt, shift, 1), 0.0)
        shift *= 2
    col_ref[...] = xt.T


def _forget_cum(f_logit, b_forget, batch, seq):
    t = f_logit.shape[0]
    return pl.pallas_call(
        functools.partial(_forget_cum_kernel, seq=seq),
        grid=(batch,),
        in_specs=[pl.BlockSpec((seq, LANES), lambda b: (b, 0)),
                  pl.BlockSpec((1, LANES), lambda b: (0, 0))],
        out_specs=pl.BlockSpec((seq, LANES), lambda b: (b, 0)),
        out_shape=jax.ShapeDtypeStruct((t, LANES), F32),
        compiler_params=_cparams(("parallel",)),
        name="forget_cum",
    )(f_logit, b_forget)


def _bias_lanes(hd):
    return HEAD_DIM * (1 - hd)


def _bias_tiles(cum, hp, is_query):
    n_step = ATT_COLS * HEADS_PER_BLOCK
    pieces = []
    rest = cum * LOG2E
    for _ in range(N_BIAS_PIECES):
        p = rest.astype(BF16)
        pieces.append(p)
        rest = rest - p.astype(F32)
    stacked = jnp.concatenate(pieces, axis=1)
    row = lax.broadcasted_iota(jnp.int32, (N_BIAS_PIECES * LANES, 1), 0)
    piece = row >> 7
    head = (row & (LANES - 1)) - n_step * hp
    first = 0 if is_query else N_BIAS_PIECES
    target = (head >> 1) * LANES + (1 - (head & 1)) * HEAD_DIM + first + piece
    target = jnp.where(jnp.logical_and(head >= 0, head < n_step), target, -1)
    col = lax.broadcasted_iota(jnp.int32, (1, ATT_COLS * LANES), 1)
    selector = jnp.where(col == target, 1.0 if is_query else -1.0, 0.0).astype(BF16)
    tile = _dot(stacked, selector)
    ones_first = N_BIAS_PIECES if is_query else 0
    in_half = col & (HEAD_DIM - 1)
    ones = jnp.logical_and(in_half >= ones_first, in_half < ones_first + N_BIAS_PIECES)
    return jnp.where(ones, 1.0, tile)


def _fox_kernel(q_ref, k_ref, vt_ref, cq_ref, ck_ref, o_ref, kp_ref, vp_ref, ksq_ref, s0_ref, s1_ref,
                max0_ref, max1_ref, *, tq, sub, seq):
    hp = pl.program_id(1)
    qi = pl.program_id(2)
    s_refs = (s0_ref, s1_ref)
    max_refs = (max0_ref, max1_ref)
    nk = seq // sub
    lane, hmasks = _head_masks()
    heads = range(ATT_COLS * HEADS_PER_BLOCK)

    def with_bias_lanes(x, bias_tile, hd):
        return jnp.where(hmasks[hd % HEADS_PER_BLOCK], _col_block(x, hd), _col_block(bias_tile, hd)).astype(BF16)

    @pl.when(qi == 0)
    def _():
        kf = k_ref[...].astype(F32)
        k_bias = _bias_tiles(ck_ref[...], hp, False)
        extra = lax.broadcasted_iota(jnp.int32, (FOX_VALUE_ROWS - HEAD_DIM, sub), 0)
        ones = jnp.where(extra == 0, 1.0, 0.0).astype(BF16)
        for hd in heads:
            side = hd % HEADS_PER_BLOCK
            k_hd = _col_block(kf, hd)
            kp_ref[hd] = with_bias_lanes(kf, k_bias, hd)
            k_sq = jnp.sum(jnp.where(hmasks[side], k_hd * k_hd, 0.0), axis=1, keepdims=True)
            ksq_ref[hd] = jnp.broadcast_to(jnp.max(k_sq, axis=0, keepdims=True), (1, LANES))
            for jb in range(nk):
                vp_ref[hd, jb, 0:HEAD_DIM] = _head_rows(vt_ref[jb], hd)
                vp_ref[hd, jb, HEAD_DIM:FOX_VALUE_ROWS] = ones

    qf = q_ref[...].astype(F32) * (HEAD_DIM ** -0.5 * LOG2E)
    q_bias = _bias_tiles(cq_ref[...], hp, True)
    qh = [with_bias_lanes(qf, q_bias, hd) for hd in heads]
    key = lax.broadcasted_iota(jnp.int32, (sub, tq), 0)
    qry = lax.broadcasted_iota(jnp.int32, (sub, tq), 1)

    pick = lax.broadcasted_iota(jnp.int32, (8, LANES), 0)
    pick_lane = lax.broadcasted_iota(jnp.int32, (8, LANES), 1)
    head_rows = jnp.where(pick == pick_lane // HEAD_DIM, 1.0, 0.0).astype(BF16)
    reach = []
    for hd in heads:
        side = hd % HEADS_PER_BLOCK
        q_hd = _col_block(qf, hd)
        q_sq = _dot_nt(head_rows, (q_hd * q_hd).astype(BF16))[side:side + 1]
        spare = _bias_lanes(side)
        bias_row = jnp.where((pick_lane >= spare) & (pick_lane < spare + N_BIAS_PIECES), 1.0, 0.0).astype(BF16)
        cq_row = _dot_nt(bias_row, qh[hd])[0:1]
        reach.append(NORM_SLACK * jnp.sqrt(q_sq * ksq_ref[hd][:, 0:1]) + cq_row)

    def scores_to(buf, jb, diag_off):
        ks = pl.multiple_of(jb * sub, sub)
        for hd in heads:
            s = _dot_nt(kp_ref[hd, pl.ds(ks, sub), :], qh[hd])
            if diag_off is not None:
                s = jnp.where((key + diag_off) <= qry, s, -jnp.inf)
            s_refs[buf][hd] = s
            max_refs[buf][hd] = jnp.max(s, axis=0, keepdims=True)

    def apply_from(buf, jb, carry):
        out = []
        for hd in heads:
            m, acc = carry[hd]
            m_new = jnp.maximum(m, max_refs[buf][hd])
            p = jnp.exp2(s_refs[buf][hd] - m_new).astype(BF16)
            out.append((m_new, jnp.exp2(m - m_new) * acc + _dot(vp_ref[hd, jb], p)))
        return out

    def rest_is_zero(carry, jb):
        ck_row = ck_ref[pl.ds(jnp.maximum((jb + 1) * sub - 1, 0), 1), :]
        worst = None
        for hd in heads:
            sel = lane == ATT_COLS * HEADS_PER_BLOCK * hp + hd
            ck_last = jnp.sum(jnp.where(sel, ck_row, 0.0), axis=1, keepdims=True) * LOG2E
            gap = reach[hd] - ck_last - carry[hd][0]
            worst = gap if worst is None else jnp.maximum(worst, gap)
        return jnp.max(worst) < -(UNDERFLOW_BITS + FOX_MARGIN_BITS)

    carry = [(jnp.full((1, tq), -jnp.inf, F32), jnp.zeros((FOX_VALUE_ROWS, tq), F32)) for _ in heads]
    carry = _two_stage_blocks(qi, sub, scores_to, apply_from, carry, upper_diag_first=False,
                              rest_is_zero=rest_is_zero)
    normed = []
    for hd in heads:
        acc = carry[hd][1]
        normed.append(acc[0:HEAD_DIM] * (1.0 / acc[HEAD_DIM:HEAD_DIM + 1, :]))
    _store_heads(o_ref, normed)


def _forgetting_attention(proj, v_t, cum_cols, batch, seq, n_heads):
    t = proj.shape[0]
    width = n_heads * HEAD_DIM
    nb = width // LANES
    tq = min(ATT_T, seq)
    sub = tq // 2
    nq = seq // tq
    nk = seq // sub
    assert v_t.shape[2] == sub
    n_step = ATT_COLS * HEADS_PER_BLOCK
    cols = ATT_COLS * LANES
    ng = nb // ATT_COLS
    score_buf = pltpu.VMEM((n_step, sub, tq), F32)
    max_buf = pltpu.VMEM((n_step, 1, tq), F32)
    q_spec = pl.BlockSpec((tq, cols), lambda b, hp, qi: (b * nq + qi, hp))
    k_spec = pl.BlockSpec((seq, cols), lambda b, hp, qi: (b, ng + hp))
    v_spec = pl.BlockSpec((nk, cols, sub), lambda b, hp, qi: (b, hp, 0))
    cq_spec = pl.BlockSpec((tq, LANES), lambda b, hp, qi: (b * nq + qi, 0))
    ck_spec = pl.BlockSpec((seq, LANES), lambda b, hp, qi: (b, 0))
    return pl.pallas_call(
        functools.partial(_fox_kernel, tq=tq, sub=sub, seq=seq),
        grid=(batch, ng, nq),
        in_specs=[q_spec, k_spec, v_spec, cq_spec, ck_spec],
        out_specs=pl.BlockSpec((tq, cols), lambda b, hp, qi: (b * nq + qi, hp)),
        out_shape=jax.ShapeDtypeStruct((t, width), BF16),
        scratch_shapes=[pltpu.VMEM((n_step, seq, LANES), BF16),
                        pltpu.VMEM((n_step, nk, FOX_VALUE_ROWS, sub), BF16),
                        pltpu.VMEM((n_step, 1, LANES), F32),
                        score_buf, score_buf, max_buf, max_buf],
        compiler_params=_cparams(("parallel", "parallel", "arbitrary")),
        name="forgetting_attention",
    )(proj, proj, v_t, cum_cols, cum_cols)


def _swiglu_partial(x, wg_ref, wu_ref, wd_ref):
    g = _dot(x, wg_ref[...])
    u = _dot(x, wu_ref[...])
    a = g * (1.0 / (1.0 + jnp.exp(-g))) * u
    return _dot(a.astype(BF16), wd_ref[...])


def _ffn_dense_kernel(x_ref, wg_ref, wu_ref, wd_ref, h_ref, gain_ref, ho_ref, xn_ref, acc_ref):
    f = pl.program_id(1)

    @pl.when(f == 0)
    def _():
        acc_ref[...] = h_ref[...]

    acc_ref[...] += _swiglu_partial(x_ref[...], wg_ref, wu_ref, wd_ref)

    @pl.when(f == pl.num_programs(1) - 1)
    def _():
        h_new = acc_ref[...]
        ho_ref[...] = h_new
        xn_ref[...] = _rms(h_new, gain_ref[...]).astype(xn_ref.dtype)


def _ffn_dense(xn, wg, wu, wd, h, next_gain):
    t, d = h.shape
    dff = wg.shape[1]
    tm = min(FFN_TM, t)
    tf = dff // FFN_NF
    row = pl.BlockSpec((tm, d), lambda i, f: (i, 0))
    return pl.pallas_call(
        _ffn_dense_kernel,
        grid=(t // tm, FFN_NF),
        in_specs=[row,
                  pl.BlockSpec((d, tf), lambda i, f: (0, f)),
                  pl.BlockSpec((d, tf), lambda i, f: (0, f)),
                  pl.BlockSpec((tf, d), lambda i, f: (f, 0)),
                  row,
                  pl.BlockSpec((1, d), lambda i, f: (0, 0))],
        out_specs=[row, row],
        out_shape=[jax.ShapeDtypeStruct((t, d), F32), jax.ShapeDtypeStruct((t, d), BF16)],
        scratch_shapes=[pltpu.VMEM((tm, d), F32)],
        compiler_params=_cparams(("parallel", "arbitrary")),
        name="ffn_dense",
    )(xn, wg, wu, wd, h, next_gain.reshape(1, d))


def _ffn_grouped_kernel(te_ref, nu_ref, x_ref, wg_ref, wu_ref, wd_ref, o_ref, acc_ref):
    i = pl.program_id(0)
    f = pl.program_id(1)
    used = i < nu_ref[0]

    @pl.when(jnp.logical_and(used, f == 0))
    def _():
        acc_ref[...] = jnp.zeros_like(acc_ref)

    @pl.when(used)
    def _():
        acc_ref[...] += _swiglu_partial(x_ref[...].astype(BF16), wg_ref, wu_ref, wd_ref)

    @pl.when(f == pl.num_programs(1) - 1)
    def _():
        @pl.when(used)
        def _():
            o_ref[...] = acc_ref[...]

        @pl.when(jnp.logical_not(used))
        def _():
            o_ref[...] = jnp.zeros_like(o_ref)


def _ffn_grouped(x_sorted, wg, wu, wd, tile_expert, n_used):
    r, d = x_sorted.shape
    dff = wg.shape[2]
    tm = FFN_TM
    tf = dff // FFN_NF
    grid_spec = pltpu.PrefetchScalarGridSpec(
        num_scalar_prefetch=2,
        grid=(r // tm, FFN_NF),
        in_specs=[pl.BlockSpec((tm, d), lambda i, f, te, nu: (i, 0)),
                  pl.BlockSpec((None, d, tf), lambda i, f, te, nu: (te[i], 0, f)),
                  pl.BlockSpec((None, d, tf), lambda i, f, te, nu: (te[i], 0, f)),
                  pl.BlockSpec((None, tf, d), lambda i, f, te, nu: (te[i], f, 0))],
        out_specs=pl.BlockSpec((tm, d), lambda i, f, te, nu: (i, 0)),
        scratch_shapes=[pltpu.VMEM((tm, d), F32)],
    )
    return pl.pallas_call(
        _ffn_grouped_kernel,
        grid_spec=grid_spec,
        out_shape=jax.ShapeDtypeStruct((r, d), F32),
        compiler_params=_cparams(("arbitrary", "arbitrary")),
        name="ffn_grouped",
    )(tile_expert, n_used, x_sorted, wg, wu, wd)


def _router_kernel(h_ref, gain_ref, wr_ref, xn_ref, route_ref):
    xn = _rms(h_ref[...], gain_ref[...])
    xn_ref[...] = xn
    logits = jnp.dot(xn, wr_ref[...], precision=lax.Precision.HIGHEST, preferred_element_type=F32)
    lane = lax.broadcasted_iota(jnp.int32, logits.shape, 1)
    lane_f = lane.astype(F32)
    lg = jnp.where(lane < N_EXPERTS, logits, -jnp.inf)
    v1 = jnp.max(lg, axis=1, keepdims=True)
    i1 = jnp.min(jnp.where(lg == v1, lane_f, float(LANES)), axis=1, keepdims=True)
    lg2 = jnp.where(lane_f == i1, -jnp.inf, lg)
    v2 = jnp.max(lg2, axis=1, keepdims=True)
    i2 = jnp.min(jnp.where(lg2 == v2, lane_f, float(LANES)), axis=1, keepdims=True)
    e2 = jnp.exp(v2 - v1)
    g1 = 1.0 / (1.0 + e2)
    g2 = e2 / (1.0 + e2)
    route_ref[...] = jnp.where(lane == 0, i1, jnp.where(lane == 1, i2,
                               jnp.where(lane == 2, g1, jnp.where(lane == 3, g2, 0.0))))


def _router(h, gain, w_router):
    t, d = h.shape
    tm = min(ROW_TM, t)
    wr = jnp.zeros((d, LANES), F32).at[:, :N_EXPERTS].set(w_router)
    row = pl.BlockSpec((tm, d), lambda i: (i, 0))
    return pl.pallas_call(
        _router_kernel,
        grid=(t // tm,),
        in_specs=[row, pl.BlockSpec((1, d), lambda i: (0, 0)),
                  pl.BlockSpec((d, LANES), lambda i: (0, 0))],
        out_specs=[row, pl.BlockSpec((tm, LANES), lambda i: (i, 0))],
        out_shape=[jax.ShapeDtypeStruct((t, d), F32), jax.ShapeDtypeStruct((t, LANES), F32)],
        compiler_params=_cparams(("parallel",)),
        name="router",
    )(h, gain.reshape(1, d), wr)


def _scatter_kernel(dest_ref, x_ref, init_hbm, o_hbm, sem, *, tm):
    del init_hbm

    def row_copy(r, kk):
        return pltpu.make_async_copy(x_ref.at[pl.ds(r, 1), :],
                                     o_hbm.at[pl.ds(dest_ref[0, 0, TOP_K * r + kk], 1), :], sem)

    def issue(r, carry):
        for kk in range(TOP_K):
            row_copy(r, kk).start(priority=kk)
        return carry

    lax.fori_loop(0, tm, issue, 0, unroll=DMA_UNROLL)

    for _ in range(TOP_K):
        pltpu.make_async_copy(x_ref, o_hbm.at[pl.ds(0, tm), :], sem).wait()


def _scatter_rows(xn, dest, n_rows):
    t, d = xn.shape
    tm = min(GATHER_TM, t)
    nt = t // tm
    dest3 = dest.reshape(nt, 1, TOP_K * tm)
    init = jnp.zeros((n_rows, d), F32)
    return pl.pallas_call(
        functools.partial(_scatter_kernel, tm=tm),
        grid=(nt,),
        in_specs=[pl.BlockSpec((1, 1, TOP_K * tm), lambda i: (i, 0, 0), memory_space=pltpu.SMEM),
                  pl.BlockSpec((tm, d), lambda i: (i, 0)),
                  pl.BlockSpec(memory_space=pl.ANY)],
        out_specs=pl.BlockSpec(memory_space=pl.ANY),
        out_shape=jax.ShapeDtypeStruct((n_rows, d), F32),
        scratch_shapes=[pltpu.SemaphoreType.DMA(())],
        input_output_aliases={2: 0},
        compiler_params=_cparams(("arbitrary",)),
        name="moe_scatter",
    )(dest3, xn, init)


def _combine_kernel(dcur_ref, dnext_ref, y_hbm, h_ref, route_ref, gain_ref, o_ref, buf, sems, *, tm):
    i = pl.program_id(0)
    n = pl.num_programs(0)
    slot = i % 2

    def row_copy(d_ref, s, r, kk):
        return pltpu.make_async_copy(y_hbm.at[pl.ds(d_ref[0, 0, TOP_K * r + kk], 1), :],
                                     buf.at[s, kk, pl.ds(r, 1), :], sems.at[s])

    def issue(d_ref, s):
        def body(r, carry):
            for kk in range(TOP_K):
                row_copy(d_ref, s, r, kk).start(priority=kk)
            return carry
        lax.fori_loop(0, tm, body, 0, unroll=DMA_UNROLL)

    @pl.when(i == 0)
    def _():
        issue(dcur_ref, slot)

    @pl.when(i + 1 < n)
    def _():
        issue(dnext_ref, 1 - slot)

    for kk in range(TOP_K):
        pltpu.make_async_copy(y_hbm.at[pl.ds(0, tm), :], buf.at[slot, kk], sems.at[slot]).wait()

    route = route_ref[...]
    lane = lax.broadcasted_iota(jnp.int32, route.shape, 1)
    g1 = jnp.sum(jnp.where(lane == 2, route, 0.0), axis=1, keepdims=True)
    g2 = jnp.sum(jnp.where(lane == 3, route, 0.0), axis=1, keepdims=True)
    out = h_ref[...] + (g1 * buf[slot, 0] + g2 * buf[slot, 1])
    o_ref[...] = _rms(out, gain_ref[...])


def _combine(y_sorted, dest, h, route, final_gain):
    t, d = h.shape
    tm = min(GATHER_TM, t)
    nt = t // tm
    dest3 = dest.reshape(nt, 1, TOP_K * tm)
    row = pl.BlockSpec((tm, d), lambda i: (i, 0))
    smem = functools.partial(pl.BlockSpec, (1, 1, TOP_K * tm), memory_space=pltpu.SMEM)
    return pl.pallas_call(
        functools.partial(_combine_kernel, tm=tm),
        grid=(nt,),
        in_specs=[smem(index_map=lambda i: (i, 0, 0)),
                  smem(index_map=lambda i: (jnp.minimum(i + 1, nt - 1), 0, 0)),
                  pl.BlockSpec(memory_space=pl.ANY),
                  row,
                  pl.BlockSpec((tm, LANES), lambda i: (i, 0)),
                  pl.BlockSpec((1, d), lambda i: (0, 0))],
        out_specs=row,
        out_shape=jax.ShapeDtypeStruct((t, d), F32),
        scratch_shapes=[pltpu.VMEM((2, TOP_K, tm, d), F32), pltpu.SemaphoreType.DMA((2,))],
        compiler_params=_cparams(("arbitrary",)),
        name="moe_combine",
    )(dest3, dest3, y_sorted, h, route, final_gain.reshape(1, d))


def _routing_plan(expert_idx, tm):
    t = expert_idx.shape[0]
    flat = expert_idx.reshape(-1)
    onehot = (flat[:, None] == jnp.arange(N_EXPERTS, dtype=jnp.int32)[None, :]).astype(jnp.int32)
    running = jnp.cumsum(onehot, axis=0)
    rank = jnp.sum((running - onehot) * onehot, axis=1)
    counts = running[-1]
    tiles = (counts + tm - 1) // tm
    tile_end = jnp.cumsum(tiles)
    group_start = (tile_end - tiles) * tm
    dest = jnp.sum(onehot * group_start[None, :], axis=1) + rank
    n_tiles = (t * TOP_K) // tm + N_EXPERTS
    n_used = tile_end[-1]
    tile_id = jnp.minimum(jnp.arange(n_tiles, dtype=jnp.int32), n_used - 1)
    tile_expert = jnp.sum((tile_id[:, None] >= tile_end[None, :]).astype(jnp.int32), axis=1)
    return dest.reshape(t, TOP_K).astype(jnp.int32), tile_expert.astype(jnp.int32), \
        n_used.reshape(1).astype(jnp.int32), n_tiles * tm


def kernel(x, attn_norm_even, w_in_even, ret_norm_even, w_out_even, ffn_norm_even, w_gate_even, w_up_even, w_down_even, attn_norm_odd, w_in_odd, b_forget_odd, w_out_odd, ffn_norm_odd, w_router_odd, w_gate_moe_odd, w_up_moe_odd, w_down_moe_odd, final_norm):
    batch, seq, d = x.shape
    t = batch * seq
    n_ret = d // (2 * HEAD_DIM)
    n_sb = d // (2 * HEAD_DIM)
    n_fox = d // HEAD_DIM
    ret_width = n_ret * HEAD_DIM
    sb_width = n_sb * HEAD_DIM
    fox_width = n_fox * HEAD_DIM
    sub = min(ATT_T, seq) // 2
    h0 = x.reshape(t, d)

    w_in = w_in_even[0]
    n_direct = 4 * ret_width + 2 * sb_width
    proj, xn0 = _norm_mm(h0, attn_norm_even[0], w_in[:, :n_direct].astype(BF16), BF16)
    v_sb_t = _mm_nt(xn0, w_in[:, n_direct:].T.astype(BF16), sub)
    y_ret = _retention(proj, ret_norm_even[0], batch, seq, n_ret)
    y_sb = _stick_breaking(proj, v_sb_t, batch, seq, n_sb, first_group=4)
    w_out = w_out_even[0].astype(BF16)
    h1, xn1 = _outproj([y_ret, y_sb], [w_out[:ret_width], w_out[ret_width:]], h0, ffn_norm_even[0])
    h2, xn2 = _ffn_dense(xn1, w_gate_even[0].astype(BF16), w_up_even[0].astype(BF16),
                         w_down_even[0].astype(BF16), h1, attn_norm_odd[0])

    w_in = w_in_odd[0]
    proj = _mm(xn2, w_in[:, :2 * fox_width].astype(BF16), BF16)
    v_fox_t = _mm_nt(xn2, w_in[:, 2 * fox_width:3 * fox_width].T.astype(BF16), sub)
    w_f = jnp.zeros((d, LANES), BF16).at[:, :n_fox].set(w_in[:, 3 * fox_width:].astype(BF16))
    f_logit = _mm(xn2, w_f, F32)
    b_f = jnp.zeros((1, LANES), F32).at[0, :n_fox].set(b_forget_odd[0])
    cum_cols = _forget_cum(f_logit, b_f, batch, seq)
    y_fox = _forgetting_attention(proj, v_fox_t, cum_cols, batch, seq, n_fox)
    h3 = _outproj([y_fox], [w_out_odd[0].astype(BF16)], h2)

    xn3, route = _router(h3, ffn_norm_odd[0], w_router_odd[0])
    expert_idx = route[:, :TOP_K].astype(jnp.int32)
    dest, tile_expert, n_used, n_rows = _routing_plan(expert_idx, FFN_TM)
    x_sorted = _scatter_rows(xn3, dest, n_rows)
    y_sorted = _ffn_grouped(x_sorted, w_gate_moe_odd[0].astype(BF16), w_up_moe_odd[0].astype(BF16),
                            w_down_moe_odd[0].astype(BF16), tile_expert, n_used)
    out = _combine(y_sorted, dest, h3, route, final_norm)
    return out.reshape(batch, seq, d)
```

```python
import functools

import jax
import jax.numpy as jnp
import numpy as np
from jax import lax
from jax.experimental import pallas as pl
from jax.experimental.pallas import tpu as pltpu

F32 = jnp.float32
BF16 = jnp.bfloat16

LANES = 128
HEAD_DIM = 64
HEADS_PER_BLOCK = LANES // HEAD_DIM
N_EXPERTS = 8
TOP_K = 2
ROPE_BASE = 10000.0
NORM_EPS = 1e-6
GROUP_NORM_EPS = 1e-5
LOG2E = 1.4426950408889634
UNDERFLOW_BITS = 160.0
FOX_MARGIN_BITS = 8.0
NORM_SLACK = 1.02
RET_CHUNK = 256
VMEM_LIMIT = 56 * 1024 * 1024

MM_TM = 1024
MM_TN = 1024
ROW_TM = 1024
ROUTER_TM = 512
ATT_T = 512
ATT_COLS = 2
RET_TS = 2048
FFN_TM = 256
FFN_NF = 1
GATHER_TM = 512
DMA_UNROLL = 8
N_BIAS_PIECES = 3
FOX_VALUE_ROWS = HEAD_DIM + 16


def _cparams(sem, vmem=VMEM_LIMIT):
    return pltpu.CompilerParams(dimension_semantics=sem, vmem_limit_bytes=vmem)


def _rms(xf, gain_row):
    ms = jnp.mean(xf * xf, axis=-1, keepdims=True)
    return xf * lax.rsqrt(ms + NORM_EPS) * gain_row


def _dot(a, b):
    return jnp.dot(a, b, preferred_element_type=F32)


def _dot_nt(a, b):
    return lax.dot_general(a, b, (((1,), (1,)), ((), ())), preferred_element_type=F32)


def _dot_tn(a, b):
    return lax.dot_general(a, b, (((0,), (0,)), ((), ())), preferred_element_type=F32)


def _norm_mm_kernel(x_ref, g_ref, w_ref, o_ref, xn_ref):
    @pl.when(pl.program_id(1) == 0)
    def _():
        xn_ref[...] = _rms(x_ref[...], g_ref[...]).astype(BF16)

    o_ref[...] = _dot(xn_ref[...], w_ref[...]).astype(o_ref.dtype)


def _norm_mm(x, gain, w, out_dtype):
    t, k = x.shape
    n = w.shape[1]
    tm, tn = min(MM_TM, t), min(MM_TN, n)
    return pl.pallas_call(
        _norm_mm_kernel,
        grid=(t // tm, n // tn),
        in_specs=[pl.BlockSpec((tm, k), lambda i, j: (i, 0)),
                  pl.BlockSpec((1, k), lambda i, j: (0, 0)),
                  pl.BlockSpec((k, tn), lambda i, j: (0, j))],
        out_specs=[pl.BlockSpec((tm, tn), lambda i, j: (i, j)),
                   pl.BlockSpec((tm, k), lambda i, j: (i, 0))],
        out_shape=[jax.ShapeDtypeStruct((t, n), out_dtype), jax.ShapeDtypeStruct((t, k), BF16)],
        compiler_params=_cparams(("parallel", "arbitrary")),
        name="norm_mm",
    )(x, gain.reshape(1, k), w)


def _mm_kernel(x_ref, w_ref, o_ref):
    o_ref[...] = _dot(x_ref[...], w_ref[...]).astype(o_ref.dtype)


def _mm(x, w, out_dtype):
    t, k = x.shape
    n = w.shape[1]
    tm, tn = min(MM_TM, t), min(MM_TN, n)
    return pl.pallas_call(
        _mm_kernel,
        grid=(t // tm, n // tn),
        in_specs=[pl.BlockSpec((tm, k), lambda i, j: (i, 0)),
                  pl.BlockSpec((k, tn), lambda i, j: (0, j))],
        out_specs=pl.BlockSpec((tm, tn), lambda i, j: (i, j)),
        out_shape=jax.ShapeDtypeStruct((t, n), out_dtype),
        compiler_params=_cparams(("parallel", "parallel")),
        name="mm",
    )(x, w)


def _mm_nt_kernel(x_ref, wt_ref, o_ref, *, sub):
    res = _dot_nt(wt_ref[...], x_ref[...]).astype(o_ref.dtype)
    for s in range(o_ref.shape[0]):
        o_ref[s] = res[:, s * sub:(s + 1) * sub]


def _mm_nt(x, wt, sub):
    t, k = x.shape
    n = wt.shape[0]
    tm, tn = min(MM_TM, t), min(MM_TN, n)
    return pl.pallas_call(
        functools.partial(_mm_nt_kernel, sub=sub),
        grid=(t // tm, n // tn),
        in_specs=[pl.BlockSpec((tm, k), lambda i, j: (i, 0)),
                  pl.BlockSpec((tn, k), lambda i, j: (j, 0))],
        out_specs=pl.BlockSpec((tm // sub, tn, sub), lambda i, j: (i, j, 0)),
        out_shape=jax.ShapeDtypeStruct((t // sub, n, sub), BF16),
        compiler_params=_cparams(("parallel", "parallel")),
        name="mm_nt",
    )(x, wt)


def _outproj_kernel(*refs, n_in, with_norm):
    ys = refs[:n_in]
    ws = refs[n_in:2 * n_in]
    h_ref = refs[2 * n_in]
    pos = 2 * n_in + 1
    acc = h_ref[...]
    for y_ref, w_ref in zip(ys, ws):
        acc = acc + _dot(y_ref[...], w_ref[...])
    if with_norm:
        g_ref, ho_ref, xn_ref = refs[pos], refs[pos + 1], refs[pos + 2]
        ho_ref[...] = acc
        xn_ref[...] = _rms(acc, g_ref[...]).astype(xn_ref.dtype)
    else:
        refs[pos][...] = acc


def _outproj(ys, ws, h, gain=None):
    t, d = h.shape
    tm = min(ROW_TM, t)
    n_in = len(ys)
    with_norm = gain is not None
    in_specs = [pl.BlockSpec((tm, y.shape[1]), lambda i: (i, 0)) for y in ys]
    in_specs += [pl.BlockSpec(w.shape, lambda i: (0, 0)) for w in ws]
    in_specs += [pl.BlockSpec((tm, d), lambda i: (i, 0))]
    args = list(ys) + list(ws) + [h]
    row_spec = pl.BlockSpec((tm, d), lambda i: (i, 0))
    if with_norm:
        in_specs += [pl.BlockSpec((1, d), lambda i: (0, 0))]
        args += [gain.reshape(1, d)]
        out_specs = [row_spec, row_spec]
        out_shape = [jax.ShapeDtypeStruct((t, d), F32), jax.ShapeDtypeStruct((t, d), BF16)]
    else:
        out_specs = row_spec
        out_shape = jax.ShapeDtypeStruct((t, d), F32)
    return pl.pallas_call(
        functools.partial(_outproj_kernel, n_in=n_in, with_norm=with_norm),
        grid=(t // tm,),
        in_specs=in_specs,
        out_specs=out_specs,
        out_shape=out_shape,
        compiler_params=_cparams(("parallel",)),
        name="outproj",
    )(*args)


def _retention_tables(seq, n_heads):
    half = HEAD_DIM // 2
    lane = np.arange(LANES)
    inv_freq = ROPE_BASE ** (-jnp.arange(half, dtype=F32) / half)
    ang = jnp.arange(seq, dtype=F32)[:, None] * inv_freq[None, :]
    cos, sin = jnp.cos(ang), jnp.sin(ang)
    cos_t = jnp.tile(cos, (1, LANES // half))
    sign = np.where((lane % HEAD_DIM) < half, -1.0, 1.0).astype(np.float32)
    sin_t = jnp.tile(sin, (1, LANES // half)) * sign[None, :]
    c = RET_CHUNK
    log_gamma = jnp.log(1.0 - 2.0 ** (-5.0 - jnp.arange(n_heads, dtype=F32)))
    pos = jnp.arange(c, dtype=F32)
    diff = pos[:, None] - pos[None, :]
    intra = jnp.where(diff >= 0.0,
                      jnp.exp(log_gamma[:, None, None] * jnp.maximum(diff, 0.0)), 0.0)
    intra = intra.reshape(n_heads // 2, 2, c, c)
    q_decay = jnp.exp(log_gamma[:, None] * (pos + 1.0))
    k_decay = jnp.exp(log_gamma[:, None] * (c - 1.0 - pos))
    chunk_decay = jnp.exp(log_gamma * c)

    def per_lane(tab):
        tab = tab.reshape(n_heads // 2, 2, c)
        return jnp.repeat(tab.transpose(0, 2, 1), HEAD_DIM, axis=2)

    head_of = lane // HEAD_DIM
    same = (head_of[:, None] == head_of[None, :]).astype(np.float32)
    cd = chunk_decay.reshape(n_heads // 2, 2)
    cd_rows = jnp.repeat(cd, HEAD_DIM, axis=1)
    state_decay = cd_rows[:, :, None] * same[None]
    return cos_t, sin_t, intra, per_lane(q_decay), per_lane(k_decay), state_decay, jnp.asarray(same)


def _retention_kernel(q_ref, k_ref, v_ref, g_ref, cos_ref, sin_ref, intra_ref, qd_ref, kd_ref,
                      sd_ref, same_ref, rn_ref, o_ref, state_ref, *, ts):
    @pl.when(pl.program_id(2) == 0)
    def _():
        state_ref[...] = jnp.zeros_like(state_ref)

    c = RET_CHUNK
    lane = lax.broadcasted_iota(jnp.int32, (1, LANES), 1)
    first_half = (lane % HEAD_DIM) < (HEAD_DIM // 2)
    head0 = lane < HEAD_DIM

    def rot(t, cos, sin):
        swapped = jnp.where(first_half, pltpu.roll(t, LANES - HEAD_DIM // 2, 1),
                            pltpu.roll(t, HEAD_DIM // 2, 1))
        return t * cos + swapped * sin

    for ci in range(ts // c):
        rows = slice(ci * c, (ci + 1) * c)
        cos, sin = cos_ref[rows, :], sin_ref[rows, :]
        q = rot(q_ref[rows, :].astype(F32), cos, sin)
        k = rot(k_ref[rows, :].astype(F32), cos, sin) * (HEAD_DIM ** -0.5)
        v = v_ref[rows, :]
        kb = k.astype(BF16)
        inner = []
        for hd in range(HEADS_PER_BLOCK):
            hmask = head0 if hd == 0 else jnp.logical_not(head0)
            qh = jnp.where(hmask, q, 0.0).astype(BF16)
            scores = _dot_nt(qh, kb) * intra_ref[hd]
            inner.append(_dot(scores.astype(BF16), v))
        state = state_ref[...]
        cross = _dot((q * qd_ref[...]).astype(BF16), state.astype(BF16))
        y = jnp.where(head0, inner[0], inner[1]) + cross
        kv = _dot_tn((k * kd_ref[...]).astype(BF16), v)
        state_ref[...] = state * sd_ref[...] + kv * same_ref[...]

        s0 = jnp.sum(jnp.where(head0, y, 0.0), axis=1, keepdims=True)
        s1 = jnp.sum(jnp.where(head0, 0.0, y), axis=1, keepdims=True)
        d = y - jnp.where(head0, s0, s1) * (1.0 / HEAD_DIM)
        dd = d * d
        v0 = jnp.sum(jnp.where(head0, dd, 0.0), axis=1, keepdims=True)
        v1 = jnp.sum(jnp.where(head0, 0.0, dd), axis=1, keepdims=True)
        var = jnp.where(head0, v0, v1) * (1.0 / HEAD_DIM)
        g = g_ref[rows, :].astype(F32)
        silu = g * (1.0 / (1.0 + jnp.exp(-g)))
        o_ref[rows, :] = (d * lax.rsqrt(var + GROUP_NORM_EPS) * rn_ref[...] * silu).astype(o_ref.dtype)


def _retention(proj, ret_norm, batch, seq, n_heads):
    t = proj.shape[0]
    width = n_heads * HEAD_DIM
    nb = width // LANES
    ts = min(RET_TS, seq)
    ns = seq // ts
    tabs = _retention_tables(seq, n_heads)
    cos_t, sin_t, intra, qd, kd, sd, same = tabs
    c = RET_CHUNK

    def col(group):
        return pl.BlockSpec((ts, LANES), lambda b, hp, si: (b * ns + si, group * nb + hp))

    in_specs = [col(0), col(1), col(2), col(3),
                pl.BlockSpec((ts, LANES), lambda b, hp, si: (si, 0)),
                pl.BlockSpec((ts, LANES), lambda b, hp, si: (si, 0)),
                pl.BlockSpec((None, 2, c, c), lambda b, hp, si: (hp, 0, 0, 0)),
                pl.BlockSpec((None, c, LANES), lambda b, hp, si: (hp, 0, 0)),
                pl.BlockSpec((None, c, LANES), lambda b, hp, si: (hp, 0, 0)),
                pl.BlockSpec((None, LANES, LANES), lambda b, hp, si: (hp, 0, 0)),
                pl.BlockSpec((LANES, LANES), lambda b, hp, si: (0, 0)),
                pl.BlockSpec((1, LANES), lambda b, hp, si: (0, hp))]
    return pl.pallas_call(
        functools.partial(_retention_kernel, ts=ts),
        grid=(batch, nb, ns),
        in_specs=in_specs,
        out_specs=pl.BlockSpec((ts, LANES), lambda b, hp, si: (b * ns + si, hp)),
        out_shape=jax.ShapeDtypeStruct((t, width), BF16),
        scratch_shapes=[pltpu.VMEM((LANES, LANES), F32)],
        compiler_params=_cparams(("parallel", "parallel", "arbitrary")),
        name="retention",
    )(proj, proj, proj, proj, cos_t, sin_t, intra, qd, kd, sd, same, ret_norm.reshape(1, width))


def _head_masks():
    lane = lax.broadcasted_iota(jnp.int32, (1, LANES), 1)
    head0 = lane < HEAD_DIM
    return lane, [head0, jnp.logical_not(head0)]


def _col_block(x, hd):
    cb = hd // HEADS_PER_BLOCK
    return x[:, cb * LANES:(cb + 1) * LANES]


def _head_rows(x, hd):
    return x[hd * HEAD_DIM:(hd + 1) * HEAD_DIM]


def _store_heads(o_ref, acc_t):
    for cb in range(ATT_COLS):
        pair = jnp.concatenate(acc_t[HEADS_PER_BLOCK * cb:HEADS_PER_BLOCK * (cb + 1)], axis=0)
        o_ref[:, cb * LANES:(cb + 1) * LANES] = pair.T.astype(o_ref.dtype)


def _two_stage_blocks(qi, sub, scores_to, apply_from, carry, upper_diag_first, rest_is_zero=None):
    top = 2 * qi + 1
    first, second = ((top, sub), (top - 1, 0)) if upper_diag_first else ((top - 1, 0), (top, sub))
    scores_to(0, *first)
    scores_to(1, *second)
    carry = apply_from(0, first[0], carry)

    def more(state):
        i, carry = state
        if rest_is_zero is None:
            return i < qi
        return jnp.logical_and(i < qi, jnp.logical_not(rest_is_zero(carry, top - 2 - 2 * i)))

    def pair(state):
        i, carry = state
        jb = top - 2 - 2 * i
        scores_to(0, jb, None)
        carry = apply_from(1, jnp.where(i == 0, second[0], jb + 1), carry)
        scores_to(1, jb - 1, None)
        return i + 1, apply_from(0, jb, carry)

    n_pairs, carry = lax.while_loop(more, pair, (jnp.int32(0), carry))
    last = jnp.where(n_pairs == 0, second[0], top - 1 - 2 * n_pairs)
    if rest_is_zero is None:
        return apply_from(1, last, carry)
    skip = jnp.logical_and(n_pairs > 0, rest_is_zero(carry, last))
    return lax.cond(skip, lambda c: c, lambda c: apply_from(1, last, c), carry)


def _sb_kernel(q_ref, k_ref, vt_ref, o_ref, d0_ref, d1_ref, tot0_ref, tot1_ref, *, tq, sub):
    qi = pl.program_id(2)
    d_refs = (d0_ref, d1_ref)
    tot_refs = (tot0_ref, tot1_ref)
    _, hmasks = _head_masks()
    key = lax.broadcasted_iota(jnp.int32, (sub, tq), 0)
    qry = lax.broadcasted_iota(jnp.int32, (sub, tq), 1)
    r = lax.broadcasted_iota(jnp.int32, (sub, sub), 0)
    c = lax.broadcasted_iota(jnp.int32, (sub, sub), 1)
    suffix = jnp.where(c >= r, 1.0, 0.0).astype(BF16)
    qf = q_ref[...].astype(F32) * (HEAD_DIM ** -0.5 * LOG2E)
    heads = range(ATT_COLS * HEADS_PER_BLOCK)
    qh = [jnp.where(hmasks[hd % HEADS_PER_BLOCK], _col_block(qf, hd), 0.0).astype(BF16) for hd in heads]

    sign_bit = jnp.uint32(0x80000000)

    def scores_to(buf, jb, diag_off):
        k = k_ref[jb]
        for hd in heads:
            z = _dot_nt(_col_block(k, hd), qh[hd])
            if diag_off is not None:
                z = jnp.where((key + diag_off) < qry, z, -jnp.inf)
            neg_abs = lax.bitcast_convert_type(lax.bitcast_convert_type(z, jnp.uint32) | sign_bit, F32)
            fail = jnp.maximum(z, 0.0) + jnp.log2(1.0 + jnp.exp2(neg_abs))
            tail = _dot(suffix, fail.astype(BF16))
            d_refs[buf][hd] = jnp.minimum(z - tail, 0.0)
            tot_refs[buf][hd] = tail[0:1, :]

    def apply_from(buf, jb, carry):
        vt = vt_ref[jb]
        out = []
        for hd in heads:
            later, acc = carry[hd]
            w = jnp.exp2(d_refs[buf][hd] - later)
            out.append((later + tot_refs[buf][hd], acc + _dot(_head_rows(vt, hd), w.astype(BF16))))
        return out

    def rest_is_zero(carry, _):
        least = carry[0][0]
        for hd in heads[1:]:
            least = jnp.minimum(least, carry[hd][0])
        return jnp.min(least) > UNDERFLOW_BITS

    carry = [(jnp.zeros((1, tq), F32), jnp.zeros((HEAD_DIM, tq), F32)) for _ in heads]
    carry = _two_stage_blocks(qi, sub, scores_to, apply_from, carry, upper_diag_first=True,
                              rest_is_zero=rest_is_zero)
    _store_heads(o_ref, [c[1] for c in carry])


def _stick_breaking(proj, v_t, batch, seq, n_heads, first_group):
    t = proj.shape[0]
    width = n_heads * HEAD_DIM
    nb = width // LANES
    tq = min(ATT_T, seq)
    sub = tq // 2
    nq = seq // tq
    nk = seq // sub
    assert v_t.shape[2] == sub
    proj_k = proj.reshape(t // sub, sub, proj.shape[1])
    n_step = ATT_COLS * HEADS_PER_BLOCK
    cols = ATT_COLS * LANES
    ng = nb // ATT_COLS
    score_buf = pltpu.VMEM((n_step, sub, tq), F32)
    total_buf = pltpu.VMEM((n_step, 1, tq), F32)
    q_spec = pl.BlockSpec((tq, cols), lambda b, hp, qi: (b * nq + qi, first_group * ng + hp))
    k_spec = pl.BlockSpec((nk, sub, cols), lambda b, hp, qi: (b, 0, (first_group + 1) * ng + hp))
    v_spec = pl.BlockSpec((nk, cols, sub), lambda b, hp, qi: (b, hp, 0))
    return pl.pallas_call(
        functools.partial(_sb_kernel, tq=tq, sub=sub),
        grid=(batch, ng, nq),
        in_specs=[q_spec, k_spec, v_spec],
        out_specs=pl.BlockSpec((tq, cols), lambda b, hp, qi: (b * nq + qi, hp)),
        out_shape=jax.ShapeDtypeStruct((t, width), BF16),
        scratch_shapes=[score_buf, score_buf, total_buf, total_buf],
        compiler_params=_cparams(("parallel", "parallel", "arbitrary")),
        name="stick_breaking",
    )(proj, proj_k, v_t)


def _forget_cum_kernel(f_ref, b_ref, col_ref, *, seq):
    x = f_ref[...] + b_ref[...]
    log_f = jnp.minimum(x, 0.0) - jnp.log(1.0 + jnp.exp(-jnp.abs(x)))
    xt = log_f.T
    pos = lax.broadcasted_iota(jnp.int32, xt.shape, 1)
    shift = 1
    while shift < seq:
        xt = xt + jnp.where(pos >= shift, pltpu.roll(xt, shift, 1), 0.0)
        shift *= 2
    col_ref[...] = xt.T


def _forget_cum(f_logit, b_forget, batch, seq):
    t = f_logit.shape[0]
    return pl.pallas_call(
        functools.partial(_forget_cum_kernel, seq=seq),
        grid=(batch,),
        in_specs=[pl.BlockSpec((seq, LANES), lambda b: (b, 0)),
                  pl.BlockSpec((1, LANES), lambda b: (0, 0))],
        out_specs=pl.BlockSpec((seq, LANES), lambda b: (b, 0)),
        out_shape=jax.ShapeDtypeStruct((t, LANES), F32),
        compiler_params=_cparams(("parallel",)),
        name="forget_cum",
    )(f_logit, b_forget)


def _bias_lanes(hd):
    return HEAD_DIM * (1 - hd)


def _bias_tiles(cum, hp, is_query):
    n_step = ATT_COLS * HEADS_PER_BLOCK
    pieces = []
    rest = cum * LOG2E
    for _ in range(N_BIAS_PIECES):
        p = rest.astype(BF16)
        pieces.append(p)
        rest = rest - p.astype(F32)
    stacked = jnp.concatenate(pieces, axis=1)
    row = lax.broadcasted_iota(jnp.int32, (N_BIAS_PIECES * LANES, 1), 0)
    piece = row >> 7
    head = (row & (LANES - 1)) - n_step * hp
    first = 0 if is_query else N_BIAS_PIECES
    target = (head >> 1) * LANES + (1 - (head & 1)) * HEAD_DIM + first + piece
    target = jnp.where(jnp.logical_and(head >= 0, head < n_step), target, -1)
    col = lax.broadcasted_iota(jnp.int32, (1, ATT_COLS * LANES), 1)
    selector = jnp.where(col == target, 1.0 if is_query else -1.0, 0.0).astype(BF16)
    tile = _dot(stacked, selector)
    ones_first = N_BIAS_PIECES if is_query else 0
    in_half = col & (HEAD_DIM - 1)
    ones = jnp.logical_and(in_half >= ones_first, in_half < ones_first + N_BIAS_PIECES)
    return jnp.where(ones, 1.0, tile)


def _fox_kernel(q_ref, k_ref, vt_ref, cq_ref, ck_ref, o_ref, kp_ref, vp_ref, ksq_ref, s0_ref, s1_ref,
                max0_ref, max1_ref, *, tq, sub, seq):
    hp = pl.program_id(1)
    qi = pl.program_id(2)
    s_refs = (s0_ref, s1_ref)
    max_refs = (max0_ref, max1_ref)
    nk = seq // sub
    lane, hmasks = _head_masks()
    heads = range(ATT_COLS * HEADS_PER_BLOCK)

    def with_bias_lanes(x, bias_tile, hd):
        return jnp.where(hmasks[hd % HEADS_PER_BLOCK], _col_block(x, hd), _col_block(bias_tile, hd)).astype(BF16)

    @pl.when(qi == 0)
    def _():
        kf = k_ref[...].astype(F32)
        k_bias = _bias_tiles(ck_ref[...], hp, False)
        extra = lax.broadcasted_iota(jnp.int32, (FOX_VALUE_ROWS - HEAD_DIM, sub), 0)
        ones = jnp.where(extra == 0, 1.0, 0.0).astype(BF16)
        for hd in heads:
            side = hd % HEADS_PER_BLOCK
            k_hd = _col_block(kf, hd)
            kp_ref[hd] = with_bias_lanes(kf, k_bias, hd)
            k_sq = jnp.sum(jnp.where(hmasks[side], k_hd * k_hd, 0.0), axis=1, keepdims=True)
            ksq_ref[hd] = jnp.broadcast_to(jnp.max(k_sq, axis=0, keepdims=True), (1, LANES))
            for jb in range(nk):
                vp_ref[hd, jb, 0:HEAD_DIM] = _head_rows(vt_ref[jb], hd)
                vp_ref[hd, jb, HEAD_DIM:FOX_VALUE_ROWS] = ones

    qf = q_ref[...].astype(F32) * (HEAD_DIM ** -0.5 * LOG2E)
    q_bias = _bias_tiles(cq_ref[...], hp, True)
    qh = [with_bias_lanes(qf, q_bias, hd) for hd in heads]
    key = lax.broadcasted_iota(jnp.int32, (sub, tq), 0)
    qry = lax.broadcasted_iota(jnp.int32, (sub, tq), 1)

    pick = lax.broadcasted_iota(jnp.int32, (8, LANES), 0)
    pick_lane = lax.broadcasted_iota(jnp.int32, (8, LANES), 1)
    head_rows = jnp.where(pick == pick_lane // HEAD_DIM, 1.0, 0.0).astype(BF16)
    reach = []
    for hd in heads:
        side = hd % HEADS_PER_BLOCK
        q_hd = _col_block(qf, hd)
        q_sq = _dot_nt(head_rows, (q_hd * q_hd).astype(BF16))[side:side + 1]
        spare = _bias_lanes(side)
        bias_row = jnp.where((pick_lane >= spare) & (pick_lane < spare + N_BIAS_PIECES), 1.0, 0.0).astype(BF16)
        cq_row = _dot_nt(bias_row, qh[hd])[0:1]
        reach.append(NORM_SLACK * jnp.sqrt(q_sq * ksq_ref[hd][:, 0:1]) + cq_row)

    def scores_to(buf, jb, diag_off):
        ks = pl.multiple_of(jb * sub, sub)
        for hd in heads:
            s = _dot_nt(kp_ref[hd, pl.ds(ks, sub), :], qh[hd])
            if diag_off is not None:
                s = jnp.where((key + diag_off) <= qry, s, -jnp.inf)
            s_refs[buf][hd] = s
            max_refs[buf][hd] = jnp.max(s, axis=0, keepdims=True)

    def apply_from(buf, jb, carry):
        out = []
        for hd in heads:
            m, acc = carry[hd]
            m_new = jnp.maximum(m, max_refs[buf][hd])
            p = jnp.exp2(s_refs[buf][hd] - m_new).astype(BF16)
            out.append((m_new, jnp.exp2(m - m_new) * acc + _dot(vp_ref[hd, jb], p)))
        return out

    def rest_is_zero(carry, jb):
        ck_row = ck_ref[pl.ds(jnp.maximum((jb + 1) * sub - 1, 0), 1), :]
        worst = None
        for hd in heads:
            sel = lane == ATT_COLS * HEADS_PER_BLOCK * hp + hd
            ck_last = jnp.sum(jnp.where(sel, ck_row, 0.0), axis=1, keepdims=True) * LOG2E
            gap = reach[hd] - ck_last - carry[hd][0]
            worst = gap if worst is None else jnp.maximum(worst, gap)
        return jnp.max(worst) < -(UNDERFLOW_BITS + FOX_MARGIN_BITS)

    carry = [(jnp.full((1, tq), -jnp.inf, F32), jnp.zeros((FOX_VALUE_ROWS, tq), F32)) for _ in heads]
    carry = _two_stage_blocks(qi, sub, scores_to, apply_from, carry, upper_diag_first=False,
                              rest_is_zero=rest_is_zero)
    normed = []
    for hd in heads:
        acc = carry[hd][1]
        normed.append(acc[0:HEAD_DIM] * (1.0 / acc[HEAD_DIM:HEAD_DIM + 1, :]))
    _store_heads(o_ref, normed)


def _forgetting_attention(proj, v_t, cum_cols, batch, seq, n_heads):
    t = proj.shape[0]
    width = n_heads * HEAD_DIM
    nb = width // LANES
    tq = min(ATT_T, seq)
    sub = tq // 2
    nq = seq // tq
    nk = seq // sub
    assert v_t.shape[2] == sub
    n_step = ATT_COLS * HEADS_PER_BLOCK
    cols = ATT_COLS * LANES
    ng = nb // ATT_COLS
    score_buf = pltpu.VMEM((n_step, sub, tq), F32)
    max_buf = pltpu.VMEM((n_step, 1, tq), F32)
    q_spec = pl.BlockSpec((tq, cols), lambda b, hp, qi: (b * nq + qi, hp))
    k_spec = pl.BlockSpec((seq, cols), lambda b, hp, qi: (b, ng + hp))
    v_spec = pl.BlockSpec((nk, cols, sub), lambda b, hp, qi: (b, hp, 0))
    cq_spec = pl.BlockSpec((tq, LANES), lambda b, hp, qi: (b * nq + qi, 0))
    ck_spec = pl.BlockSpec((seq, LANES), lambda b, hp, qi: (b, 0))
    return pl.pallas_call(
        functools.partial(_fox_kernel, tq=tq, sub=sub, seq=seq),
        grid=(batch, ng, nq),
        in_specs=[q_spec, k_spec, v_spec, cq_spec, ck_spec],
        out_specs=pl.BlockSpec((tq, cols), lambda b, hp, qi: (b * nq + qi, hp)),
        out_shape=jax.ShapeDtypeStruct((t, width), BF16),
        scratch_shapes=[pltpu.VMEM((n_step, seq, LANES), BF16),
                        pltpu.VMEM((n_step, nk, FOX_VALUE_ROWS, sub), BF16),
                        pltpu.VMEM((n_step, 1, LANES), F32),
                        score_buf, score_buf, max_buf, max_buf],
        compiler_params=_cparams(("parallel", "parallel", "arbitrary")),
        name="forgetting_attention",
    )(proj, proj, v_t, cum_cols, cum_cols)


def _swiglu_partial(x, wg_ref, wu_ref, wd_ref):
    g = _dot(x, wg_ref[...])
    u = _dot(x, wu_ref[...])
    a = g * (1.0 / (1.0 + jnp.exp(-g))) * u
    return _dot(a.astype(BF16), wd_ref[...])


def _ffn_dense_kernel(x_ref, wg_ref, wu_ref, wd_ref, h_ref, gain_ref, ho_ref, xn_ref, acc_ref):
    f = pl.program_id(1)

    @pl.when(f == 0)
    def _():
        acc_ref[...] = h_ref[...]

    acc_ref[...] += _swiglu_partial(x_ref[...], wg_ref, wu_ref, wd_ref)

    @pl.when(f == pl.num_programs(1) - 1)
    def _():
        h_new = acc_ref[...]
        ho_ref[...] = h_new
        xn_ref[...] = _rms(h_new, gain_ref[...]).astype(xn_ref.dtype)


def _ffn_dense(xn, wg, wu, wd, h, next_gain):
    t, d = h.shape
    dff = wg.shape[1]
    tm = min(FFN_TM, t)
    tf = dff // FFN_NF
    row = pl.BlockSpec((tm, d), lambda i, f: (i, 0))
    return pl.pallas_call(
        _ffn_dense_kernel,
        grid=(t // tm, FFN_NF),
        in_specs=[row,
                  pl.BlockSpec((d, tf), lambda i, f: (0, f)),
                  pl.BlockSpec((d, tf), lambda i, f: (0, f)),
                  pl.BlockSpec((tf, d), lambda i, f: (f, 0)),
                  row,
                  pl.BlockSpec((1, d), lambda i, f: (0, 0))],
        out_specs=[row, row],
        out_shape=[jax.ShapeDtypeStruct((t, d), F32), jax.ShapeDtypeStruct((t, d), BF16)],
        scratch_shapes=[pltpu.VMEM((tm, d), F32)],
        compiler_params=_cparams(("parallel", "arbitrary")),
        name="ffn_dense",
    )(xn, wg, wu, wd, h, next_gain.reshape(1, d))


def _ffn_grouped_kernel(te_ref, nu_ref, x_ref, wg_ref, wu_ref, wd_ref, o_ref, acc_ref):
    i = pl.program_id(0)
    f = pl.program_id(1)
    used = i < nu_ref[0]

    @pl.when(jnp.logical_and(used, f == 0))
    def _():
        acc_ref[...] = jnp.zeros_like(acc_ref)

    @pl.when(used)
    def _():
        acc_ref[...] += _swiglu_partial(x_ref[...].astype(BF16), wg_ref, wu_ref, wd_ref)

    @pl.when(f == pl.num_programs(1) - 1)
    def _():
        @pl.when(used)
        def _():
            o_ref[...] = acc_ref[...]

        @pl.when(jnp.logical_not(used))
        def _():
            o_ref[...] = jnp.zeros_like(o_ref)


def _ffn_grouped(x_sorted, wg, wu, wd, tile_expert, n_used):
    r, d = x_sorted.shape
    dff = wg.shape[2]
    tm = FFN_TM
    tf = dff // FFN_NF
    grid_spec = pltpu.PrefetchScalarGridSpec(
        num_scalar_prefetch=2,
        grid=(r // tm, FFN_NF),
        in_specs=[pl.BlockSpec((tm, d), lambda i, f, te, nu: (i, 0)),
                  pl.BlockSpec((None, d, tf), lambda i, f, te, nu: (te[i], 0, f)),
                  pl.BlockSpec((None, d, tf), lambda i, f, te, nu: (te[i], 0, f)),
                  pl.BlockSpec((None, tf, d), lambda i, f, te, nu: (te[i], f, 0))],
        out_specs=pl.BlockSpec((tm, d), lambda i, f, te, nu: (i, 0)),
        scratch_shapes=[pltpu.VMEM((tm, d), F32)],
    )
    return pl.pallas_call(
        _ffn_grouped_kernel,
        grid_spec=grid_spec,
        out_shape=jax.ShapeDtypeStruct((r, d), F32),
        compiler_params=_cparams(("arbitrary", "arbitrary")),
        name="ffn_grouped",
    )(tile_expert, n_used, x_sorted, wg, wu, wd)


def _route(xn, wr_ref, route_ref):
    logits = jnp.dot(xn, wr_ref[...], precision=lax.Precision.HIGHEST, preferred_element_type=F32)
    lane = lax.broadcasted_iota(jnp.int32, logits.shape, 1)
    lane_f = lane.astype(F32)
    lg = jnp.where(lane < N_EXPERTS, logits, -jnp.inf)
    v1 = jnp.max(lg, axis=1, keepdims=True)
    i1 = jnp.min(jnp.where(lg == v1, lane_f, float(LANES)), axis=1, keepdims=True)
    lg2 = jnp.where(lane_f == i1, -jnp.inf, lg)
    v2 = jnp.max(lg2, axis=1, keepdims=True)
    i2 = jnp.min(jnp.where(lg2 == v2, lane_f, float(LANES)), axis=1, keepdims=True)
    e2 = jnp.exp(v2 - v1)
    g1 = 1.0 / (1.0 + e2)
    g2 = e2 / (1.0 + e2)
    route_ref[...] = jnp.where(lane == 0, i1, jnp.where(lane == 1, i2,
                               jnp.where(lane == 2, g1, jnp.where(lane == 3, g2, 0.0))))


def _outproj_router_kernel(y_ref, w_ref, h_ref, gain_ref, wr_ref, ho_ref, xn_ref, route_ref):
    h_new = h_ref[...] + _dot(y_ref[...], w_ref[...])
    ho_ref[...] = h_new
    xn = _rms(h_new, gain_ref[...])
    xn_ref[...] = xn
    _route(xn, wr_ref, route_ref)


def _outproj_router(y, w, h, gain, w_router):
    t, d = h.shape
    tm = min(ROUTER_TM, t)
    wr = jnp.zeros((d, LANES), F32).at[:, :N_EXPERTS].set(w_router)
    row = pl.BlockSpec((tm, d), lambda i: (i, 0))
    return pl.pallas_call(
        _outproj_router_kernel,
        grid=(t // tm,),
        in_specs=[pl.BlockSpec((tm, y.shape[1]), lambda i: (i, 0)),
                  pl.BlockSpec(w.shape, lambda i: (0, 0)),
                  row,
                  pl.BlockSpec((1, d), lambda i: (0, 0)),
                  pl.BlockSpec((d, LANES), lambda i: (0, 0))],
        out_specs=[row, row, pl.BlockSpec((tm, LANES), lambda i: (i, 0))],
        out_shape=[jax.ShapeDtypeStruct((t, d), F32), jax.ShapeDtypeStruct((t, d), F32),
                   jax.ShapeDtypeStruct((t, LANES), F32)],
        compiler_params=_cparams(("parallel",)),
        name="outproj_router",
    )(y, w, h, gain.reshape(1, d), wr)


def _scatter_kernel(dest_ref, x_ref, init_hbm, o_hbm, sem, *, tm):
    del init_hbm

    def row_copy(r, kk):
        return pltpu.make_async_copy(x_ref.at[pl.ds(r, 1), :],
                                     o_hbm.at[pl.ds(dest_ref[0, 0, TOP_K * r + kk], 1), :], sem)

    def issue(r, carry):
        for kk in range(TOP_K):
            row_copy(r, kk).start(priority=kk)
        return carry

    lax.fori_loop(0, tm, issue, 0, unroll=DMA_UNROLL)

    for _ in range(TOP_K):
        pltpu.make_async_copy(x_ref, o_hbm.at[pl.ds(0, tm), :], sem).wait()


def _scatter_rows(xn, dest, n_rows):
    t, d = xn.shape
    tm = min(GATHER_TM, t)
    nt = t // tm
    dest3 = dest.reshape(nt, 1, TOP_K * tm)
    init = jnp.zeros((n_rows, d), F32)
    return pl.pallas_call(
        functools.partial(_scatter_kernel, tm=tm),
        grid=(nt,),
        in_specs=[pl.BlockSpec((1, 1, TOP_K * tm), lambda i: (i, 0, 0), memory_space=pltpu.SMEM),
                  pl.BlockSpec((tm, d), lambda i: (i, 0)),
                  pl.BlockSpec(memory_space=pl.ANY)],
        out_specs=pl.BlockSpec(memory_space=pl.ANY),
        out_shape=jax.ShapeDtypeStruct((n_rows, d), F32),
        scratch_shapes=[pltpu.SemaphoreType.DMA(())],
        input_output_aliases={2: 0},
        compiler_params=_cparams(("arbitrary",)),
        name="moe_scatter",
    )(dest3, xn, init)


def _combine_kernel(dcur_ref, dnext_ref, y_hbm, h_ref, route_ref, gain_ref, o_ref, buf, sems, *, tm):
    i = pl.program_id(0)
    n = pl.num_programs(0)
    slot = i % 2

    def row_copy(d_ref, s, r, kk):
        return pltpu.make_async_copy(y_hbm.at[pl.ds(d_ref[0, 0, TOP_K * r + kk], 1), :],
                                     buf.at[s, kk, pl.ds(r, 1), :], sems.at[s])

    def issue(d_ref, s):
        def body(r, carry):
            for kk in range(TOP_K):
                row_copy(d_ref, s, r, kk).start(priority=kk)
            return carry
        lax.fori_loop(0, tm, body, 0, unroll=DMA_UNROLL)

    @pl.when(i == 0)
    def _():
        issue(dcur_ref, slot)

    @pl.when(i + 1 < n)
    def _():
        issue(dnext_ref, 1 - slot)

    for kk in range(TOP_K):
        pltpu.make_async_copy(y_hbm.at[pl.ds(0, tm), :], buf.at[slot, kk], sems.at[slot]).wait()

    route = route_ref[...]
    lane = lax.broadcasted_iota(jnp.int32, route.shape, 1)
    g1 = jnp.sum(jnp.where(lane == 2, route, 0.0), axis=1, keepdims=True)
    g2 = jnp.sum(jnp.where(lane == 3, route, 0.0), axis=1, keepdims=True)
    out = h_ref[...] + (g1 * buf[slot, 0] + g2 * buf[slot, 1])
    o_ref[...] = _rms(out, gain_ref[...])


def _combine(y_sorted, dest, h, route, final_gain):
    t, d = h.shape
    tm = min(GATHER_TM, t)
    nt = t // tm
    dest3 = dest.reshape(nt, 1, TOP_K * tm)
    row = pl.BlockSpec((tm, d), lambda i: (i, 0))
    smem = functools.partial(pl.BlockSpec, (1, 1, TOP_K * tm), memory_space=pltpu.SMEM)
    return pl.pallas_call(
        functools.partial(_combine_kernel, tm=tm),
        grid=(nt,),
        in_specs=[smem(index_map=lambda i: (i, 0, 0)),
                  smem(index_map=lambda i: (jnp.minimum(i + 1, nt - 1), 0, 0)),
                  pl.BlockSpec(memory_space=pl.ANY),
                  row,
                  pl.BlockSpec((tm, LANES), lambda i: (i, 0)),
                  pl.BlockSpec((1, d), lambda i: (0, 0))],
        out_specs=row,
        out_shape=jax.ShapeDtypeStruct((t, d), F32),
        scratch_shapes=[pltpu.VMEM((2, TOP_K, tm, d), F32), pltpu.SemaphoreType.DMA((2,))],
        compiler_params=_cparams(("arbitrary",)),
        name="moe_combine",
    )(dest3, dest3, y_sorted, h, route, final_gain.reshape(1, d))


def _routing_plan(expert_idx, tm):
    t = expert_idx.shape[0]
    flat = expert_idx.reshape(-1)
    onehot = (flat[:, None] == jnp.arange(N_EXPERTS, dtype=jnp.int32)[None, :]).astype(jnp.int32)
    running = jnp.cumsum(onehot, axis=0)
    rank = jnp.sum((running - onehot) * onehot, axis=1)
    counts = running[-1]
    tiles = (counts + tm - 1) // tm
    tile_end = jnp.cumsum(tiles)
    group_start = (tile_end - tiles) * tm
    dest = jnp.sum(onehot * group_start[None, :], axis=1) + rank
    n_tiles = (t * TOP_K) // tm + N_EXPERTS
    n_used = tile_end[-1]
    tile_id = jnp.minimum(jnp.arange(n_tiles, dtype=jnp.int32), n_used - 1)
    tile_expert = jnp.sum((tile_id[:, None] >= tile_end[None, :]).astype(jnp.int32), axis=1)
    return dest.reshape(t, TOP_K).astype(jnp.int32), tile_expert.astype(jnp.int32), \
        n_used.reshape(1).astype(jnp.int32), n_tiles * tm


def kernel(x, attn_norm_even, w_in_even, ret_norm_even, w_out_even, ffn_norm_even, w_gate_even, w_up_even, w_down_even, attn_norm_odd, w_in_odd, b_forget_odd, w_out_odd, ffn_norm_odd, w_router_odd, w_gate_moe_odd, w_up_moe_odd, w_down_moe_odd, final_norm):
    batch, seq, d = x.shape
    t = batch * seq
    n_ret = d // (2 * HEAD_DIM)
    n_sb = d // (2 * HEAD_DIM)
    n_fox = d // HEAD_DIM
    ret_width = n_ret * HEAD_DIM
    sb_width = n_sb * HEAD_DIM
    fox_width = n_fox * HEAD_DIM
    sub = min(ATT_T, seq) // 2
    h0 = x.reshape(t, d)

    w_in = w_in_even[0]
    n_direct = 4 * ret_width + 2 * sb_width
    proj, xn0 = _norm_mm(h0, attn_norm_even[0], w_in[:, :n_direct].astype(BF16), BF16)
    v_sb_t = _mm_nt(xn0, w_in[:, n_direct:].T.astype(BF16), sub)
    y_ret = _retention(proj, ret_norm_even[0], batch, seq, n_ret)
    y_sb = _stick_breaking(proj, v_sb_t, batch, seq, n_sb, first_group=4)
    w_out = w_out_even[0].astype(BF16)
    h1, xn1 = _outproj([y_ret, y_sb], [w_out[:ret_width], w_out[ret_width:]], h0, ffn_norm_even[0])
    h2, xn2 = _ffn_dense(xn1, w_gate_even[0].astype(BF16), w_up_even[0].astype(BF16),
                         w_down_even[0].astype(BF16), h1, attn_norm_odd[0])

    w_in = w_in_odd[0]
    proj = _mm(xn2, w_in[:, :2 * fox_width].astype(BF16), BF16)
    v_fox_t = _mm_nt(xn2, w_in[:, 2 * fox_width:3 * fox_width].T.astype(BF16), sub)
    w_f = jnp.zeros((d, LANES), BF16).at[:, :n_fox].set(w_in[:, 3 * fox_width:].astype(BF16))
    f_logit = _mm(xn2, w_f, F32)
    b_f = jnp.zeros((1, LANES), F32).at[0, :n_fox].set(b_forget_odd[0])
    cum_cols = _forget_cum(f_logit, b_f, batch, seq)
    y_fox = _forgetting_attention(proj, v_fox_t, cum_cols, batch, seq, n_fox)
    h3, xn3, route = _outproj_router(y_fox, w_out_odd[0].astype(BF16), h2, ffn_norm_odd[0], w_router_odd[0])
    expert_idx = route[:, :TOP_K].astype(jnp.int32)
    dest, tile_expert, n_used, n_rows = _routing_plan(expert_idx, FFN_TM)
    x_sorted = _scatter_rows(xn3, dest, n_rows)
    y_sorted = _ffn_grouped(x_sorted, w_gate_moe_odd[0].astype(BF16), w_up_moe_odd[0].astype(BF16),
                            w_down_moe_odd[0].astype(BF16), tile_expert, n_used)
    out = _combine(y_sorted, dest, h3, route, final_norm)
    return out.reshape(batch, seq, d)
```
